```python
import math
import jax, jax.numpy as jnp
from jax import lax
import numpy as np

D_MODEL = 1024
BATCH = 32
SEQ = 2048
DEPTH = 4
DEC_BATCH = 16
DEC_SEQ = 64
PAST_LEN = 2048

CHUNK = 64
N_MIXERS = 3
N_RET = (DEPTH + 2) // 3
N_POOL = (DEPTH + 1) // 3
N_ATT = DEPTH // 3
D_FF = 2816
EPS = 1e-6
RET_HEADS = 4
RET_DK = D_MODEL // RET_HEADS
RET_DV = 2 * D_MODEL // RET_HEADS
ROPE_BASE = 10000.0
POOL_WINDOWS = (2, 4, 8, 16)
POOL_GROUPS = 4
POOL_GC = D_MODEL // POOL_GROUPS
POOL_BUF = 15
ATT_HEADS = 16
ATT_HD = D_MODEL // ATT_HEADS
LEFT_CHUNKS = 8
BAND_ROWS = LEFT_CHUNKS * CHUNK
REL_MIN = -(CHUNK - 1)
REL_MAX = 256
REL_SIZE = REL_MAX - REL_MIN + 1
NEG_INF = -1e30

kernel_name = 'hybrid_streaming_encoder_step'

F32 = jnp.float32


def rmsnorm(x, g):
    xf = x.astype(F32)
    y = xf * lax.rsqrt(jnp.mean(xf * xf, axis=-1, keepdims=True) + EPS) * g.astype(F32)
    return y.astype(x.dtype)


def swiglu(h, w_in, w_out):
    gate, up = jnp.split(h @ w_in, 2, axis=-1)
    return (jax.nn.silu(gate) * up) @ w_out


def rotary(x, pos):
    half = x.shape[-1] // 2
    inv = ROPE_BASE ** (-jnp.arange(half, dtype=F32) / half)
    ang = pos.astype(F32)[:, None] * inv[None, :]
    cos = jnp.cos(ang)[None, :, None, :]
    sin = jnp.sin(ang)[None, :, None, :]
    xf = x.astype(F32)
    x1, x2 = xf[..., :half], xf[..., half:]
    return jnp.concatenate([x1 * cos - x2 * sin, x1 * sin + x2 * cos], axis=-1)


def retention(h, s0, pos0, w_in, w_out, gn_g):
    b, t, _ = h.shape
    lc = min(CHUNK, t)
    nc = t // lc
    q, k, v, g = jnp.split(h @ w_in, [D_MODEL, 2 * D_MODEL, 4 * D_MODEL], axis=-1)
    pos = pos0 + jnp.arange(t)
    q = rotary(q.reshape(b, t, RET_HEADS, RET_DK), pos)
    k = rotary(k.reshape(b, t, RET_HEADS, RET_DK), pos) * (RET_DK ** -0.5)
    v = v.reshape(b, t, RET_HEADS, RET_DV).astype(F32)

    def to_chunks(a):
        return a.reshape(b, nc, lc, RET_HEADS, a.shape[-1]).transpose(1, 0, 3, 2, 4)

    gamma = 1.0 - 2.0 ** (-5.0 - jnp.arange(RET_HEADS, dtype=F32))
    idx = jnp.arange(lc, dtype=F32)
    diff = idx[:, None] - idx[None, :]
    decay = jnp.where(diff >= 0, gamma[:, None, None] ** jnp.maximum(diff, 0.0), 0.0)
    q_dec = gamma[:, None] ** (idx + 1.0)
    k_dec = gamma[:, None] ** (lc - 1.0 - idx)
    c_dec = gamma ** float(lc)

    def step(s, qkv):
        qc, kc, vc = qkv
        inner = jnp.einsum('bhid,bhjd->bhij', qc, kc) * decay
        o = jnp.einsum('bhij,bhje->bhie', inner, vc) + jnp.einsum('bhid,bhde->bhie', qc * q_dec[:, :, None], s)
        s = s * c_dec[:, None, None] + jnp.einsum('bhjd,bhje->bhde', kc * k_dec[:, :, None], vc)
        return s, o

    s_fin, o = lax.scan(step, s0.astype(F32), (to_chunks(q), to_chunks(k), to_chunks(v)))
    o = o.transpose(1, 0, 3, 2, 4).reshape(b, t, RET_HEADS, RET_DV)
    mu = jnp.mean(o, axis=-1, keepdims=True)
    var = jnp.mean(jnp.square(o - mu), axis=-1, keepdims=True)
    o = (o - mu) * lax.rsqrt(var + EPS) * gn_g.astype(F32)
    y = (jax.nn.silu(g) * o.reshape(b, t, 2 * D_MODEL).astype(h.dtype)) @ w_out
    return y, s_fin


def pool_mixer(h, buf, pos0, w_grp, b_grp, scale):
    bsz, t, _ = h.shape
    ext = jnp.concatenate([buf.astype(h.dtype), h], axis=1)
    cs = jnp.cumsum(ext.astype(F32), axis=1)
    cs = jnp.concatenate([jnp.zeros((bsz, 1, D_MODEL), F32), cs], axis=1)
    pos = pos0 + jnp.arange(t)
    upper = cs[:, POOL_BUF + 1:POOL_BUF + 1 + t]
    means = []
    for gi, w in enumerate(POOL_WINDOWS):
        ch = slice(gi * POOL_GC, (gi + 1) * POOL_GC)
        wsum = upper[..., ch] - cs[:, POOL_BUF + 1 - w:POOL_BUF + 1 - w + t, ch]
        cnt = jnp.minimum(pos + 1, w).astype(F32)[None, :, None]
        means.append(wsum / cnt)
    pooled = jnp.stack(means, axis=2) - h.astype(F32).reshape(bsz, t, POOL_GROUPS, POOL_GC)
    y = jnp.einsum('btgc,gce->btge', pooled.astype(h.dtype), w_grp).reshape(bsz, t, D_MODEL) + b_grp
    return y * scale, ext[:, -POOL_BUF:]


def att_project(h, w_qkv, q_g, k_g):
    b, t, _ = h.shape
    q, k, v = jnp.split(h @ w_qkv, 3, axis=-1)
    q = rmsnorm(q.reshape(b, t, ATT_HEADS, ATT_HD), q_g)
    k = rmsnorm(k.reshape(b, t, ATT_HEADS, ATT_HD), k_g)
    return q, k, v.reshape(b, t, ATT_HEADS, ATT_HD)


def band_attend(q, k, v, q_pos, k_pos, rel_bias):
    s = jnp.einsum('bqhd,bkhd->bhqk', q, k).astype(F32) * (ATT_HD ** -0.5)
    rel = jnp.clip(q_pos[:, None] - k_pos[None, :], REL_MIN, REL_MAX) - REL_MIN
    s = s + rel_bias[:, rel].astype(F32)[None]
    qc = jnp.floor_divide(q_pos, CHUNK)[:, None]
    kc = jnp.floor_divide(k_pos, CHUNK)[None, :]
    ok = (k_pos[None, :] >= 0) & (kc <= qc) & (kc >= qc - LEFT_CHUNKS)
    s = jnp.where(ok[None, None], s, NEG_INF)
    p = jax.nn.softmax(s, axis=-1)
    return jnp.einsum('bhqk,bkhd->bqhd', p.astype(v.dtype), v)


def chunk_attention_prompt(q, k, v, rel_bias):
    b, t = q.shape[:2]
    nc = t // CHUNK
    pad = ((0, 0), (BAND_ROWS, 0), (0, 0), (0, 0))
    kp = jnp.pad(k, pad)
    vp = jnp.pad(v, pad)
    band = BAND_ROWS + CHUNK

    def one(n):
        start = n * CHUNK
        qn = lax.dynamic_slice_in_dim(q, start, CHUNK, axis=1)
        kn = lax.dynamic_slice_in_dim(kp, start, band, axis=1)
        vn = lax.dynamic_slice_in_dim(vp, start, band, axis=1)
        q_pos = start + jnp.arange(CHUNK)
        k_pos = start - BAND_ROWS + jnp.arange(band)
        return band_attend(qn, kn, vn, q_pos, k_pos, rel_bias)

    o = lax.map(one, jnp.arange(nc))
    return o.transpose(1, 0, 2, 3, 4).reshape(b, t, ATT_HEADS, ATT_HD)


def trunk(x, c, pos0, ret_states, pool_bufs, att_k_cache, att_v_cache, p):
    b, t, _ = x.shape
    new_ret, new_pool, new_k, new_v = [], [], [], []
    c_act = jax.nn.silu(c)
    for i in range(DEPTH):
        mod = c_act @ p['ada_w'][i] + p['ada_b'][i]
        sh1, sc1, g1, shm, scm, gm, sh2, sc2, g2 = [m[:, None, :] for m in jnp.split(mod, 9, axis=-1)]
        h = rmsnorm(x, p['norm_g'][i, 0]) * (1.0 + sc1) + sh1
        x = x + 0.5 * g1 * swiglu(h, p['ffn1_w_in'][i], p['ffn1_w_out'][i])
        h = rmsnorm(x, p['norm_g'][i, 1]) * (1.0 + scm) + shm
        j = i // N_MIXERS
        kind = i % N_MIXERS
        if kind == 0:
            y, s_new = retention(h, ret_states[j], pos0, p['ret_w_in'][j], p['ret_w_out'][j], p['ret_gn_g'][j])
            new_ret.append(s_new.astype(x.dtype))
        elif kind == 1:
            y, buf_new = pool_mixer(h, pool_bufs[j], pos0, p['pool_w'][j], p['pool_b'][j], p['pool_scale'][j])
            new_pool.append(buf_new)
        else:
            q, k, v = att_project(h, p['att_w_qkv'][j], p['att_q_g'][j], p['att_k_g'][j])
            rb = p['att_rel_bias'][j]
            if att_k_cache is None:
                o = chunk_attention_prompt(q, k, v, rb)
                keep = min(BAND_ROWS, t)
                new_k.append(k[:, t - keep:])
                new_v.append(v[:, t - keep:])
            else:
                ck = att_k_cache[j].astype(k.dtype)
                cv = att_v_cache[j].astype(v.dtype)
                wc = ck.shape[1]
                kb = jnp.concatenate([ck, k], axis=1)
                vb = jnp.concatenate([cv, v], axis=1)
                k_pos = jnp.concatenate([pos0 - wc + jnp.arange(wc), pos0 + jnp.arange(t)])
                q_pos = pos0 + jnp.arange(t)
                o = band_attend(q, kb, vb, q_pos, k_pos, rb)
                new_k.append(k)
                new_v.append(v)
            y = o.reshape(b, t, D_MODEL) @ p['att_w_o'][j]
        x = x + gm * y
        h = rmsnorm(x, p['norm_g'][i, 2]) * (1.0 + sc2) + sh2
        x = x + 0.5 * g2 * swiglu(h, p['ffn2_w_in'][i], p['ffn2_w_out'][i])
    return x, jnp.stack(new_ret), jnp.stack(new_pool), jnp.stack(new_k), jnp.stack(new_v)


def setup_inputs(seed: int = 0) -> dict:
    key = jax.random.key(seed)
    ks = jax.random.split(key, 26)

    def nrm(k, shape, scale):
        return jax.random.normal(k, shape, F32) * scale

    wc = min(BAND_ROWS, PAST_LEN)
    d = D_MODEL
    return {
        'x_prompt': nrm(ks[0], (BATCH, SEQ, d), 1.0),
        'x_sample': nrm(ks[1], (DEC_BATCH, DEC_SEQ, d), 1.0),
        'c_prompt': nrm(ks[2], (BATCH, d), 1.0),
        'c_sample': nrm(ks[3], (DEC_BATCH, d), 1.0),
        'state_ret': nrm(ks[4], (N_RET, DEC_BATCH, RET_HEADS, RET_DK, RET_DV), 0.5),
        'state_pool': nrm(ks[5], (N_POOL, DEC_BATCH, POOL_BUF, d), 1.0),
        'cache_att_k': nrm(ks[6], (N_ATT, DEC_BATCH, wc, ATT_HEADS, ATT_HD), 1.0),
        'cache_att_v': nrm(ks[7], (N_ATT, DEC_BATCH, wc, ATT_HEADS, ATT_HD), 1.0),
        'norm_g': 1.0 + nrm(ks[8], (DEPTH, 3, d), 0.02),
        'ada_w': nrm(ks[9], (DEPTH, d, 9 * d), 0.5 * d ** -0.5),
        'ada_b': nrm(ks[10], (DEPTH, 9 * d), 0.02),
        'ffn1_w_in': nrm(ks[11], (DEPTH, d, 2 * D_FF), d ** -0.5),
        'ffn1_w_out': nrm(ks[12], (DEPTH, D_FF, d), D_FF ** -0.5),
        'ffn2_w_in': nrm(ks[13], (DEPTH, d, 2 * D_FF), d ** -0.5),
        'ffn2_w_out': nrm(ks[14], (DEPTH, D_FF, d), D_FF ** -0.5),
        'ret_w_in': nrm(ks[15], (N_RET, d, 6 * d), d ** -0.5),
        'ret_w_out': nrm(ks[16], (N_RET, 2 * d, d), (2 * d) ** -0.5),
        'ret_gn_g': 1.0 + nrm(ks[17], (N_RET, RET_HEADS, RET_DV), 0.02),
        'pool_w': nrm(ks[18], (N_POOL, POOL_GROUPS, POOL_GC, POOL_GC), POOL_GC ** -0.5),
        'pool_b': nrm(ks[19], (N_POOL, d), 0.02),
        'pool_scale': 1.0 + nrm(ks[20], (N_POOL, d), 0.02),
        'att_w_qkv': nrm(ks[21], (N_ATT, d, 3 * d), d ** -0.5),
        'att_w_o': nrm(ks[22], (N_ATT, d, d), d ** -0.5),
        'att_q_g': 1.0 + nrm(ks[23], (N_ATT, ATT_HD), 0.02),
        'att_k_g': 1.0 + nrm(ks[24], (N_ATT, ATT_HD), 0.02),
        'att_rel_bias': nrm(ks[25], (N_ATT, ATT_HEADS, REL_SIZE), 0.5),
    }


def reference(x_prompt, x_sample, c_prompt, c_sample, state_ret, state_pool, cache_att_k, cache_att_v,
              norm_g, ada_w, ada_b, ffn1_w_in, ffn1_w_out, ffn2_w_in, ffn2_w_out,
              ret_w_in, ret_w_out, ret_gn_g, pool_w, pool_b, pool_scale,
              att_w_qkv, att_w_o, att_q_g, att_k_g, att_rel_bias):
    p = {
        'norm_g': norm_g, 'ada_w': ada_w, 'ada_b': ada_b,
        'ffn1_w_in': ffn1_w_in, 'ffn1_w_out': ffn1_w_out,
        'ffn2_w_in': ffn2_w_in, 'ffn2_w_out': ffn2_w_out,
        'ret_w_in': ret_w_in, 'ret_w_out': ret_w_out, 'ret_gn_g': ret_gn_g,
        'pool_w': pool_w, 'pool_b': pool_b, 'pool_scale': pool_scale,
        'att_w_qkv': att_w_qkv, 'att_w_o': att_w_o, 'att_q_g': att_q_g, 'att_k_g': att_k_g,
        'att_rel_bias': att_rel_bias,
    }
    bp = x_prompt.shape[0]
    ret0 = jnp.zeros((N_RET, bp, RET_HEADS, RET_DK, RET_DV), F32)
    pool0 = jnp.zeros((N_POOL, bp, POOL_BUF, D_MODEL), x_prompt.dtype)
    y_prompt, ret_p, pool_p, k_p, v_p = trunk(x_prompt, c_prompt, 0, ret0, pool0, None, None, p)
    y_sample, ret_s, pool_s, k_s, v_s = trunk(x_sample, c_sample, PAST_LEN, state_ret, state_pool,
                                              cache_att_k, cache_att_v, p)
    return (y_prompt, y_sample, ret_p, ret_s, pool_p, pool_s, k_p, v_p, k_s, v_s)
```

```python
import functools

import numpy as np
import jax
import jax.numpy as jnp
from jax import lax
from jax.experimental import pallas as pl
from jax.experimental.pallas import tpu as pltpu

F32 = jnp.float32
BF16 = jnp.bfloat16

D_MODEL = 1024
DEPTH = 4
D_FF = 2816
EPS = 1e-6
CHUNK = 64
PAST_LEN = 2048
N_MIXERS = 3
RET_HEADS = 4
RET_DK = D_MODEL // RET_HEADS
RET_DV = 2 * D_MODEL // RET_HEADS
ROPE_BASE = 10000.0
ROPE_HALF = RET_DK // 2
POOL_WINDOWS = (2, 4, 8, 16)
POOL_GROUPS = 4
POOL_GC = D_MODEL // POOL_GROUPS
POOL_BUF = 15
POOL_CARRY = 16
ATT_HEADS = 16
ATT_HD = D_MODEL // ATT_HEADS
ATT_GROUP = 4
ATT_GW = ATT_GROUP * ATT_HD
ATT_NGROUPS = ATT_HEADS // ATT_GROUP
LEFT_CHUNKS = 8
BAND_ROWS = LEFT_CHUNKS * CHUNK
BAND = BAND_ROWS + CHUNK
REL_MIN = -(CHUNK - 1)
REL_MAX = 256
NEG_INF = -1e30
N_MOD = 9

VMEM_LIMIT_BYTES = 56 * 1024 * 1024
PROMPT_ROWS = 512
RET_CHUNK_PROMPT = 256


def _cparams(n_grid):
    return pltpu.CompilerParams(
        dimension_semantics=("arbitrary",) * n_grid,
        vmem_limit_bytes=VMEM_LIMIT_BYTES,
    )


def _resident(block_shape, index_map):
    return pl.BlockSpec(block_shape, index_map, pipeline_mode=pl.Buffered(1))


def _norm_mod(x, ng, sc, sh):
    ms = jnp.mean(x * x, axis=-1, keepdims=True)
    y = x * lax.rsqrt(ms + EPS) * ng
    return y * (1.0 + sc) + sh


def _silu(x):
    return x * (1.0 / (1.0 + jnp.exp(-x)))


def _ada_kernel(c_ref, w_ref, b_ref, o_ref):
    c = _silu(c_ref[...]).astype(BF16)
    w = w_ref[...].astype(BF16)
    o_ref[...] = jnp.dot(c, w, preferred_element_type=F32) + b_ref[...]


def _ada(c_all, ada_w, ada_b):
    n = c_all.shape[0]
    b4 = ada_b.reshape(DEPTH, N_MOD, 1, D_MODEL)
    out = pl.pallas_call(
        _ada_kernel,
        grid=(DEPTH, N_MOD),
        in_specs=[
            pl.BlockSpec((n, D_MODEL), lambda i, j: (0, 0)),
            pl.BlockSpec((None, D_MODEL, D_MODEL), lambda i, j: (i, 0, j)),
            pl.BlockSpec((None, None, 1, D_MODEL), lambda i, j: (i, j, 0, 0)),
        ],
        out_specs=pl.BlockSpec((None, None, n, D_MODEL), lambda i, j: (i, j, 0, 0)),
        out_shape=jax.ShapeDtypeStruct((DEPTH, N_MOD, n, D_MODEL), F32),
        compiler_params=_cparams(2),
        name="ada_mod",
    )(c_all, ada_w, b4)
    return out.reshape(DEPTH, N_MOD, n, 1, D_MODEL)


def _mod_spec(layer, j, bb, row0):
    blk0 = row0 // bb
    return pl.BlockSpec((None, None, bb, 1, D_MODEL),
                        lambda b, t: (layer, j, blk0 + b, 0, 0))


def _ng_spec(layer, j):
    return pl.BlockSpec((None, None, 1, D_MODEL), lambda b, t: (layer, j, 0, 0))


def _x_spec(bb, tt, width=D_MODEL):
    return pl.BlockSpec((bb, tt, width), lambda b, t: (b, t, 0))


def _ffn_kernel(x_ref, sh_ref, sc_ref, g_ref, ng_ref, win_ref, wout_ref, o_ref):
    bb, tt, d = x_ref.shape
    x = x_ref[...]
    h = _norm_mod(x, ng_ref[...], sc_ref[...], sh_ref[...])
    hb = h.reshape(bb * tt, d).astype(BF16)
    gu = jnp.dot(hb, win_ref[...], preferred_element_type=F32)
    act = (_silu(gu[:, :D_FF]) * gu[:, D_FF:]).astype(BF16)
    y = jnp.dot(act, wout_ref[...], preferred_element_type=F32)
    o_ref[...] = x + (0.5 * g_ref[...]) * y.reshape(bb, tt, d)


def _ffn(x, mod, norm_g4, w_in, w_out, layer, which, bb, tt, row0):
    b, t, d = x.shape
    j0 = 0 if which == 1 else 6
    nj = 0 if which == 1 else 2
    return pl.pallas_call(
        _ffn_kernel,
        grid=(b // bb, t // tt),
        in_specs=[
            _x_spec(bb, tt),
            _mod_spec(layer, j0, bb, row0),
            _mod_spec(layer, j0 + 1, bb, row0),
            _mod_spec(layer, j0 + 2, bb, row0),
            _ng_spec(layer, nj),
            _resident((None, D_MODEL, 2 * D_FF), lambda b_, t_: (layer, 0, 0)),
            _resident((None, D_FF, D_MODEL), lambda b_, t_: (layer, 0, 0)),
        ],
        out_specs=_x_spec(bb, tt),
        out_shape=jax.ShapeDtypeStruct(x.shape, F32),
        compiler_params=_cparams(2),
        name=f"ffn{which}_l{layer}",
    )(x, mod, mod, mod, norm_g4, w_in, w_out)


def _matres_kernel(a_ref, x_ref, g_ref, w_ref, o_ref):
    bb, tt, k = a_ref.shape
    y = jnp.dot(a_ref[...].reshape(bb * tt, k), w_ref[...], preferred_element_type=F32)
    o_ref[...] = x_ref[...] + g_ref[...] * y.reshape(bb, tt, D_MODEL)


def _matres(a, x, mod, w, layer, widx, bb, tt, row0, name):
    b, t, k = a.shape
    return pl.pallas_call(
        _matres_kernel,
        grid=(b // bb, t // tt),
        in_specs=[
            _x_spec(bb, tt, k),
            _x_spec(bb, tt),
            _mod_spec(layer, 5, bb, row0),
            _resident((None, k, D_MODEL), lambda b_, t_: (widx, 0, 0)),
        ],
        out_specs=_x_spec(bb, tt),
        out_shape=jax.ShapeDtypeStruct(x.shape, F32),
        compiler_params=_cparams(2),
        name=name,
    )(a, x, mod, w)


def _ret_proj_kernel(x_ref, sh_ref, sc_ref, ng_ref, w_ref, cos_ref, sin_ref,
                     q_ref, k_ref, v_ref, g_ref):
    bb, tt, d = x_ref.shape
    h = _norm_mod(x_ref[...], ng_ref[...], sc_ref[...], sh_ref[...])
    hb = h.reshape(bb * tt, d).astype(BF16)
    cos = cos_ref[...][None]
    sin = sin_ref[...][None]
    for idx, (o_ref, scale) in enumerate(((q_ref, 1.0), (k_ref, RET_DK ** -0.5))):
        y = jnp.dot(hb, w_ref[:, idx * d:(idx + 1) * d], preferred_element_type=F32)
        y = y.reshape(bb, tt, d)
        for hd in range(RET_HEADS):
            lo = hd * RET_DK
            x1 = y[:, :, lo:lo + ROPE_HALF]
            x2 = y[:, :, lo + ROPE_HALF:lo + RET_DK]
            o_ref[:, :, lo:lo + ROPE_HALF] = ((x1 * cos - x2 * sin) * scale).astype(BF16)
            o_ref[:, :, lo + ROPE_HALF:lo + RET_DK] = ((x1 * sin + x2 * cos) * scale).astype(BF16)
    v = jnp.dot(hb, w_ref[:, 2 * d:4 * d], preferred_element_type=F32)
    v_ref[...] = v.reshape(bb, tt, 2 * d).astype(BF16)
    g = jnp.dot(hb, w_ref[:, 4 * d:6 * d], preferred_element_type=F32)
    g_ref[...] = g.reshape(bb, tt, 2 * d).astype(BF16)


def _ret_proj(x, mod, norm_g4, w_in, cos, sin, layer, widx, bb, tt, row0):
    b, t, d = x.shape
    tab = pl.BlockSpec((tt, ROPE_HALF), lambda b_, t_: (t_, 0))
    return pl.pallas_call(
        _ret_proj_kernel,
        grid=(b // bb, t // tt),
        in_specs=[
            _x_spec(bb, tt),
            _mod_spec(layer, 3, bb, row0),
            _mod_spec(layer, 4, bb, row0),
            _ng_spec(layer, 1),
            _resident((None, D_MODEL, 6 * D_MODEL), lambda b_, t_: (widx, 0, 0)),
            tab, tab,
        ],
        out_specs=[_x_spec(bb, tt), _x_spec(bb, tt),
                   _x_spec(bb, tt, 2 * d), _x_spec(bb, tt, 2 * d)],
        out_shape=[jax.ShapeDtypeStruct((b, t, d), BF16), jax.ShapeDtypeStruct((b, t, d), BF16),
                   jax.ShapeDtypeStruct((b, t, 2 * d), BF16), jax.ShapeDtypeStruct((b, t, 2 * d), BF16)],
        compiler_params=_cparams(2),
        name=f"ret_proj_l{layer}",
    )(x, mod, mod, norm_g4, w_in, cos, sin)


def _ret_core_kernel(cdec, has_state, *refs):
    if has_state:
        q_ref, k_ref, v_ref, g_ref, s0_ref, dec_ref, qd_ref, kd_ref, gn_ref, y_ref, s_ref = refs
    else:
        q_ref, k_ref, v_ref, g_ref, dec_ref, qd_ref, kd_ref, gn_ref, y_ref, s_ref = refs
        s0_ref = None

    @pl.when(pl.program_id(1) == 0)
    def _():
        if has_state:
            s_ref[...] = s0_ref[...]
        else:
            s_ref[...] = jnp.zeros(s_ref.shape, F32)

    for hd in range(RET_HEADS):
        ksl = slice(hd * RET_DK, (hd + 1) * RET_DK)
        vsl = slice(hd * RET_DV, (hd + 1) * RET_DV)
        qh = q_ref[:, ksl]
        kh = k_ref[:, ksl]
        vh = v_ref[:, vsl]
        s = s_ref[hd]
        inner = lax.dot_general(qh, kh, (((1,), (1,)), ((), ())),
                                preferred_element_type=F32) * dec_ref[hd]
        qd = (qh.astype(F32) * qd_ref[hd]).astype(BF16)
        o = (jnp.dot(inner.astype(BF16), vh, preferred_element_type=F32)
             + jnp.dot(qd, s.astype(BF16), preferred_element_type=F32))
        kd = (kh.astype(F32) * kd_ref[hd]).astype(BF16)
        s_ref[hd] = s * cdec[hd] + lax.dot_general(
            kd, vh, (((0,), (0,)), ((), ())), preferred_element_type=F32)
        mu = jnp.mean(o, axis=-1, keepdims=True)
        oc = o - mu
        var = jnp.mean(oc * oc, axis=-1, keepdims=True)
        on = oc * lax.rsqrt(var + EPS) * gn_ref[:, vsl]
        gh = g_ref[:, vsl].astype(F32)
        y_ref[:, vsl] = (_silu(gh) * on).astype(BF16)


def _ret_tables(lc):
    gamma = 1.0 - 2.0 ** (-5.0 - np.arange(RET_HEADS, dtype=np.float64))
    idx = np.arange(lc, dtype=np.float64)
    diff = idx[:, None] - idx[None, :]
    decay = np.where(diff >= 0, gamma[:, None, None] ** np.maximum(diff, 0.0), 0.0)
    q_dec = gamma[:, None] ** (idx + 1.0)
    k_dec = gamma[:, None] ** (lc - 1.0 - idx)
    c_dec = gamma ** float(lc)
    qd = np.broadcast_to(q_dec[:, :, None], (RET_HEADS, lc, RET_DK))
    kd = np.broadcast_to(k_dec[:, :, None], (RET_HEADS, lc, RET_DK))
    return (jnp.asarray(decay, F32), jnp.asarray(qd, F32), jnp.asarray(kd, F32),
            tuple(float(c) for c in c_dec))


def _ret_core(q, k, v, g, s0, gn_g, lc, name):
    b, t, d = q.shape
    decay, qd, kd, cdec = _ret_tables(lc)
    has_state = s0 is not None
    row = lambda width: pl.BlockSpec((None, lc, width), lambda b_, t_: (b_, t_, 0))
    const3 = lambda a: _resident(a.shape, lambda b_, t_: (0, 0, 0))
    s_spec = pl.BlockSpec((None, RET_HEADS, RET_DK, RET_DV), lambda b_, t_: (b_, 0, 0, 0))
    in_specs = [row(d), row(d), row(2 * d), row(2 * d)]
    args = [q, k, v, g]
    if has_state:
        in_specs.append(s_spec)
        args.append(s0)
    in_specs += [const3(decay), const3(qd), const3(kd),
                 pl.BlockSpec((1, 2 * d), lambda b_, t_: (0, 0))]
    args += [decay, qd, kd, gn_g]
    return pl.pallas_call(
        functools.partial(_ret_core_kernel, cdec, has_state),
        grid=(b, t // lc),
        in_specs=in_specs,
        out_specs=[row(2 * d), s_spec],
        out_shape=[jax.ShapeDtypeStruct((b, t, 2 * d), BF16),
                   jax.ShapeDtypeStruct((b, RET_HEADS, RET_DK, RET_DV), F32)],
        compiler_params=_cparams(2),
        name=name,
    )(*args)


def _pool_kernel(pos0, x_ref, sh_ref, sc_ref, gm_ref, ng_ref, buf_ref, w_ref, pb_ref, ps_ref,
                 o_ref, nb_ref, carry_ref):
    tt, d = x_ref.shape
    t_idx = pl.program_id(1)

    @pl.when(t_idx == 0)
    def _():
        carry_ref[...] = buf_ref[...]

    x = x_ref[...]
    h = _norm_mod(x[None], ng_ref[...], sc_ref[...], sh_ref[...])[0]
    ext = jnp.concatenate([carry_ref[...], h], axis=0)
    carry_ref[...] = h[tt - POOL_CARRY:, :]
    nb_ref[...] = h[tt - POOL_BUF:, :]

    pos = pos0 + t_idx * tt + lax.broadcasted_iota(jnp.int32, (tt, 1), 0)
    run = ext
    w = 1
    outs = []
    for gi, win in enumerate(POOL_WINDOWS):
        while w < win:
            n = run.shape[0]
            run = run[w:, :] + run[:n - w, :]
            w *= 2
        lo = gi * POOL_GC
        rows = run.shape[0]
        wsum = run[rows - tt:, lo - (d - run.shape[1]):lo - (d - run.shape[1]) + POOL_GC]
        inv_cnt = 1.0 / jnp.minimum(pos + 1, win).astype(F32)
        pooled = wsum * inv_cnt - h[:, lo:lo + POOL_GC]
        outs.append(jnp.dot(pooled.astype(BF16), w_ref[gi], preferred_element_type=F32))
        run = run[:, POOL_GC:]
    y = (jnp.concatenate(outs, axis=1) + pb_ref[...]) * ps_ref[...]
    o_ref[...] = x + gm_ref[0] * y


def _pool(x, mod, norm_g4, buf16, w, pb, ps, layer, widx, tt, row0, pos0):
    b, t, d = x.shape
    xs = pl.BlockSpec((None, tt, d), lambda b_, t_: (b_, t_, 0))
    vec = pl.BlockSpec((None, 1, d), lambda b_, t_: (widx, 0, 0))
    return pl.pallas_call(
        functools.partial(_pool_kernel, pos0),
        grid=(b, t // tt),
        in_specs=[
            xs,
            _mod_spec(layer, 3, 1, row0),
            _mod_spec(layer, 4, 1, row0),
            _mod_spec(layer, 5, 1, row0),
            _ng_spec(layer, 1),
            pl.BlockSpec((None, POOL_CARRY, d), lambda b_, t_: (b_, 0, 0)),
            pl.BlockSpec((None, POOL_GROUPS, POOL_GC, POOL_GC), lambda b_, t_: (widx, 0, 0, 0)),
            vec, vec,
        ],
        out_specs=[xs, pl.BlockSpec((None, POOL_BUF, d), lambda b_, t_: (b_, 0, 0))],
        out_shape=[jax.ShapeDtypeStruct(x.shape, F32),
                   jax.ShapeDtypeStruct((b, POOL_BUF, d), F32)],
        scratch_shapes=[pltpu.VMEM((POOL_CARRY, d), F32)],
        compiler_params=_cparams(2),
        name=f"pool_l{layer}",
    )(x, mod, mod, mod, norm_g4, buf16, w, pb, ps)


def _att_proj_kernel(x_ref, sh_ref, sc_ref, ng_ref, w_ref, p_ref, qg_ref, kg_ref,
                     q_ref, k_ref, v_ref, kb_ref, vb_ref):
    bb, tt, d = x_ref.shape
    h = _norm_mod(x_ref[...], ng_ref[...], sc_ref[...], sh_ref[...])
    hb = h.reshape(bb * tt, d).astype(BF16)

    def head_norm(y, gain):
        ms = jnp.dot((y * y).astype(BF16), p_ref[...], preferred_element_type=F32)
        return y * lax.rsqrt(ms + EPS) * gain

    q = jnp.dot(hb, w_ref[:, 0:d], preferred_element_type=F32)
    q = head_norm(q, qg_ref[...]) * (ATT_HD ** -0.5)
    q_ref[...] = q.reshape(bb, tt, d).astype(BF16)
    k = jnp.dot(hb, w_ref[:, d:2 * d], preferred_element_type=F32)
    k = head_norm(k, kg_ref[...]).reshape(bb, tt, d)
    k_ref[...] = k
    kb_ref[...] = k.astype(BF16)
    v = jnp.dot(hb, w_ref[:, 2 * d:3 * d], preferred_element_type=F32).reshape(bb, tt, d)
    v_ref[...] = v
    vb_ref[...] = v.astype(BF16)


def _att_proj(x, mod, norm_g4, w_qkv, pavg, qg, kg, layer, widx, bb, tt, row0):
    b, t, d = x.shape
    vec = pl.BlockSpec((None, 1, d), lambda b_, t_: (widx, 0, 0))
    f32o = jax.ShapeDtypeStruct((b, t, d), F32)
    b16o = jax.ShapeDtypeStruct((b, t, d), BF16)
    return pl.pallas_call(
        _att_proj_kernel,
        grid=(b // bb, t // tt),
        in_specs=[
            _x_spec(bb, tt),
            _mod_spec(layer, 3, bb, row0),
            _mod_spec(layer, 4, bb, row0),
            _ng_spec(layer, 1),
            _resident((None, d, 3 * d), lambda b_, t_: (widx, 0, 0)),
            _resident((d, d), lambda b_, t_: (0, 0)),
            vec, vec,
        ],
        out_specs=[_x_spec(bb, tt)] * 5,
        out_shape=[b16o, f32o, f32o, b16o, b16o],
        compiler_params=_cparams(2),
        name=f"att_proj_l{layer}",
    )(x, mod, mod, norm_g4, w_qkv, pavg, qg, kg)


def _att_core_kernel(first_pos, q_ref, k_ref, v_ref, bias_ref, o_ref):
    t = q_ref.shape[0]
    rb = lax.broadcasted_iota(jnp.int32, (ATT_GW, ATT_GW), 0) // ATT_HD
    cb = lax.broadcasted_iota(jnp.int32, (ATT_GW, ATT_GW), 1) // ATT_HD
    diag = rb == cb

    def chunk(n, carry):
        r0 = pl.multiple_of(n * CHUNK, CHUNK)
        qc = q_ref[pl.ds(r0, CHUNK), :]
        qbd = jnp.where(diag, jnp.concatenate([qc] * ATT_GROUP, axis=0), jnp.zeros((), BF16))
        kb = k_ref[pl.ds(r0, BAND), :]
        vb = v_ref[pl.ds(r0, BAND), :]
        s = lax.dot_general(qbd, kb, (((1,), (1,)), ((), ())), preferred_element_type=F32)
        s = s + bias_ref[...]
        if first_pos < 0:
            kpos = first_pos + r0 + lax.broadcasted_iota(jnp.int32, (1, BAND), 1)
            s = jnp.where(kpos >= 0, s, NEG_INF)
        m = jnp.max(s, axis=-1, keepdims=True)
        p = jnp.exp(s - m)
        l = jnp.sum(p, axis=-1, keepdims=True)
        o = jnp.dot(p.astype(BF16), vb, preferred_element_type=F32) * (1.0 / l)
        o = jnp.where(diag, o, 0.0)
        og = o[0:CHUNK]
        for hh in range(1, ATT_GROUP):
            og = og + o[hh * CHUNK:(hh + 1) * CHUNK]
        o_ref[pl.ds(r0, CHUNK), :] = og.astype(BF16)
        return carry

    lax.fori_loop(0, t // CHUNK, chunk, 0)


def _att_core(q, kfull, vfull, bias, first_pos, name):
    b, t, d = q.shape
    grp = lambda rows: pl.BlockSpec((None, rows, ATT_GW), lambda b_, g_: (b_, 0, g_))
    return pl.pallas_call(
        functools.partial(_att_core_kernel, first_pos),
        grid=(b, ATT_NGROUPS),
        in_specs=[grp(t), grp(t + BAND_ROWS), grp(t + BAND_ROWS),
                  pl.BlockSpec((None, ATT_GW, BAND), lambda b_, g_: (g_, 0, 0))],
        out_specs=grp(t),
        out_shape=jax.ShapeDtypeStruct((b, t, d), BF16),
        compiler_params=_cparams(2),
        name=name,
    )(q, kfull, vfull, bias)


def _att_bias_table(rel_bias):
    i = np.arange(CHUNK)[:, None]
    j = np.arange(BAND)[None, :]
    rel = np.clip(i - j + BAND_ROWS, REL_MIN, REL_MAX) - REL_MIN
    tab = rel_bias[:, rel]
    return tab.reshape(ATT_NGROUPS, ATT_GW, BAND)


def _rope_tables(pos0, t):
    inv = ROPE_BASE ** (-np.arange(ROPE_HALF, dtype=np.float64) / ROPE_HALF)
    ang = (pos0 + np.arange(t, dtype=np.float64))[:, None] * inv[None, :]
    return jnp.asarray(np.cos(ang), F32), jnp.asarray(np.sin(ang), F32)


def _trunk(x, mod, row0, pos0, bb, tt, ret_lc, ret_states, pool_bufs, att_k_cache, att_v_cache, p):
    b, t, d = x.shape
    new_ret, new_pool, new_k, new_v = [], [], [], []
    cos, sin = _rope_tables(pos0, t)
    for i in range(DEPTH):
        x = _ffn(x, mod, p['norm_g'], p['ffn1_w_in'], p['ffn1_w_out'], i, 1, bb, tt, row0)
        j = i // N_MIXERS
        kind = i % N_MIXERS
        if kind == 0:
            q, k, v, g = _ret_proj(x, mod, p['norm_g'], p['ret_w_in'], cos, sin, i, j, bb, tt, row0)
            s0 = None if ret_states is None else ret_states[j]
            y, s_new = _ret_core(q, k, v, g, s0, p['ret_gn_g'][j].reshape(1, 2 * d), ret_lc,
                                 f"ret_core_l{i}")
            new_ret.append(s_new)
            x = _matres(y, x, mod, p['ret_w_out'], i, j, bb, tt, row0, f"ret_out_l{i}")
        elif kind == 1:
            if pool_bufs is None:
                buf16 = jnp.zeros((b, POOL_CARRY, d), F32)
            else:
                buf16 = jnp.pad(pool_bufs[j], ((0, 0), (POOL_CARRY - POOL_BUF, 0), (0, 0)))
            x, nb = _pool(x, mod, p['norm_g'], buf16, p['pool_w'], p['pool_b'], p['pool_scale'],
                          i, j, min(tt, t), row0, pos0)
            new_pool.append(nb)
        else:
            q, k, v, kb, vb = _att_proj(x, mod, p['norm_g'], p['att_w_qkv'], p['att_pavg'],
                                        p['att_q_g'], p['att_k_g'], i, j, bb, tt, row0)
            if att_k_cache is None:
                pad = ((0, 0), (BAND_ROWS, 0), (0, 0))
                kfull = jnp.pad(kb, pad)
                vfull = jnp.pad(vb, pad)
                keep = min(BAND_ROWS, t)
                new_k.append(k[:, t - keep:].reshape(b, keep, ATT_HEADS, ATT_HD))
                new_v.append(v[:, t - keep:].reshape(b, keep, ATT_HEADS, ATT_HD))
            else:
                ck = att_k_cache[j].reshape(b, BAND_ROWS, d).astype(BF16)
                cv = att_v_cache[j].reshape(b, BAND_ROWS, d).astype(BF16)
                kfull = jnp.concatenate([ck, kb], axis=1)
                vfull = jnp.concatenate([cv, vb], axis=1)
                new_k.append(k.reshape(b, t, ATT_HEADS, ATT_HD))
                new_v.append(v.reshape(b, t, ATT_HEADS, ATT_HD))
            o = _att_core(q, kfull, vfull, p['att_bias'][j], pos0 - BAND_ROWS, f"att_core_l{i}")
            x = _matres(o, x, mod, p['att_w_o'], i, j, bb, tt, row0, f"att_out_l{i}")
        x = _ffn(x, mod, p['norm_g'], p['ffn2_w_in'], p['ffn2_w_out'], i, 2, bb, tt, row0)
    return x, jnp.stack(new_ret), jnp.stack(new_pool), jnp.stack(new_k), jnp.stack(new_v)


def kernel(x_prompt, x_sample, c_prompt, c_sample, state_ret, state_pool, cache_att_k, cache_att_v, norm_g, ada_w, ada_b, ffn1_w_in, ffn1_w_out, ffn2_w_in, ffn2_w_out, ret_w_in, ret_w_out, ret_gn_g, pool_w, pool_b, pool_scale, att_w_qkv, att_w_o, att_q_g, att_k_g, att_rel_bias):
    assert cache_att_k.shape[2] == BAND_ROWS, "key cache must hold exactly the left band"
    bp, tp, d = x_prompt.shape
    bs, ts, _ = x_sample.shape
    n_att = att_w_qkv.shape[0]
    head_id = np.arange(d) // ATT_HD
    pavg = jnp.asarray((head_id[:, None] == head_id[None, :]) / ATT_HD, BF16)
    p = {
        'norm_g': norm_g.reshape(DEPTH, 3, 1, d),
        'ffn1_w_in': ffn1_w_in.astype(BF16), 'ffn1_w_out': ffn1_w_out.astype(BF16),
        'ffn2_w_in': ffn2_w_in.astype(BF16), 'ffn2_w_out': ffn2_w_out.astype(BF16),
        'ret_w_in': ret_w_in.astype(BF16), 'ret_w_out': ret_w_out.astype(BF16),
        'ret_gn_g': ret_gn_g,
        'pool_w': pool_w.astype(BF16),
        'pool_b': pool_b.reshape(-1, 1, d), 'pool_scale': pool_scale.reshape(-1, 1, d),
        'att_w_qkv': att_w_qkv.astype(BF16), 'att_w_o': att_w_o.astype(BF16),
        'att_q_g': jnp.tile(att_q_g, (1, ATT_HEADS)).reshape(n_att, 1, d),
        'att_k_g': jnp.tile(att_k_g, (1, ATT_HEADS)).reshape(n_att, 1, d),
        'att_pavg': pavg,
        'att_bias': [_att_bias_table(att_rel_bias[j]) for j in range(n_att)],
    }
    mod = _ada(jnp.concatenate([c_prompt, c_sample], axis=0), ada_w, ada_b)

    tt_p = min(PROMPT_ROWS, tp)
    y_prompt, ret_p, pool_p, k_p, v_p = _trunk(
        x_prompt, mod, 0, 0, 1, tt_p, min(RET_CHUNK_PROMPT, tp), None, None, None, None, p)
    bb_s = max(1, min(bs, PROMPT_ROWS // ts))
    y_sample, ret_s, pool_s, k_s, v_s = _trunk(
        x_sample, mod, bp, PAST_LEN, bb_s, ts, ts, state_ret, state_pool,
        cache_att_k, cache_att_v, p)
    return (y_prompt, y_sample, ret_p, ret_s, pool_p, pool_s, k_p, v_p, k_s, v_s)
```

```python
import functools

import numpy as np
import jax
import jax.numpy as jnp
from jax import lax
from jax.experimental import pallas as pl
from jax.experimental.pallas import tpu as pltpu

F32 = jnp.float32
BF16 = jnp.bfloat16

D_MODEL = 1024
DEPTH = 4
D_FF = 2816
EPS = 1e-6
CHUNK = 64
PAST_LEN = 2048
N_MIXERS = 3
RET_HEADS = 4
RET_DK = D_MODEL // RET_HEADS
RET_DV = 2 * D_MODEL // RET_HEADS
ROPE_BASE = 10000.0
ROPE_HALF = RET_DK // 2
POOL_WINDOWS = (2, 4, 8, 16)
POOL_GROUPS = 4
POOL_GC = D_MODEL // POOL_GROUPS
POOL_BUF = 15
POOL_CARRY = 16
ATT_HEADS = 16
ATT_HD = D_MODEL // ATT_HEADS
ATT_GROUP = 4
ATT_GW = ATT_GROUP * ATT_HD
ATT_NGROUPS = ATT_HEADS // ATT_GROUP
LEFT_CHUNKS = 8
BAND_ROWS = LEFT_CHUNKS * CHUNK
BAND = BAND_ROWS + CHUNK
REL_MIN = -(CHUNK - 1)
REL_MAX = 256
REL_SIZE = REL_MAX - REL_MIN + 1
REL_BASE_W = 640
NEG_INF = -1e30
N_MOD = 9

VMEM_LIMIT_BYTES = 56 * 1024 * 1024
PROMPT_ROWS = 512
RET_CHUNK_PROMPT = 256
ATT_UNROLL = 8


def _cparams(n_grid):
    return pltpu.CompilerParams(
        dimension_semantics=("arbitrary",) * n_grid,
        vmem_limit_bytes=VMEM_LIMIT_BYTES,
    )


def _resident(block_shape, index_map):
    return pl.BlockSpec(block_shape, index_map, pipeline_mode=pl.Buffered(1))


def _norm_mod(x, ng, sc, sh):
    ms = jnp.mean(x * x, axis=-1, keepdims=True)
    y = x * lax.rsqrt(ms + EPS) * ng
    return y * (1.0 + sc) + sh


def _silu(x):
    return x * (1.0 / (1.0 + jnp.exp(-x)))


def _ada_kernel(c_ref, w_ref, b_ref, o_ref):
    c = _silu(c_ref[...]).astype(BF16)
    w = w_ref[...].astype(BF16)
    o_ref[...] = jnp.dot(c, w, preferred_element_type=F32) + b_ref[...]


def _ada(c_all, ada_w, ada_b):
    n = c_all.shape[0]
    b4 = ada_b.reshape(DEPTH, N_MOD, 1, D_MODEL)
    out = pl.pallas_call(
        _ada_kernel,
        grid=(DEPTH, N_MOD),
        in_specs=[
            pl.BlockSpec((n, D_MODEL), lambda i, j: (0, 0)),
            pl.BlockSpec((None, D_MODEL, D_MODEL), lambda i, j: (i, 0, j)),
            pl.BlockSpec((None, None, 1, D_MODEL), lambda i, j: (i, j, 0, 0)),
        ],
        out_specs=pl.BlockSpec((None, None, n, D_MODEL), lambda i, j: (i, j, 0, 0)),
        out_shape=jax.ShapeDtypeStruct((DEPTH, N_MOD, n, D_MODEL), F32),
        compiler_params=_cparams(2),
        name="ada_mod",
    )(c_all, ada_w, b4)
    return out.reshape(DEPTH, N_MOD, n, 1, D_MODEL)


def _mod_spec(layer, j, bb, row0):
    blk0 = row0 // bb
    return pl.BlockSpec((None, None, bb, 1, D_MODEL),
                        lambda b, t: (layer, j, blk0 + b, 0, 0))


def _ng_spec(layer, j):
    return pl.BlockSpec((None, None, 1, D_MODEL), lambda b, t: (layer, j, 0, 0))


def _x_spec(bb, tt, width=D_MODEL):
    return pl.BlockSpec((bb, tt, width), lambda b, t: (b, t, 0))


def _ffn_kernel(x_ref, sh_ref, sc_ref, g_ref, ng_ref, win_ref, wout_ref, o_ref):
    bb, tt, d = x_ref.shape
    x = x_ref[...]
    h = _norm_mod(x, ng_ref[...], sc_ref[...], sh_ref[...])
    hb = h.reshape(bb * tt, d).astype(BF16)
    gu = jnp.dot(hb, win_ref[...], preferred_element_type=F32)
    act = (_silu(gu[:, :D_FF]) * gu[:, D_FF:]).astype(BF16)
    y = jnp.dot(act, wout_ref[...], preferred_element_type=F32)
    o_ref[...] = x + (0.5 * g_ref[...]) * y.reshape(bb, tt, d)


def _ffn(x, mod, norm_g4, w_in, w_out, layer, which, bb, tt, row0):
    b, t, d = x.shape
    j0 = 0 if which == 1 else 6
    nj = 0 if which == 1 else 2
    return pl.pallas_call(
        _ffn_kernel,
        grid=(b // bb, t // tt),
        in_specs=[
            _x_spec(bb, tt),
            _mod_spec(layer, j0, bb, row0),
            _mod_spec(layer, j0 + 1, bb, row0),
            _mod_spec(layer, j0 + 2, bb, row0),
            _ng_spec(layer, nj),
            _resident((None, D_MODEL, 2 * D_FF), lambda b_, t_: (layer, 0, 0)),
            _resident((None, D_FF, D_MODEL), lambda b_, t_: (layer, 0, 0)),
        ],
        out_specs=_x_spec(bb, tt),
        out_shape=jax.ShapeDtypeStruct(x.shape, F32),
        compiler_params=_cparams(2),
        name=f"ffn{which}_l{layer}",
    )(x, mod, mod, mod, norm_g4, w_in, w_out)


def _matres_kernel(a_ref, x_ref, g_ref, w_ref, o_ref):
    bb, tt, k = a_ref.shape
    y = jnp.dot(a_ref[...].reshape(bb * tt, k), w_ref[...], preferred_element_type=F32)
    o_ref[...] = x_ref[...] + g_ref[...] * y.reshape(bb, tt, D_MODEL)


def _matres(a, x, mod, w, layer, widx, bb, tt, row0, name):
    b, t, k = a.shape
    return pl.pallas_call(
        _matres_kernel,
        grid=(b // bb, t // tt),
        in_specs=[
            _x_spec(bb, tt, k),
            _x_spec(bb, tt),
            _mod_spec(layer, 5, bb, row0),
            _resident((None, k, D_MODEL), lambda b_, t_: (widx, 0, 0)),
        ],
        out_specs=_x_spec(bb, tt),
        out_shape=jax.ShapeDtypeStruct(x.shape, F32),
        compiler_params=_cparams(2),
        name=name,
    )(a, x, mod, w)


def _ret_proj_kernel(x_ref, sh_ref, sc_ref, ng_ref, w_ref, cos_ref, sin_ref, qs_ref, ks_ref,
                     q_ref, k_ref, v_ref, g_ref):
    bb, tt, d = x_ref.shape
    h = _norm_mod(x_ref[...], ng_ref[...], sc_ref[...], sh_ref[...])
    hb = h.reshape(bb * tt, d).astype(BF16)
    cos = cos_ref[...][None]
    sin = sin_ref[...][None]
    for idx, (o_ref, s_ref) in enumerate(((q_ref, qs_ref), (k_ref, ks_ref))):
        y = jnp.dot(hb, w_ref[:, idx * d:(idx + 1) * d], preferred_element_type=F32)
        y = y.reshape(bb, tt, d)
        for hd in range(RET_HEADS):
            lo = hd * RET_DK
            mid = lo + ROPE_HALF
            x1 = y[:, :, lo:mid]
            x2 = y[:, :, mid:lo + RET_DK]
            o_ref[:, :, lo:mid] = ((x1 * cos - x2 * sin) * s_ref[:, lo:mid][None]).astype(BF16)
            o_ref[:, :, mid:lo + RET_DK] = (
                (x1 * sin + x2 * cos) * s_ref[:, mid:lo + RET_DK][None]).astype(BF16)
    v = jnp.dot(hb, w_ref[:, 2 * d:4 * d], preferred_element_type=F32)
    v_ref[...] = v.reshape(bb, tt, 2 * d).astype(BF16)
    g = jnp.dot(hb, w_ref[:, 4 * d:6 * d], preferred_element_type=F32)
    g_ref[...] = _silu(g).reshape(bb, tt, 2 * d).astype(BF16)


def _ret_gamma():
    return 1.0 - 2.0 ** (-5.0 - np.arange(RET_HEADS, dtype=np.float64))


def _ret_scale_tables(tt, lc):
    gamma = _ret_gamma()
    i = (np.arange(tt) % lc).astype(np.float64)[:, None]
    g_lane = np.repeat(gamma, RET_DK)[None, :]
    qs = g_lane ** (i + 1.0)
    ks = (RET_DK ** -0.5) * g_lane ** (-(i + 1.0))
    return jnp.asarray(qs, F32), jnp.asarray(ks, F32)


def _ret_proj(x, mod, norm_g4, w_in, cos, sin, lc, layer, widx, bb, tt, row0):
    b, t, d = x.shape
    assert tt % lc == 0
    qs, ks = _ret_scale_tables(tt, lc)
    tab = pl.BlockSpec((tt, ROPE_HALF), lambda b_, t_: (t_, 0))
    scale = _resident((tt, d), lambda b_, t_: (0, 0))
    return pl.pallas_call(
        _ret_proj_kernel,
        grid=(b // bb, t // tt),
        in_specs=[
            _x_spec(bb, tt),
            _mod_spec(layer, 3, bb, row0),
            _mod_spec(layer, 4, bb, row0),
            _ng_spec(layer, 1),
            _resident((None, D_MODEL, 6 * D_MODEL), lambda b_, t_: (widx, 0, 0)),
            tab, tab, scale, scale,
        ],
        out_specs=[_x_spec(bb, tt), _x_spec(bb, tt),
                   _x_spec(bb, tt, 2 * d), _x_spec(bb, tt, 2 * d)],
        out_shape=[jax.ShapeDtypeStruct((b, t, d), BF16), jax.ShapeDtypeStruct((b, t, d), BF16),
                   jax.ShapeDtypeStruct((b, t, 2 * d), BF16), jax.ShapeDtypeStruct((b, t, 2 * d), BF16)],
        compiler_params=_cparams(2),
        name=f"ret_proj_l{layer}",
    )(x, mod, mod, norm_g4, w_in, cos, sin, qs, ks)


def _ret_core_kernel(cdec, has_state, *refs):
    if has_state:
        q_ref, k_ref, v_ref, g_ref, s0_ref, tril_ref, gn_ref, y_ref, s_ref = refs
    else:
        q_ref, k_ref, v_ref, g_ref, tril_ref, gn_ref, y_ref, s_ref = refs
        s0_ref = None

    @pl.when(pl.program_id(1) == 0)
    def _():
        if has_state:
            s_ref[...] = s0_ref[...]
        else:
            s_ref[...] = jnp.zeros(s_ref.shape, F32)

    for hd in range(RET_HEADS):
        ksl = slice(hd * RET_DK, (hd + 1) * RET_DK)
        vsl = slice(hd * RET_DV, (hd + 1) * RET_DV)
        qh = q_ref[:, ksl]
        kh = k_ref[:, ksl]
        vh = v_ref[:, vsl]
        s = s_ref[hd]
        inner = lax.dot_general(qh, kh, (((1,), (1,)), ((), ())),
                                preferred_element_type=F32) * tril_ref[...]
        o = (jnp.dot(inner.astype(BF16), vh, preferred_element_type=F32)
             + jnp.dot(qh, s.astype(BF16), preferred_element_type=F32))
        s_ref[hd] = (s + lax.dot_general(kh, vh, (((0,), (0,)), ((), ())),
                                         preferred_element_type=F32)) * cdec[hd]
        mu = jnp.mean(o, axis=-1, keepdims=True)
        oc = o - mu
        var = jnp.mean(oc * oc, axis=-1, keepdims=True)
        on = oc * lax.rsqrt(var + EPS) * gn_ref[:, vsl]
        y_ref[:, vsl] = (g_ref[:, vsl].astype(F32) * on).astype(BF16)


def _ret_core(q, k, v, g, s0, gn_g, lc, name):
    b, t, d = q.shape
    cdec = tuple(float(c) for c in _ret_gamma() ** float(lc))
    tril = jnp.asarray(np.tril(np.ones((lc, lc))), F32)
    has_state = s0 is not None
    row = lambda width: pl.BlockSpec((None, lc, width), lambda b_, t_: (b_, t_, 0))
    s_spec = pl.BlockSpec((None, RET_HEADS, RET_DK, RET_DV), lambda b_, t_: (b_, 0, 0, 0))
    in_specs = [row(d), row(d), row(2 * d), row(2 * d)]
    args = [q, k, v, g]
    if has_state:
        in_specs.append(s_spec)
        args.append(s0)
    in_specs += [_resident((lc, lc), lambda b_, t_: (0, 0)),
                 pl.BlockSpec((1, 2 * d), lambda b_, t_: (0, 0))]
    args += [tril, gn_g]
    return pl.pallas_call(
        functools.partial(_ret_core_kernel, cdec, has_state),
        grid=(b, t // lc),
        in_specs=in_specs,
        out_specs=[row(2 * d), s_spec],
        out_shape=[jax.ShapeDtypeStruct((b, t, 2 * d), BF16),
                   jax.ShapeDtypeStruct((b, RET_HEADS, RET_DK, RET_DV), F32)],
        compiler_params=_cparams(2),
        name=name,
    )(*args)


def _pool_kernel(pos0, x_ref, sh_ref, sc_ref, gm_ref, ng_ref, buf_ref, w_ref, pb_ref, ps_ref,
                 o_ref, nb_ref, carry_ref):
    tt, d = x_ref.shape
    t_idx = pl.program_id(1)

    @pl.when(t_idx == 0)
    def _():
        carry_ref[...] = buf_ref[...]

    x = x_ref[...]
    h = _norm_mod(x[None], ng_ref[...], sc_ref[...], sh_ref[...])[0]
    ext = jnp.concatenate([carry_ref[...], h], axis=0)
    carry_ref[...] = h[tt - POOL_CARRY:, :]
    nb_ref[...] = h[tt - POOL_BUF:, :]

    pos = pos0 + t_idx * tt + lax.broadcasted_iota(jnp.int32, (tt, 1), 0)
    run = ext
    w = 1
    outs = []
    for gi, win in enumerate(POOL_WINDOWS):
        while w < win:
            n = run.shape[0]
            run = run[w:, :] + run[:n - w, :]
            w *= 2
        wsum = run[run.shape[0] - tt:, :POOL_GC]
        inv_cnt = 1.0 / jnp.minimum(pos + 1, win).astype(F32)
        lo = gi * POOL_GC
        pooled = wsum * inv_cnt - h[:, lo:lo + POOL_GC]
        outs.append(jnp.dot(pooled.astype(BF16), w_ref[gi], preferred_element_type=F32))
        run = run[:, POOL_GC:]
    y = (jnp.concatenate(outs, axis=1) + pb_ref[...]) * ps_ref[...]
    o_ref[...] = x + gm_ref[0] * y


def _pool(x, mod, norm_g4, buf16, w, pb, ps, layer, widx, tt, row0, pos0):
    b, t, d = x.shape
    xs = pl.BlockSpec((None, tt, d), lambda b_, t_: (b_, t_, 0))
    vec = pl.BlockSpec((None, 1, d), lambda b_, t_: (widx, 0, 0))
    return pl.pallas_call(
        functools.partial(_pool_kernel, pos0),
        grid=(b, t // tt),
        in_specs=[
            xs,
            _mod_spec(layer, 3, 1, row0),
            _mod_spec(layer, 4, 1, row0),
            _mod_spec(layer, 5, 1, row0),
            _ng_spec(layer, 1),
            pl.BlockSpec((None, POOL_CARRY, d), lambda b_, t_: (b_, 0, 0)),
            pl.BlockSpec((None, POOL_GROUPS, POOL_GC, POOL_GC), lambda b_, t_: (widx, 0, 0, 0)),
            vec, vec,
        ],
        out_specs=[xs, pl.BlockSpec((None, POOL_BUF, d), lambda b_, t_: (b_, 0, 0))],
        out_shape=[jax.ShapeDtypeStruct(x.shape, F32),
                   jax.ShapeDtypeStruct((b, POOL_BUF, d), F32)],
        scratch_shapes=[pltpu.VMEM((POOL_CARRY, d), F32)],
        compiler_params=_cparams(2),
        name=f"pool_l{layer}",
    )(x, mod, mod, mod, norm_g4, buf16, w, pb, ps)


def _att_proj_kernel(first_keep_tile, x_ref, sh_ref, sc_ref, ng_ref, w_ref, p_ref, qg_ref, kg_ref,
                     q_ref, kb_ref, vb_ref, kk_ref, vk_ref):
    bb, tt, d = x_ref.shape
    h = _norm_mod(x_ref[...], ng_ref[...], sc_ref[...], sh_ref[...])
    hb = h.reshape(bb * tt, d).astype(BF16)

    def head_norm(y, gain):
        ms = jnp.dot((y * y).astype(BF16), p_ref[...], preferred_element_type=F32)
        return y * lax.rsqrt(ms + EPS) * gain

    q = jnp.dot(hb, w_ref[:, 0:d], preferred_element_type=F32)
    q = head_norm(q, qg_ref[...]) * (ATT_HD ** -0.5)
    q_ref[...] = q.reshape(bb, tt, d).astype(BF16)
    k = jnp.dot(hb, w_ref[:, d:2 * d], preferred_element_type=F32)
    k = head_norm(k, kg_ref[...]).reshape(bb, tt, d)
    kb_ref[...] = k.astype(BF16)
    v = jnp.dot(hb, w_ref[:, 2 * d:3 * d], preferred_element_type=F32).reshape(bb, tt, d)
    vb_ref[...] = v.astype(BF16)

    @pl.when(pl.program_id(1) >= first_keep_tile)
    def _():
        kk_ref[...] = k
        vk_ref[...] = v


def _att_proj(x, mod, norm_g4, w_qkv, pavg, qg, kg, keep, layer, widx, bb, tt, row0):
    b, t, d = x.shape
    assert keep % tt == 0 and t % tt == 0
    first_keep_tile = (t - keep) // tt
    vec = pl.BlockSpec((None, 1, d), lambda b_, t_: (widx, 0, 0))
    keep_spec = pl.BlockSpec((bb, tt, d),
                             lambda b_, t_: (b_, jnp.maximum(t_ - first_keep_tile, 0), 0))
    b16o = jax.ShapeDtypeStruct((b, t, d), BF16)
    f32k = jax.ShapeDtypeStruct((b, keep, d), F32)
    return pl.pallas_call(
        functools.partial(_att_proj_kernel, first_keep_tile),
        grid=(b // bb, t // tt),
        in_specs=[
            _x_spec(bb, tt),
            _mod_spec(layer, 3, bb, row0),
            _mod_spec(layer, 4, bb, row0),
            _ng_spec(layer, 1),
            _resident((None, d, 3 * d), lambda b_, t_: (widx, 0, 0)),
            _resident((d, d), lambda b_, t_: (0, 0)),
            vec, vec,
        ],
        out_specs=[_x_spec(bb, tt)] * 3 + [keep_spec] * 2,
        out_shape=[b16o, b16o, b16o, f32k, f32k],
        compiler_params=_cparams(2),
        name=f"att_proj_l{layer}",
    )(x, mod, mod, norm_g4, w_qkv, pavg, qg, kg)


def _att_core_kernel(first_pos, has_past, *refs):
    if has_past:
        q_ref, k_ref, v_ref, pk_ref, pv_ref, bias_ref, o_ref, kbuf, vbuf = refs
    else:
        q_ref, k_ref, v_ref, bias_ref, o_ref, kbuf, vbuf = refs
    t = q_ref.shape[0]
    if has_past:
        kbuf[0:BAND_ROWS, :] = pk_ref[...].astype(BF16)
        vbuf[0:BAND_ROWS, :] = pv_ref[...].astype(BF16)
    else:
        kbuf[0:BAND_ROWS, :] = jnp.zeros((BAND_ROWS, ATT_GW), BF16)
        vbuf[0:BAND_ROWS, :] = jnp.zeros((BAND_ROWS, ATT_GW), BF16)
    kbuf[BAND_ROWS:, :] = k_ref[...]
    vbuf[BAND_ROWS:, :] = v_ref[...]

    rb = lax.broadcasted_iota(jnp.int32, (ATT_GW, ATT_GW), 0) // ATT_HD
    cb = lax.broadcasted_iota(jnp.int32, (ATT_GW, ATT_GW), 1) // ATT_HD
    diag = rb == cb

    def make_chunk(masked):
        def chunk(n, carry):
            r0 = pl.multiple_of(n * CHUNK, CHUNK)
            qc = q_ref[pl.ds(r0, CHUNK), :]
            qbd = jnp.where(diag, jnp.concatenate([qc] * ATT_GROUP, axis=0), jnp.zeros((), BF16))
            kb = kbuf[pl.ds(r0, BAND), :]
            vb = vbuf[pl.ds(r0, BAND), :]
            s = lax.dot_general(qbd, kb, (((1,), (1,)), ((), ())), preferred_element_type=F32)
            s = s + bias_ref[...]
            if masked:
                kpos = first_pos + r0 + lax.broadcasted_iota(jnp.int32, (1, BAND), 1)
                s = jnp.where(kpos >= 0, s, NEG_INF)
            m = jnp.max(s, axis=-1, keepdims=True)
            p = jnp.exp(s - m)
            l = jnp.sum(p, axis=-1, keepdims=True)
            o = jnp.dot(p.astype(BF16), vb, preferred_element_type=F32) * (1.0 / l)
            o = jnp.where(diag, o, 0.0)
            og = o[0:CHUNK]
            for hh in range(1, ATT_GROUP):
                og = og + o[hh * CHUNK:(hh + 1) * CHUNK]
            o_ref[pl.ds(r0, CHUNK), :] = og.astype(BF16)
            return carry
        return chunk

    n_chunks = t // CHUNK
    n_masked = min(n_chunks, max(0, -(first_pos // CHUNK)))
    for lo, hi, masked in ((0, n_masked, True), (n_masked, n_chunks, False)):
        if hi > lo:
            trips = hi - lo
            unroll = next(u for u in (ATT_UNROLL, 2, 1) if trips % u == 0)
            lax.fori_loop(lo, hi, make_chunk(masked), 0, unroll=unroll)


def _att_core(q, k, v, past_k, past_v, bias, first_pos, name):
    b, t, d = q.shape
    has_past = past_k is not None
    grp = lambda rows: pl.BlockSpec((None, rows, ATT_GW), lambda b_, g_: (b_, 0, g_))
    in_specs = [grp(t), grp(t), grp(t)]
    args = [q, k, v]
    if has_past:
        in_specs += [grp(BAND_ROWS), grp(BAND_ROWS)]
        args += [past_k, past_v]
    in_specs.append(pl.BlockSpec((None, ATT_GW, BAND), lambda b_, g_: (g_, 0, 0)))
    args.append(bias)
    return pl.pallas_call(
        functools.partial(_att_core_kernel, first_pos, has_past),
        grid=(b, ATT_NGROUPS),
        in_specs=in_specs,
        out_specs=grp(t),
        out_shape=jax.ShapeDtypeStruct((b, t, d), BF16),
        scratch_shapes=[pltpu.VMEM((BAND_ROWS + t, ATT_GW), BF16)] * 2,
        compiler_params=_cparams(2),
        name=name,
    )(*args)


def _att_bias_kernel(rb_ref, o_ref):
    rb = rb_ref[...]
    hi = rb.astype(BF16)
    r1 = rb - hi.astype(F32)
    mid = r1.astype(BF16)
    lo = (r1 - mid.astype(F32)).astype(BF16)
    r = lax.broadcasted_iota(jnp.int32, (REL_SIZE, REL_BASE_W), 0)
    u = lax.broadcasted_iota(jnp.int32, (REL_SIZE, REL_BASE_W), 1)
    rel = jnp.clip(BAND_ROWS + CHUNK - 1 - u, REL_MIN, REL_MAX) - REL_MIN
    sel = jnp.where(r == rel, 1.0, 0.0).astype(BF16)
    base = (jnp.dot(hi, sel, preferred_element_type=F32)
            + jnp.dot(mid, sel, preferred_element_type=F32)
            + jnp.dot(lo, sel, preferred_element_type=F32))
    for i in range(CHUNK):
        o_ref[i] = base[:, CHUNK - 1 - i:CHUNK - 1 - i + BAND]


def _att_bias_table(rel_bias):
    tab = pl.pallas_call(
        _att_bias_kernel,
        out_shape=jax.ShapeDtypeStruct((CHUNK, ATT_HEADS, BAND), F32),
        name="att_bias",
    )(rel_bias)
    return tab.transpose(1, 0, 2).reshape(ATT_NGROUPS, ATT_GW, BAND)


def _rope_tables(pos0, t):
    inv = ROPE_BASE ** (-np.arange(ROPE_HALF, dtype=np.float64) / ROPE_HALF)
    ang = (pos0 + np.arange(t, dtype=np.float64))[:, None] * inv[None, :]
    return jnp.asarray(np.cos(ang), F32), jnp.asarray(np.sin(ang), F32)


def _trunk(x, mod, row0, pos0, bb, tt, ret_lc, ret_states, pool_bufs, att_k_cache, att_v_cache, p):
    b, t, d = x.shape
    new_ret, new_pool, new_k, new_v = [], [], [], []
    cos, sin = _rope_tables(pos0, t)
    for i in range(DEPTH):
        x = _ffn(x, mod, p['norm_g'], p['ffn1_w_in'], p['ffn1_w_out'], i, 1, bb, tt, row0)
        j = i // N_MIXERS
        kind = i % N_MIXERS
        if kind == 0:
            q, k, v, g = _ret_proj(x, mod, p['norm_g'], p['ret_w_in'], cos, sin, ret_lc,
                                   i, j, bb, tt, row0)
            s0 = None if ret_states is None else ret_states[j]
            y, s_new = _ret_core(q, k, v, g, s0, p['ret_gn_g'][j].reshape(1, 2 * d), ret_lc,
                                 f"ret_core_l{i}")
            new_ret.append(s_new)
            x = _matres(y, x, mod, p['ret_w_out'], i, j, bb, tt, row0, f"ret_out_l{i}")
        elif kind == 1:
            if pool_bufs is None:
                buf16 = jnp.zeros((b, POOL_CARRY, d), F32)
            else:
                buf16 = jnp.pad(pool_bufs[j], ((0, 0), (POOL_CARRY - POOL_BUF, 0), (0, 0)))
            x, nb = _pool(x, mod, p['norm_g'], buf16, p['pool_w'], p['pool_b'], p['pool_scale'],
                          i, j, min(tt, t), row0, pos0)
            new_pool.append(nb)
        else:
            keep = min(BAND_ROWS, t)
            q, kb, vb, k_keep, v_keep = _att_proj(
                x, mod, p['norm_g'], p['att_w_qkv'], p['att_pavg'], p['att_q_g'], p['att_k_g'],
                keep, i, j, bb, tt, row0)
            new_k.append(k_keep.reshape(b, keep, ATT_HEADS, ATT_HD))
            new_v.append(v_keep.reshape(b, keep, ATT_HEADS, ATT_HD))
            if att_k_cache is None:
                past_k = past_v = None
            else:
                past_k = att_k_cache[j].reshape(b, BAND_ROWS, d)
                past_v = att_v_cache[j].reshape(b, BAND_ROWS, d)
            o = _att_core(q, kb, vb, past_k, past_v, p['att_bias'][j], pos0 - BAND_ROWS,
                          f"att_core_l{i}")
            x = _matres(o, x, mod, p['att_w_o'], i, j, bb, tt, row0, f"att_out_l{i}")
        x = _ffn(x, mod, p['norm_g'], p['ffn2_w_in'], p['ffn2_w_out'], i, 2, bb, tt, row0)
    return x, jnp.stack(new_ret), jnp.stack(new_pool), jnp.stack(new_k), jnp.stack(new_v)


def kernel(x_prompt, x_sample, c_prompt, c_sample, state_ret, state_pool, cache_att_k, cache_att_v, norm_g, ada_w, ada_b, ffn1_w_in, ffn1_w_out, ffn2_w_in, ffn2_w_out, ret_w_in, ret_w_out, ret_gn_g, pool_w, pool_b, pool_scale, att_w_qkv, att_w_o, att_q_g, att_k_g, att_rel_bias):
    assert cache_att_k.shape[2] == BAND_ROWS, "key cache must hold exactly the left band"
    bp, tp, d = x_prompt.shape
    bs, ts, _ = x_sample.shape
    n_att = att_w_qkv.shape[0]
    head_id = np.arange(d) // ATT_HD
    pavg = jnp.asarray((head_id[:, None] == head_id[None, :]) / ATT_HD, BF16)
    p = {
        'norm_g': norm_g.reshape(DEPTH, 3, 1, d),
        'ffn1_w_in': ffn1_w_in.astype(BF16), 'ffn1_w_out': ffn1_w_out.astype(BF16),
        'ffn2_w_in': ffn2_w_in.astype(BF16), 'ffn2_w_out': ffn2_w_out.astype(BF16),
        'ret_w_in': ret_w_in.astype(BF16), 'ret_w_out': ret_w_out.astype(BF16),
        'ret_gn_g': ret_gn_g,
        'pool_w': pool_w.astype(BF16),
        'pool_b': pool_b.reshape(-1, 1, d), 'pool_scale': pool_scale.reshape(-1, 1, d),
        'att_w_qkv': att_w_qkv.astype(BF16), 'att_w_o': att_w_o.astype(BF16),
        'att_q_g': jnp.tile(att_q_g, (1, ATT_HEADS)).reshape(n_att, 1, d),
        'att_k_g': jnp.tile(att_k_g, (1, ATT_HEADS)).reshape(n_att, 1, d),
        'att_pavg': pavg,
        'att_bias': [_att_bias_table(att_rel_bias[j]) for j in range(n_att)],
    }
    mod = _ada(jnp.concatenate([c_prompt, c_sample], axis=0), ada_w, ada_b)

    tt_p = min(PROMPT_ROWS, tp)
    y_prompt, ret_p, pool_p, k_p, v_p = _trunk(
        x_prompt, mod, 0, 0, 1, tt_p, min(RET_CHUNK_PROMPT, tp), None, None, None, None, p)
    bb_s = max(1, min(bs, PROMPT_ROWS // ts))
    y_sample, ret_s, pool_s, k_s, v_s = _trunk(
        x_sample, mod, bp, PAST_LEN, bb_s, ts, ts, state_ret, state_pool,
        cache_att_k, cache_att_v, p)
    return (y_prompt, y_sample, ret_p, ret_s, pool_p, pool_s, k_p, v_p, k_s, v_s)
```

```python
import functools

import numpy as np
import jax
import jax.numpy as jnp
from jax import lax
from jax.experimental import pallas as pl
from jax.experimental.pallas import tpu as pltpu

F32 = jnp.float32
BF16 = jnp.bfloat16

D_MODEL = 1024
DEPTH = 4
D_FF = 2816
EPS = 1e-6
CHUNK = 64
PAST_LEN = 2048
N_MIXERS = 3
RET_HEADS = 4
RET_DK = D_MODEL // RET_HEADS
RET_DV = 2 * D_MODEL // RET_HEADS
ROPE_BASE = 10000.0
ROPE_HALF = RET_DK // 2
POOL_WINDOWS = (2, 4, 8, 16)
POOL_GROUPS = 4
POOL_GC = D_MODEL // POOL_GROUPS
POOL_BUF = 15
POOL_CARRY = 16
ATT_HEADS = 16
ATT_HD = D_MODEL // ATT_HEADS
ATT_GROUP = 4
ATT_GW = ATT_GROUP * ATT_HD
ATT_NGROUPS = ATT_HEADS // ATT_GROUP
LEFT_CHUNKS = 8
BAND_ROWS = LEFT_CHUNKS * CHUNK
BAND = BAND_ROWS + CHUNK
REL_MIN = -(CHUNK - 1)
REL_MAX = 256
REL_SIZE = REL_MAX - REL_MIN + 1
REL_BASE_W = 640
NEG_INF = -1e30
N_MOD = 9

VMEM_LIMIT_BYTES = 56 * 1024 * 1024
PROMPT_ROWS = 512
RET_CHUNK_PROMPT = 256
ATT_UNROLL = 8


def _cparams(n_grid):
    return pltpu.CompilerParams(
        dimension_semantics=("arbitrary",) * n_grid,
        vmem_limit_bytes=VMEM_LIMIT_BYTES,
    )


def _resident(block_shape, index_map):
    return pl.BlockSpec(block_shape, index_map, pipeline_mode=pl.Buffered(1))


def _norm_mod(x, ng, sc, sh):
    ms = jnp.mean(x * x, axis=-1, keepdims=True)
    y = x * lax.rsqrt(ms + EPS) * ng
    return y * (1.0 + sc) + sh


def _silu(x):
    return x * (1.0 / (1.0 + jnp.exp(-x)))


def _ada_kernel(c_ref, w_ref, b_ref, o_ref):
    c = _silu(c_ref[...]).astype(BF16)
    w = w_ref[...].astype(BF16)
    o_ref[...] = jnp.dot(c, w, preferred_element_type=F32) + b_ref[...]


def _ada(c_all, ada_w, ada_b):
    n = c_all.shape[0]
    b4 = ada_b.reshape(DEPTH, N_MOD, 1, D_MODEL)
    out = pl.pallas_call(
        _ada_kernel,
        grid=(DEPTH, N_MOD),
        in_specs=[
            pl.BlockSpec((n, D_MODEL), lambda i, j: (0, 0)),
            pl.BlockSpec((None, D_MODEL, D_MODEL), lambda i, j: (i, 0, j)),
            pl.BlockSpec((None, None, 1, D_MODEL), lambda i, j: (i, j, 0, 0)),
        ],
        out_specs=pl.BlockSpec((None, None, n, D_MODEL), lambda i, j: (i, j, 0, 0)),
        out_shape=jax.ShapeDtypeStruct((DEPTH, N_MOD, n, D_MODEL), F32),
        compiler_params=_cparams(2),
        name="ada_mod",
    )(c_all, ada_w, b4)
    return out.reshape(DEPTH, N_MOD, n, 1, D_MODEL)


def _mod_spec(layer, j, bb, row0):
    blk0 = row0 // bb
    return pl.BlockSpec((None, None, bb, 1, D_MODEL),
                        lambda b, t: (layer, j, blk0 + b, 0, 0))


def _ng_spec(layer, j):
    return pl.BlockSpec((None, None, 1, D_MODEL), lambda b, t: (layer, j, 0, 0))


def _x_spec(bb, tt, width=D_MODEL):
    return pl.BlockSpec((bb, tt, width), lambda b, t: (b, t, 0))


def _ffn_kernel(x_ref, sh_ref, sc_ref, g_ref, ng_ref, win_ref, wout_ref, o_ref):
    bb, tt, d = x_ref.shape
    x = x_ref[...]
    h = _norm_mod(x, ng_ref[...], sc_ref[...], sh_ref[...])
    hb = h.reshape(bb * tt, d).astype(BF16)
    gu = jnp.dot(hb, win_ref[...], preferred_element_type=F32)
    act = (_silu(gu[:, :D_FF]) * gu[:, D_FF:]).astype(BF16)
    y = jnp.dot(act, wout_ref[...], preferred_element_type=F32)
    o_ref[...] = x + (0.5 * g_ref[...]) * y.reshape(bb, tt, d)


def _ffn(x, mod, norm_g4, w_in, w_out, layer, which, bb, tt, row0):
    b, t, d = x.shape
    j0 = 0 if which == 1 else 6
    nj = 0 if which == 1 else 2
    return pl.pallas_call(
        _ffn_kernel,
        grid=(b // bb, t // tt),
        in_specs=[
            _x_spec(bb, tt),
            _mod_spec(layer, j0, bb, row0),
            _mod_spec(layer, j0 + 1, bb, row0),
            _mod_spec(layer, j0 + 2, bb, row0),
            _ng_spec(layer, nj),
            _resident((None, D_MODEL, 2 * D_FF), lambda b_, t_: (layer, 0, 0)),
            _resident((None, D_FF, D_MODEL), lambda b_, t_: (layer, 0, 0)),
        ],
        out_specs=_x_spec(bb, tt),
        out_shape=jax.ShapeDtypeStruct(x.shape, F32),
        compiler_params=_cparams(2),
        name=f"ffn{which}_l{layer}",
    )(x, mod, mod, mod, norm_g4, w_in, w_out)


def _matres_kernel(a_ref, x_ref, g_ref, w_ref, o_ref):
    bb, tt, k = a_ref.shape
    y = jnp.dot(a_ref[...].reshape(bb * tt, k), w_ref[...], preferred_element_type=F32)
    o_ref[...] = x_ref[...] + g_ref[...] * y.reshape(bb, tt, D_MODEL)


def _matres(a, x, mod, w, layer, widx, bb, tt, row0, name):
    b, t, k = a.shape
    return pl.pallas_call(
        _matres_kernel,
        grid=(b // bb, t // tt),
        in_specs=[
            _x_spec(bb, tt, k),
            _x_spec(bb, tt),
            _mod_spec(layer, 5, bb, row0),
            _resident((None, k, D_MODEL), lambda b_, t_: (widx, 0, 0)),
        ],
        out_specs=_x_spec(bb, tt),
        out_shape=jax.ShapeDtypeStruct(x.shape, F32),
        compiler_params=_cparams(2),
        name=name,
    )(a, x, mod, w)


def _ret_proj_kernel(x_ref, sh_ref, sc_ref, ng_ref, w_ref, cos_ref, sin_ref, qs_ref, ks_ref,
                     q_ref, k_ref, v_ref, g_ref):
    bb, tt, d = x_ref.shape
    h = _norm_mod(x_ref[...], ng_ref[...], sc_ref[...], sh_ref[...])
    hb = h.reshape(bb * tt, d).astype(BF16)
    cos = cos_ref[...][None]
    sin = sin_ref[...][None]
    for idx, (o_ref, s_ref) in enumerate(((q_ref, qs_ref), (k_ref, ks_ref))):
        y = jnp.dot(hb, w_ref[:, idx * d:(idx + 1) * d], preferred_element_type=F32)
        y = y.reshape(bb, tt, d)
        for hd in range(RET_HEADS):
            lo = hd * RET_DK
            mid = lo + ROPE_HALF
            x1 = y[:, :, lo:mid]
            x2 = y[:, :, mid:lo + RET_DK]
            o_ref[:, :, lo:mid] = ((x1 * cos - x2 * sin) * s_ref[:, lo:mid][None]).astype(BF16)
            o_ref[:, :, mid:lo + RET_DK] = (
                (x1 * sin + x2 * cos) * s_ref[:, mid:lo + RET_DK][None]).astype(BF16)
    v = jnp.dot(hb, w_ref[:, 2 * d:4 * d], preferred_element_type=F32)
    v_ref[...] = v.reshape(bb, tt, 2 * d).astype(BF16)
    g = jnp.dot(hb, w_ref[:, 4 * d:6 * d], preferred_element_type=F32)
    g_ref[...] = _silu(g).reshape(bb, tt, 2 * d).astype(BF16)


def _ret_gamma():
    return 1.0 - 2.0 ** (-5.0 - np.arange(RET_HEADS, dtype=np.float64))


def _ret_scale_tables(tt, lc):
    gamma = _ret_gamma()
    i = (np.arange(tt) % lc).astype(np.float64)[:, None]
    g_lane = np.repeat(gamma, RET_DK)[None, :]
    qs = g_lane ** (i + 1.0)
    ks = (RET_DK ** -0.5) * g_lane ** (-(i + 1.0))
    return jnp.asarray(qs, F32), jnp.asarray(ks, F32)


def _ret_proj(x, mod, norm_g4, w_in, cos, sin, lc, layer, widx, bb, tt, row0):
    b, t, d = x.shape
    assert tt % lc == 0
    qs, ks = _ret_scale_tables(tt, lc)
    tab = pl.BlockSpec((tt, ROPE_HALF), lambda b_, t_: (t_, 0))
    scale = _resident((tt, d), lambda b_, t_: (0, 0))
    return pl.pallas_call(
        _ret_proj_kernel,
        grid=(b // bb, t // tt),
        in_specs=[
            _x_spec(bb, tt),
            _mod_spec(layer, 3, bb, row0),
            _mod_spec(layer, 4, bb, row0),
            _ng_spec(layer, 1),
            _resident((None, D_MODEL, 6 * D_MODEL), lambda b_, t_: (widx, 0, 0)),
            tab, tab, scale, scale,
        ],
        out_specs=[_x_spec(bb, tt), _x_spec(bb, tt),
                   _x_spec(bb, tt, 2 * d), _x_spec(bb, tt, 2 * d)],
        out_shape=[jax.ShapeDtypeStruct((b, t, d), BF16), jax.ShapeDtypeStruct((b, t, d), BF16),
                   jax.ShapeDtypeStruct((b, t, 2 * d), BF16), jax.ShapeDtypeStruct((b, t, 2 * d), BF16)],
        compiler_params=_cparams(2),
        name=f"ret_proj_l{layer}",
    )(x, mod, mod, norm_g4, w_in, cos, sin, qs, ks)


def _ret_head(qh, kh, vh, s, tril, gn, cdec_h):
    inner = lax.dot_general(qh, kh, (((1,), (1,)), ((), ())), preferred_element_type=F32) * tril
    o = (jnp.dot(inner.astype(BF16), vh, preferred_element_type=F32)
         + jnp.dot(qh, s.astype(BF16), preferred_element_type=F32))
    s_new = (s + lax.dot_general(kh, vh, (((0,), (0,)), ((), ())),
                                 preferred_element_type=F32)) * cdec_h
    mu = jnp.mean(o, axis=-1, keepdims=True)
    oc = o - mu
    var = jnp.mean(oc * oc, axis=-1, keepdims=True)
    return oc * lax.rsqrt(var + EPS) * gn, s_new


def _ret_core_kernel(cdec, has_state, has_alias, *refs):
    refs = list(refs)
    q_ref, k_ref, v_ref, g_ref = refs[:4]
    s0_ref = refs[4] if has_state else None
    tril_ref, gn_ref = refs[4 + has_state:6 + has_state]
    y_ref, s_ref = refs[6 + has_state + has_alias:]

    @pl.when(pl.program_id(1) == 0)
    def _():
        if has_state:
            s_ref[...] = s0_ref[...]
        else:
            s_ref[...] = jnp.zeros(s_ref.shape, F32)

    for hd in range(RET_HEADS):
        ksl = slice(hd * RET_DK, (hd + 1) * RET_DK)
        vsl = slice(hd * RET_DV, (hd + 1) * RET_DV)
        on, s_new = _ret_head(q_ref[:, ksl], k_ref[:, ksl], v_ref[:, vsl], s_ref[hd],
                              tril_ref[...], gn_ref[:, vsl], cdec[hd])
        s_ref[hd] = s_new
        y_ref[:, vsl] = (g_ref[:, vsl].astype(F32) * on).astype(BF16)


def _state_specs(n_states, j, b, s_all):
    spec = pl.BlockSpec((None, None, RET_HEADS, RET_DK, RET_DV), lambda b_, t_: (j, b_, 0, 0, 0))
    shape = jax.ShapeDtypeStruct((n_states, b, RET_HEADS, RET_DK, RET_DV), F32)
    return spec, shape


def _ret_core(q, k, v, g, s0_all, s_all, n_states, j, gn_g, lc, name):
    b, t, d = q.shape
    cdec = tuple(float(c) for c in _ret_gamma() ** float(lc))
    tril = jnp.asarray(np.tril(np.ones((lc, lc))), F32)
    has_state = s0_all is not None
    has_alias = s_all is not None
    row = lambda width: pl.BlockSpec((None, lc, width), lambda b_, t_: (b_, t_, 0))
    s_spec, s_shape = _state_specs(n_states, j, b, s_all)
    in_specs = [row(d), row(d), row(2 * d), row(2 * d)]
    args = [q, k, v, g]
    if has_state:
        in_specs.append(s_spec)
        args.append(s0_all)
    in_specs += [_resident((lc, lc), lambda b_, t_: (0, 0)),
                 pl.BlockSpec((1, 2 * d), lambda b_, t_: (0, 0))]
    args += [tril, gn_g]
    aliases = {}
    if has_alias:
        in_specs.append(pl.BlockSpec(memory_space=pl.ANY))
        args.append(s_all)
        aliases = {len(args) - 1: 1}
    return pl.pallas_call(
        functools.partial(_ret_core_kernel, cdec, has_state, has_alias),
        grid=(b, t // lc),
        in_specs=in_specs,
        out_specs=[row(2 * d), s_spec],
        out_shape=[jax.ShapeDtypeStruct((b, t, 2 * d), BF16), s_shape],
        input_output_aliases=aliases,
        compiler_params=_cparams(2),
        name=name,
    )(*args)


def _ret_fused_kernel(cdec, has_alias, *refs):
    refs = list(refs)
    (x_ref, sh_ref, sc_ref, gm_ref, ng_ref, win_ref, wout_ref, cos_ref, sin_ref, qs_ref, ks_ref,
     tril_ref, gn_ref) = refs[:13]
    o_ref, s_ref, q_scr, k_scr, v_scr, y_scr = refs[13 + has_alias:]
    tt, d = x_ref.shape

    @pl.when(pl.program_id(1) == 0)
    def _():
        s_ref[...] = jnp.zeros(s_ref.shape, F32)

    x = x_ref[...]
    hb = _norm_mod(x[None], ng_ref[...], sc_ref[...], sh_ref[...])[0].astype(BF16)
    cos = cos_ref[...]
    sin = sin_ref[...]
    for idx, (scr, tab) in enumerate(((q_scr, qs_ref), (k_scr, ks_ref))):
        y = jnp.dot(hb, win_ref[:, idx * d:(idx + 1) * d], preferred_element_type=F32)
        for hd in range(RET_HEADS):
            lo = hd * RET_DK
            mid = lo + ROPE_HALF
            hi = lo + RET_DK
            x1 = y[:, lo:mid]
            x2 = y[:, mid:hi]
            scr[:, lo:mid] = ((x1 * cos - x2 * sin) * tab[:, lo:mid]).astype(BF16)
            scr[:, mid:hi] = ((x1 * sin + x2 * cos) * tab[:, mid:hi]).astype(BF16)
    v_scr[...] = jnp.dot(hb, win_ref[:, 2 * d:4 * d], preferred_element_type=F32).astype(BF16)
    for hd in range(RET_HEADS):
        ksl = slice(hd * RET_DK, (hd + 1) * RET_DK)
        vsl = slice(hd * RET_DV, (hd + 1) * RET_DV)
        g = jnp.dot(hb, win_ref[:, 4 * d + hd * RET_DV:4 * d + (hd + 1) * RET_DV],
                    preferred_element_type=F32)
        on, s_new = _ret_head(q_scr[:, ksl], k_scr[:, ksl], v_scr[:, vsl], s_ref[hd],
                              tril_ref[...], gn_ref[:, vsl], cdec[hd])
        s_ref[hd] = s_new
        y_scr[:, vsl] = (_silu(g) * on).astype(BF16)
    out = jnp.dot(y_scr[...], wout_ref[...], preferred_element_type=F32)
    o_ref[...] = x + gm_ref[0] * out


def _ret_fused(x, mod, norm_g4, w_in, w_out, gn_g, cos, sin, s_all, n_states, lc, layer, widx, row0):
    b, t, d = x.shape
    cdec = tuple(float(c) for c in _ret_gamma() ** float(lc))
    tril = jnp.asarray(np.tril(np.ones((lc, lc))), F32)
    qs, ks = _ret_scale_tables(lc, lc)
    has_alias = s_all is not None
    xs = pl.BlockSpec((None, lc, d), lambda b_, t_: (b_, t_, 0))
    tab = pl.BlockSpec((lc, ROPE_HALF), lambda b_, t_: (t_, 0))
    const2 = lambda shape: _resident(shape, lambda b_, t_: (0, 0))
    s_spec, s_shape = _state_specs(n_states, widx, b, s_all)
    in_specs = [
        xs,
        _mod_spec(layer, 3, 1, row0), _mod_spec(layer, 4, 1, row0), _mod_spec(layer, 5, 1, row0),
        _ng_spec(layer, 1),
        _resident((None, d, 6 * d), lambda b_, t_: (widx, 0, 0)),
        _resident((None, 2 * d, d), lambda b_, t_: (widx, 0, 0)),
        tab, tab, const2((lc, d)), const2((lc, d)), const2((lc, lc)),
        pl.BlockSpec((1, 2 * d), lambda b_, t_: (0, 0)),
    ]
    args = [x, mod, mod, mod, norm_g4, w_in, w_out, cos, sin, qs, ks, tril, gn_g]
    aliases = {}
    if has_alias:
        in_specs.append(pl.BlockSpec(memory_space=pl.ANY))
        args.append(s_all)
        aliases = {len(args) - 1: 1}
    return pl.pallas_call(
        functools.partial(_ret_fused_kernel, cdec, has_alias),
        grid=(b, t // lc),
        in_specs=in_specs,
        out_specs=[xs, s_spec],
        out_shape=[jax.ShapeDtypeStruct(x.shape, F32), s_shape],
        scratch_shapes=[pltpu.VMEM((lc, d), BF16), pltpu.VMEM((lc, d), BF16),
                        pltpu.VMEM((lc, 2 * d), BF16), pltpu.VMEM((lc, 2 * d), BF16)],
        input_output_aliases=aliases,
        compiler_params=_cparams(2),
        name=f"ret_l{layer}",
    )(*args)


def _pool_kernel(pos0, x_ref, sh_ref, sc_ref, gm_ref, ng_ref, buf_ref, w_ref, pb_ref, ps_ref,
                 o_ref, nb_ref, carry_ref):
    tt, d = x_ref.shape
    t_idx = pl.program_id(1)

    @pl.when(t_idx == 0)
    def _():
        carry_ref[...] = buf_ref[...]

    x = x_ref[...]
    h = _norm_mod(x[None], ng_ref[...], sc_ref[...], sh_ref[...])[0]
    ext = jnp.concatenate([carry_ref[...], h], axis=0)
    carry_ref[...] = h[tt - POOL_CARRY:, :]
    nb_ref[...] = h[tt - POOL_BUF:, :]

    pos = pos0 + t_idx * tt + lax.broadcasted_iota(jnp.int32, (tt, 1), 0)
    run = ext
    w = 1
    outs = []
    for gi, win in enumerate(POOL_WINDOWS):
        while w < win:
            n = run.shape[0]
            run = run[w:, :] + run[:n - w, :]
            w *= 2
        wsum = run[run.shape[0] - tt:, :POOL_GC]
        inv_cnt = 1.0 / jnp.minimum(pos + 1, win).astype(F32)
        lo = gi * POOL_GC
        pooled = wsum * inv_cnt - h[:, lo:lo + POOL_GC]
        outs.append(jnp.dot(pooled.astype(BF16), w_ref[gi], preferred_element_type=F32))
        run = run[:, POOL_GC:]
    y = (jnp.concatenate(outs, axis=1) + pb_ref[...]) * ps_ref[...]
    o_ref[...] = x + gm_ref[0] * y


def _pool(x, mod, norm_g4, buf16, w, pb, ps, layer, widx, tt, row0, pos0):
    b, t, d = x.shape
    xs = pl.BlockSpec((None, tt, d), lambda b_, t_: (b_, t_, 0))
    vec = pl.BlockSpec((None, 1, d), lambda b_, t_: (widx, 0, 0))
    return pl.pallas_call(
        functools.partial(_pool_kernel, pos0),
        grid=(b, t // tt),
        in_specs=[
            xs,
            _mod_spec(layer, 3, 1, row0),
            _mod_spec(layer, 4, 1, row0),
            _mod_spec(layer, 5, 1, row0),
            _ng_spec(layer, 1),
            pl.BlockSpec((None, POOL_CARRY, d), lambda b_, t_: (b_, 0, 0)),
            pl.BlockSpec((None, POOL_GROUPS, POOL_GC, POOL_GC), lambda b_, t_: (widx, 0, 0, 0)),
            vec, vec,
        ],
        out_specs=[xs, pl.BlockSpec((None, POOL_BUF, d), lambda b_, t_: (b_, 0, 0))],
        out_shape=[jax.ShapeDtypeStruct(x.shape, F32),
                   jax.ShapeDtypeStruct((b, POOL_BUF, d), F32)],
        scratch_shapes=[pltpu.VMEM((POOL_CARRY, d), F32)],
        compiler_params=_cparams(2),
        name=f"pool_l{layer}",
    )(x, mod, mod, mod, norm_g4, buf16, w, pb, ps)


def _head_norm(y, pavg, gain):
    y2 = (y * y).astype(BF16)
    ms = jnp.concatenate(
        [jnp.dot(y2[:, g * ATT_GW:(g + 1) * ATT_GW], pavg, preferred_element_type=F32)
         for g in range(ATT_NGROUPS)], axis=1)
    return y * lax.rsqrt(ms + EPS) * gain


def _att_proj_kernel(first_keep_tile, x_ref, sh_ref, sc_ref, ng_ref, w_ref, p_ref, qg_ref, kg_ref,
                     q_ref, kb_ref, vb_ref, kk_ref, vk_ref):
    bb, tt, d = x_ref.shape
    h = _norm_mod(x_ref[...], ng_ref[...], sc_ref[...], sh_ref[...])
    hb = h.reshape(bb * tt, d).astype(BF16)
    q = jnp.dot(hb, w_ref[:, 0:d], preferred_element_type=F32)
    q = _head_norm(q, p_ref[...], qg_ref[...]) * (ATT_HD ** -0.5)
    q_ref[...] = q.reshape(bb, tt, d).astype(BF16)
    k = jnp.dot(hb, w_ref[:, d:2 * d], preferred_element_type=F32)
    k = _head_norm(k, p_ref[...], kg_ref[...]).reshape(bb, tt, d)
    kb_ref[...] = k.astype(BF16)
    v = jnp.dot(hb, w_ref[:, 2 * d:3 * d], preferred_element_type=F32).reshape(bb, tt, d)
    vb_ref[...] = v.astype(BF16)

    @pl.when(pl.program_id(1) >= first_keep_tile)
    def _():
        kk_ref[...] = k
        vk_ref[...] = v


def _att_proj(x, mod, norm_g4, w_qkv, pavg, qg, kg, keep, layer, widx, bb, tt, row0):
    b, t, d = x.shape
    assert keep % tt == 0 and t % tt == 0
    first_keep_tile = (t - keep) // tt
    vec = pl.BlockSpec((None, 1, d), lambda b_, t_: (widx, 0, 0))
    keep_spec = pl.BlockSpec((bb, tt, d),
                             lambda b_, t_: (b_, jnp.maximum(t_ - first_keep_tile, 0), 0))
    b16o = jax.ShapeDtypeStruct((b, t, d), BF16)
    f32k = jax.ShapeDtypeStruct((b, keep, d), F32)
    return pl.pallas_call(
        functools.partial(_att_proj_kernel, first_keep_tile),
        grid=(b // bb, t // tt),
        in_specs=[
            _x_spec(bb, tt),
            _mod_spec(layer, 3, bb, row0),
            _mod_spec(layer, 4, bb, row0),
            _ng_spec(layer, 1),
            _resident((None, d, 3 * d), lambda b_, t_: (widx, 0, 0)),
            _resident((ATT_GW, ATT_GW), lambda b_, t_: (0, 0)),
            vec, vec,
        ],
        out_specs=[_x_spec(bb, tt)] * 3 + [keep_spec] * 2,
        out_shape=[b16o, b16o, b16o, f32k, f32k],
        compiler_params=_cparams(2),
        name=f"att_proj_l{layer}",
    )(x, mod, mod, norm_g4, w_qkv, pavg, qg, kg)


def _head_block_diag():
    rb = lax.broadcasted_iota(jnp.int32, (ATT_GW, ATT_GW), 0) // ATT_HD
    cb = lax.broadcasted_iota(jnp.int32, (ATT_GW, ATT_GW), 1) // ATT_HD
    return rb == cb


def _att_chunk_group(qc, kb, vb, bias, diag, first_kpos):
    qbd = jnp.where(diag, jnp.concatenate([qc] * ATT_GROUP, axis=0), jnp.zeros((), BF16))
    s = lax.dot_general(qbd, kb, (((1,), (1,)), ((), ())), preferred_element_type=F32) + bias
    if first_kpos is not None:
        kpos = first_kpos + lax.broadcasted_iota(jnp.int32, (1, BAND), 1)
        s = jnp.where(kpos >= 0, s, NEG_INF)
    m = jnp.max(s, axis=-1, keepdims=True)
    p = jnp.exp(s - m)
    l = jnp.sum(p, axis=-1, keepdims=True)
    o = jnp.dot(p.astype(BF16), vb, preferred_element_type=F32) * (1.0 / l)
    o = jnp.where(diag, o, 0.0)
    og = o[0:CHUNK]
    for hh in range(1, ATT_GROUP):
        og = og + o[hh * CHUNK:(hh + 1) * CHUNK]
    return og


def _att_core_kernel(first_pos, has_past, *refs):
    if has_past:
        q_ref, k_ref, v_ref, pk_ref, pv_ref, bias_ref, o_ref, kbuf, vbuf = refs
    else:
        q_ref, k_ref, v_ref, bias_ref, o_ref, kbuf, vbuf = refs
    t = q_ref.shape[0]
    if has_past:
        kbuf[0:BAND_ROWS, :] = pk_ref[...].astype(BF16)
        vbuf[0:BAND_ROWS, :] = pv_ref[...].astype(BF16)
    else:
        kbuf[0:BAND_ROWS, :] = jnp.zeros((BAND_ROWS, ATT_GW), BF16)
        vbuf[0:BAND_ROWS, :] = jnp.zeros((BAND_ROWS, ATT_GW), BF16)
    kbuf[BAND_ROWS:, :] = k_ref[...]
    vbuf[BAND_ROWS:, :] = v_ref[...]
    diag = _head_block_diag()

    def make_chunk(masked):
        def chunk(n, carry):
            r0 = pl.multiple_of(n * CHUNK, CHUNK)
            og = _att_chunk_group(q_ref[pl.ds(r0, CHUNK), :], kbuf[pl.ds(r0, BAND), :],
                                  vbuf[pl.ds(r0, BAND), :], bias_ref[...], diag,
                                  first_pos + r0 if masked else None)
            o_ref[pl.ds(r0, CHUNK), :] = og.astype(BF16)
            return carry
        return chunk

    n_chunks = t // CHUNK
    n_masked = min(n_chunks, max(0, -(first_pos // CHUNK)))
    for lo, hi, masked in ((0, n_masked, True), (n_masked, n_chunks, False)):
        if hi > lo:
            trips = hi - lo
            unroll = next(u for u in (ATT_UNROLL, 2, 1) if trips % u == 0)
            lax.fori_loop(lo, hi, make_chunk(masked), 0, unroll=unroll)


def _att_core(q, k, v, past_k, past_v, bias, first_pos, name):
    b, t, d = q.shape
    has_past = past_k is not None
    grp = lambda rows: pl.BlockSpec((None, rows, ATT_GW), lambda b_, g_: (b_, 0, g_))
    in_specs = [grp(t), grp(t), grp(t)]
    args = [q, k, v]
    if has_past:
        in_specs += [grp(BAND_ROWS), grp(BAND_ROWS)]
        args += [past_k, past_v]
    in_specs.append(pl.BlockSpec((None, ATT_GW, BAND), lambda b_, g_: (g_, 0, 0)))
    args.append(bias)
    return pl.pallas_call(
        functools.partial(_att_core_kernel, first_pos, has_past),
        grid=(b, ATT_NGROUPS),
        in_specs=in_specs,
        out_specs=grp(t),
        out_shape=jax.ShapeDtypeStruct((b, t, d), BF16),
        scratch_shapes=[pltpu.VMEM((BAND_ROWS + t, ATT_GW), BF16)] * 2,
        compiler_params=_cparams(2),
        name=name,
    )(*args)


def _att_fused_kernel(first_keep_tile, x_ref, sh_ref, sc_ref, gm_ref, ng_ref, wqkv_ref, p_ref,
                      qg_ref, kg_ref, bias_ref, wo_ref, o_ref, kk_ref, vk_ref,
                      q_scr, kbuf, vbuf, a_scr):
    tt, d = x_ref.shape
    t_idx = pl.program_id(1)

    @pl.when(t_idx == 0)
    def _():
        kbuf[0:BAND_ROWS, :] = jnp.zeros((BAND_ROWS, d), BF16)
        vbuf[0:BAND_ROWS, :] = jnp.zeros((BAND_ROWS, d), BF16)

    @pl.when(t_idx > 0)
    def _():
        kbuf[0:BAND_ROWS, :] = kbuf[tt:tt + BAND_ROWS, :]
        vbuf[0:BAND_ROWS, :] = vbuf[tt:tt + BAND_ROWS, :]

    hb = _norm_mod(x_ref[...][None], ng_ref[...], sc_ref[...], sh_ref[...])[0].astype(BF16)
    q = jnp.dot(hb, wqkv_ref[:, 0:d], preferred_element_type=F32)
    q_scr[...] = (_head_norm(q, p_ref[...], qg_ref[...]) * (ATT_HD ** -0.5)).astype(BF16)
    k = jnp.dot(hb, wqkv_ref[:, d:2 * d], preferred_element_type=F32)
    k = _head_norm(k, p_ref[...], kg_ref[...])
    kbuf[BAND_ROWS:, :] = k.astype(BF16)
    v = jnp.dot(hb, wqkv_ref[:, 2 * d:3 * d], preferred_element_type=F32)
    vbuf[BAND_ROWS:, :] = v.astype(BF16)

    @pl.when(t_idx >= first_keep_tile)
    def _():
        kk_ref[...] = k
        vk_ref[...] = v

    diag = _head_block_diag()

    def make_chunk(masked):
        def chunk(n, carry):
            r0 = pl.multiple_of(n * CHUNK, CHUNK)
            for g in range(ATT_NGROUPS):
                gsl = slice(g * ATT_GW, (g + 1) * ATT_GW)
                og = _att_chunk_group(q_scr[pl.ds(r0, CHUNK), gsl], kbuf[pl.ds(r0, BAND), gsl],
                                      vbuf[pl.ds(r0, BAND), gsl], bias_ref[g], diag,
                                      r0 - BAND_ROWS if masked else None)
                a_scr[pl.ds(r0, CHUNK), gsl] = og.astype(BF16)
            return carry
        return chunk

    @pl.when(t_idx == 0)
    def _():
        lax.fori_loop(0, tt // CHUNK, make_chunk(True), 0, unroll=2)

    @pl.when(t_idx > 0)
    def _():
        lax.fori_loop(0, tt // CHUNK, make_chunk(False), 0, unroll=2)

    out = jnp.dot(a_scr[...], wo_ref[...], preferred_element_type=F32)
    o_ref[...] = x_ref[...] + gm_ref[0] * out


def _att_fused(x, mod, norm_g4, w_qkv, w_o, pavg, qg, kg, bias, keep, layer, widx, tt, row0):
    b, t, d = x.shape
    assert tt >= BAND_ROWS and tt % CHUNK == 0 and (tt // CHUNK) % 2 == 0
    assert keep % tt == 0 and t % tt == 0
    first_keep_tile = (t - keep) // tt
    xs = pl.BlockSpec((None, tt, d), lambda b_, t_: (b_, t_, 0))
    vec = pl.BlockSpec((None, 1, d), lambda b_, t_: (widx, 0, 0))
    keep_spec = pl.BlockSpec((None, tt, d),
                             lambda b_, t_: (b_, jnp.maximum(t_ - first_keep_tile, 0), 0))
    f32k = jax.ShapeDtypeStruct((b, keep, d), F32)
    return pl.pallas_call(
        functools.partial(_att_fused_kernel, first_keep_tile),
        grid=(b, t // tt),
        in_specs=[
            xs,
            _mod_spec(layer, 3, 1, row0), _mod_spec(layer, 4, 1, row0), _mod_spec(layer, 5, 1, row0),
            _ng_spec(layer, 1),
            _resident((None, d, 3 * d), lambda b_, t_: (widx, 0, 0)),
            _resident((ATT_GW, ATT_GW), lambda b_, t_: (0, 0)),
            vec, vec,
            _resident((ATT_NGROUPS, ATT_GW, BAND), lambda b_, t_: (0, 0, 0)),
            _resident((None, d, d), lambda b_, t_: (widx, 0, 0)),
        ],
        out_specs=[xs, keep_spec, keep_spec],
        out_shape=[jax.ShapeDtypeStruct(x.shape, F32), f32k, f32k],
        scratch_shapes=[pltpu.VMEM((tt, d), BF16), pltpu.VMEM((BAND_ROWS + tt, d), BF16),
                        pltpu.VMEM((BAND_ROWS + tt, d), BF16), pltpu.VMEM((tt, d), BF16)],
        compiler_params=_cparams(2),
        name=f"att_l{layer}",
    )(x, mod, mod, mod, norm_g4, w_qkv, pavg, qg, kg, bias, w_o)


def _att_bias_kernel(rb_ref, o_ref):
    rb = rb_ref[...]
    hi = rb.astype(BF16)
    r1 = rb - hi.astype(F32)
    mid = r1.astype(BF16)
    lo = (r1 - mid.astype(F32)).astype(BF16)
    r = lax.broadcasted_iota(jnp.int32, (REL_SIZE, REL_BASE_W), 0)
    u = lax.broadcasted_iota(jnp.int32, (REL_SIZE, REL_BASE_W), 1)
    rel = jnp.clip(BAND_ROWS + CHUNK - 1 - u, REL_MIN, REL_MAX) - REL_MIN
    sel = jnp.where(r == rel, 1.0, 0.0).astype(BF16)
    base = (jnp.dot(hi, sel, preferred_element_type=F32)
            + jnp.dot(mid, sel, preferred_element_type=F32)
            + jnp.dot(lo, sel, preferred_element_type=F32))
    for i in range(CHUNK):
        o_ref[i] = base[:, CHUNK - 1 - i:CHUNK - 1 - i + BAND]


def _att_bias_table(rel_bias):
    tab = pl.pallas_call(
        _att_bias_kernel,
        out_shape=jax.ShapeDtypeStruct((CHUNK, ATT_HEADS, BAND), F32),
        name="att_bias",
    )(rel_bias)
    return tab.transpose(1, 0, 2).reshape(ATT_NGROUPS, ATT_GW, BAND)


def _rope_tables(pos0, t):
    inv = ROPE_BASE ** (-np.arange(ROPE_HALF, dtype=np.float64) / ROPE_HALF)
    ang = (pos0 + np.arange(t, dtype=np.float64))[:, None] * inv[None, :]
    return jnp.asarray(np.cos(ang), F32), jnp.asarray(np.sin(ang), F32)


def _trunk(x, mod, row0, pos0, bb, tt, ret_lc, ret_states, pool_bufs, att_k_cache, att_v_cache, p):
    b, t, d = x.shape
    new_pool, new_k, new_v = [], [], []
    ret_all = None
    n_ret = len(range(0, DEPTH, N_MIXERS))
    cos, sin = _rope_tables(pos0, t)
    for i in range(DEPTH):
        x = _ffn(x, mod, p['norm_g'], p['ffn1_w_in'], p['ffn1_w_out'], i, 1, bb, tt, row0)
        j = i // N_MIXERS
        kind = i % N_MIXERS
        if kind == 0:
            gn = p['ret_gn_g'][j].reshape(1, 2 * d)
            if ret_states is None:
                x, ret_all = _ret_fused(x, mod, p['norm_g'], p['ret_w_in'], p['ret_w_out'], gn,
                                        cos, sin, ret_all, n_ret, ret_lc, i, j, row0)
            else:
                q, k, v, g = _ret_proj(x, mod, p['norm_g'], p['ret_w_in'], cos, sin, ret_lc,
                                       i, j, bb, tt, row0)
                y, ret_all = _ret_core(q, k, v, g, ret_states, ret_all, n_ret, j, gn, ret_lc,
                                       f"ret_core_l{i}")
                x = _matres(y, x, mod, p['ret_w_out'], i, j, bb, tt, row0, f"ret_out_l{i}")
        elif kind == 1:
            if pool_bufs is None:
                buf16 = jnp.zeros((b, POOL_CARRY, d), F32)
            else:
                buf16 = jnp.pad(pool_bufs[j], ((0, 0), (POOL_CARRY - POOL_BUF, 0), (0, 0)))
            x, nb = _pool(x, mod, p['norm_g'], buf16, p['pool_w'], p['pool_b'], p['pool_scale'],
                          i, j, min(tt, t), row0, pos0)
            new_pool.append(nb)
        else:
            keep = min(BAND_ROWS, t)
            if att_k_cache is None and pos0 == 0 and tt >= BAND_ROWS:
                x, k_keep, v_keep = _att_fused(
                    x, mod, p['norm_g'], p['att_w_qkv'], p['att_w_o'], p['att_pavg'],
                    p['att_q_g'], p['att_k_g'], p['att_bias'][j], keep, i, j, tt, row0)
            else:
                q, kb, vb, k_keep, v_keep = _att_proj(
                    x, mod, p['norm_g'], p['att_w_qkv'], p['att_pavg'], p['att_q_g'],
                    p['att_k_g'], keep, i, j, bb, tt, row0)
                if att_k_cache is None:
                    past_k = past_v = None
                else:
                    past_k = att_k_cache[j].reshape(b, BAND_ROWS, d)
                    past_v = att_v_cache[j].reshape(b, BAND_ROWS, d)
                o = _att_core(q, kb, vb, past_k, past_v, p['att_bias'][j], pos0 - BAND_ROWS,
                              f"att_core_l{i}")
                x = _matres(o, x, mod, p['att_w_o'], i, j, bb, tt, row0, f"att_out_l{i}")
            new_k.append(k_keep.reshape(b, keep, ATT_HEADS, ATT_HD))
            new_v.append(v_keep.reshape(b, keep, ATT_HEADS, ATT_HD))
        x = _ffn(x, mod, p['norm_g'], p['ffn2_w_in'], p['ffn2_w_out'], i, 2, bb, tt, row0)
    return x, ret_all, jnp.stack(new_pool), jnp.stack(new_k), jnp.stack(new_v)


def kernel(x_prompt, x_sample, c_prompt, c_sample, state_ret, state_pool, cache_att_k, cache_att_v, norm_g, ada_w, ada_b, ffn1_w_in, ffn1_w_out, ffn2_w_in, ffn2_w_out, ret_w_in, ret_w_out, ret_gn_g, pool_w, pool_b, pool_scale, att_w_qkv, att_w_o, att_q_g, att_k_g, att_rel_bias):
    assert cache_att_k.shape[2] == BAND_ROWS, "key cache must hold exactly the left band"
    bp, tp, d = x_prompt.shape
    bs, ts, _ = x_sample.shape
    n_att = att_w_qkv.shape[0]
    head_id = np.arange(ATT_GW) // ATT_HD
    pavg = jnp.asarray((head_id[:, None] == head_id[None, :]) / ATT_HD, BF16)
    p = {
        'norm_g': norm_g.reshape(DEPTH, 3, 1, d),
        'ffn1_w_in': ffn1_w_in.astype(BF16), 'ffn1_w_out': ffn1_w_out.astype(BF16),
        'ffn2_w_in': ffn2_w_in.astype(BF16), 'ffn2_w_out': ffn2_w_out.astype(BF16),
        'ret_w_in': ret_w_in.astype(BF16), 'ret_w_out': ret_w_out.astype(BF16),
        'ret_gn_g': ret_gn_g,
        'pool_w': pool_w.astype(BF16),
        'pool_b': pool_b.reshape(-1, 1, d), 'pool_scale': pool_scale.reshape(-1, 1, d),
        'att_w_qkv': att_w_qkv.astype(BF16), 'att_w_o': att_w_o.astype(BF16),
        'att_q_g': jnp.tile(att_q_g, (1, ATT_HEADS)).reshape(n_att, 1, d),
        'att_k_g': jnp.tile(att_k_g, (1, ATT_HEADS)).reshape(n_att, 1, d),
        'att_pavg': pavg,
        'att_bias': [_att_bias_table(att_rel_bias[j]) for j in range(n_att)],
    }
    mod = _ada(jnp.concatenate([c_prompt, c_sample], axis=0), ada_w, ada_b)

    tt_p = min(PROMPT_ROWS, tp)
    y_prompt, ret_p, pool_p, k_p, v_p = _trunk(
        x_prompt, mod, 0, 0, 1, tt_p, min(RET_CHUNK_PROMPT, tp), None, None, None, None, p)
    bb_s = max(1, min(bs, PROMPT_ROWS // ts))
    y_sample, ret_s, pool_s, k_s, v_s = _trunk(
        x_sample, mod, bp, PAST_LEN, bb_s, ts, ts, state_ret, state_pool,
        cache_att_k, cache_att_v, p)
    return (y_prompt, y_sample, ret_p, ret_s, pool_p, pool_s, k_p, v_p, k_s, v_s)
```

```python
import functools

import numpy as np
import jax
import jax.numpy as jnp
from jax import lax
from jax.experimental import pallas as pl
from jax.experimental.pallas import tpu as pltpu

F32 = jnp.float32
BF16 = jnp.bfloat16

D_MODEL = 1024
DEPTH = 4
D_FF = 2816
EPS = 1e-6
CHUNK = 64
PAST_LEN = 2048
N_MIXERS = 3
RET_HEADS = 4
RET_DK = D_MODEL // RET_HEADS
RET_DV = 2 * D_MODEL // RET_HEADS
ROPE_BASE = 10000.0
ROPE_HALF = RET_DK // 2
POOL_WINDOWS = (2, 4, 8, 16)
POOL_GROUPS = 4
POOL_GC = D_MODEL // POOL_GROUPS
POOL_BUF = 15
POOL_CARRY = 16
ATT_HEADS = 16
ATT_HD = D_MODEL // ATT_HEADS
ATT_GROUP = 4
ATT_GW = ATT_GROUP * ATT_HD
ATT_NGROUPS = ATT_HEADS // ATT_GROUP
LEFT_CHUNKS = 8
BAND_ROWS = LEFT_CHUNKS * CHUNK
BAND = BAND_ROWS + CHUNK
REL_MIN = -(CHUNK - 1)
REL_MAX = 256
REL_SIZE = REL_MAX - REL_MIN + 1
REL_BASE_W = 640
NEG_INF = -1e30
LOG2E = 1.4426950408889634
ATT_Q_SCALE = ATT_HD ** -0.5 * LOG2E
N_MOD = 9

VMEM_LIMIT_BYTES = 56 * 1024 * 1024
PROMPT_ROWS = 512
FFN_ROWS = 1024
RET_CHUNK_PROMPT = 256
ATT_UNROLL = 8


def _cparams(n_grid):
    return pltpu.CompilerParams(
        dimension_semantics=("arbitrary",) * n_grid,
        vmem_limit_bytes=VMEM_LIMIT_BYTES,
    )


def _resident(block_shape, index_map):
    return pl.BlockSpec(block_shape, index_map, pipeline_mode=pl.Buffered(1))


def _norm_mod(x, ng, sc, sh):
    ms = jnp.mean(x * x, axis=-1, keepdims=True)
    y = x * lax.rsqrt(ms + EPS) * ng
    return y * (1.0 + sc) + sh


def _silu(x):
    return x * (1.0 / (1.0 + jnp.exp(-x)))


def _ada_kernel(c_ref, w_ref, b_ref, o_ref):
    c = _silu(c_ref[...]).astype(BF16)
    w = w_ref[...].astype(BF16)
    o_ref[...] = jnp.dot(c, w, preferred_element_type=F32) + b_ref[...]


def _ada(c_all, ada_w, ada_b):
    n = c_all.shape[0]
    b4 = ada_b.reshape(DEPTH, N_MOD, 1, D_MODEL)
    out = pl.pallas_call(
        _ada_kernel,
        grid=(DEPTH, N_MOD),
        in_specs=[
            pl.BlockSpec((n, D_MODEL), lambda i, j: (0, 0)),
            pl.BlockSpec((None, D_MODEL, D_MODEL), lambda i, j: (i, 0, j)),
            pl.BlockSpec((None, None, 1, D_MODEL), lambda i, j: (i, j, 0, 0)),
        ],
        out_specs=pl.BlockSpec((None, None, n, D_MODEL), lambda i, j: (i, j, 0, 0)),
        out_shape=jax.ShapeDtypeStruct((DEPTH, N_MOD, n, D_MODEL), F32),
        compiler_params=_cparams(2),
        name="ada_mod",
    )(c_all, ada_w, b4)
    return out.reshape(DEPTH, N_MOD, n, 1, D_MODEL)


def _mod_spec(layer, j, bb, row0):
    blk0 = row0 // bb
    return pl.BlockSpec((None, None, bb, 1, D_MODEL),
                        lambda b, t: (layer, j, blk0 + b, 0, 0))


def _ng_spec(layer, j):
    return pl.BlockSpec((None, None, 1, D_MODEL), lambda b, t: (layer, j, 0, 0))


def _x_spec(bb, tt, width=D_MODEL):
    return pl.BlockSpec((bb, tt, width), lambda b, t: (b, t, 0))


def _ffn_kernel(x_ref, sh_ref, sc_ref, g_ref, ng_ref, win_ref, wout_ref, o_ref):
    bb, tt, d = x_ref.shape
    x = x_ref[...]
    h = _norm_mod(x, ng_ref[...], sc_ref[...], sh_ref[...])
    hb = h.reshape(bb * tt, d).astype(BF16)
    gu = jnp.dot(hb, win_ref[...], preferred_element_type=F32)
    act = (_silu(gu[:, :D_FF]) * gu[:, D_FF:]).astype(BF16)
    y = jnp.dot(act, wout_ref[...], preferred_element_type=F32)
    o_ref[...] = x + (0.5 * g_ref[...]) * y.reshape(bb, tt, d)


def _ffn(x, mod, norm_g4, w_in, w_out, layer, which, bb, tt, row0):
    b, t, d = x.shape
    j0 = 0 if which == 1 else 6
    nj = 0 if which == 1 else 2
    return pl.pallas_call(
        _ffn_kernel,
        grid=(b // bb, t // tt),
        in_specs=[
            _x_spec(bb, tt),
            _mod_spec(layer, j0, bb, row0),
            _mod_spec(layer, j0 + 1, bb, row0),
            _mod_spec(layer, j0 + 2, bb, row0),
            _ng_spec(layer, nj),
            _resident((None, D_MODEL, 2 * D_FF), lambda b_, t_: (layer, 0, 0)),
            _resident((None, D_FF, D_MODEL), lambda b_, t_: (layer, 0, 0)),
        ],
        out_specs=_x_spec(bb, tt),
        out_shape=jax.ShapeDtypeStruct(x.shape, F32),
        compiler_params=_cparams(2),
        name=f"ffn{which}_l{layer}",
    )(x, mod, mod, mod, norm_g4, w_in, w_out)


def _matres_kernel(a_ref, x_ref, g_ref, w_ref, o_ref):
    bb, tt, k = a_ref.shape
    y = jnp.dot(a_ref[...].reshape(bb * tt, k), w_ref[...], preferred_element_type=F32)
    o_ref[...] = x_ref[...] + g_ref[...] * y.reshape(bb, tt, D_MODEL)


def _matres(a, x, mod, w, layer, widx, bb, tt, row0, name):
    b, t, k = a.shape
    return pl.pallas_call(
        _matres_kernel,
        grid=(b // bb, t // tt),
        in_specs=[
            _x_spec(bb, tt, k),
            _x_spec(bb, tt),
            _mod_spec(layer, 5, bb, row0),
            _resident((None, k, D_MODEL), lambda b_, t_: (widx, 0, 0)),
        ],
        out_specs=_x_spec(bb, tt),
        out_shape=jax.ShapeDtypeStruct(x.shape, F32),
        compiler_params=_cparams(2),
        name=name,
    )(a, x, mod, w)


def _ret_proj_kernel(x_ref, sh_ref, sc_ref, ng_ref, w_ref, cos_ref, sin_ref, qs_ref, ks_ref,
                     q_ref, k_ref, v_ref, g_ref):
    bb, tt, d = x_ref.shape
    h = _norm_mod(x_ref[...], ng_ref[...], sc_ref[...], sh_ref[...])
    hb = h.reshape(bb * tt, d).astype(BF16)
    cos = cos_ref[...][None]
    sin = sin_ref[...][None]
    for idx, (o_ref, s_ref) in enumerate(((q_ref, qs_ref), (k_ref, ks_ref))):
        y = jnp.dot(hb, w_ref[:, idx * d:(idx + 1) * d], preferred_element_type=F32)
        y = y.reshape(bb, tt, d)
        for hd in range(RET_HEADS):
            lo = hd * RET_DK
            mid = lo + ROPE_HALF
            x1 = y[:, :, lo:mid]
            x2 = y[:, :, mid:lo + RET_DK]
            o_ref[:, :, lo:mid] = ((x1 * cos - x2 * sin) * s_ref[:, lo:mid][None]).astype(BF16)
            o_ref[:, :, mid:lo + RET_DK] = (
                (x1 * sin + x2 * cos) * s_ref[:, mid:lo + RET_DK][None]).astype(BF16)
    v = jnp.dot(hb, w_ref[:, 2 * d:4 * d], preferred_element_type=F32)
    v_ref[...] = v.reshape(bb, tt, 2 * d).astype(BF16)
    g = jnp.dot(hb, w_ref[:, 4 * d:6 * d], preferred_element_type=F32)
    g_ref[...] = _silu(g).reshape(bb, tt, 2 * d).astype(BF16)


def _ret_gamma():
    return 1.0 - 2.0 ** (-5.0 - np.arange(RET_HEADS, dtype=np.float64))


def _ret_scale_tables(tt, lc):
    gamma = _ret_gamma()
    i = (np.arange(tt) % lc).astype(np.float64)[:, None]
    g_lane = np.repeat(gamma, RET_DK)[None, :]
    qs = g_lane ** (i + 1.0)
    ks = (RET_DK ** -0.5) * g_lane ** (-(i + 1.0))
    return jnp.asarray(qs, F32), jnp.asarray(ks, F32)


def _ret_proj(x, mod, norm_g4, w_in, cos, sin, lc, layer, widx, bb, tt, row0):
    b, t, d = x.shape
    assert tt % lc == 0
    qs, ks = _ret_scale_tables(tt, lc)
    tab = pl.BlockSpec((tt, ROPE_HALF), lambda b_, t_: (t_, 0))
    scale = _resident((tt, d), lambda b_, t_: (0, 0))
    return pl.pallas_call(
        _ret_proj_kernel,
        grid=(b // bb, t // tt),
        in_specs=[
            _x_spec(bb, tt),
            _mod_spec(layer, 3, bb, row0),
            _mod_spec(layer, 4, bb, row0),
            _ng_spec(layer, 1),
            _resident((None, D_MODEL, 6 * D_MODEL), lambda b_, t_: (widx, 0, 0)),
            tab, tab, scale, scale,
        ],
        out_specs=[_x_spec(bb, tt), _x_spec(bb, tt),
                   _x_spec(bb, tt, 2 * d), _x_spec(bb, tt, 2 * d)],
        out_shape=[jax.ShapeDtypeStruct((b, t, d), BF16), jax.ShapeDtypeStruct((b, t, d), BF16),
                   jax.ShapeDtypeStruct((b, t, 2 * d), BF16), jax.ShapeDtypeStruct((b, t, 2 * d), BF16)],
        compiler_params=_cparams(2),
        name=f"ret_proj_l{layer}",
    )(x, mod, mod, norm_g4, w_in, cos, sin, qs, ks)


def _ret_head(qh, kh, vh, s, tril, gn, cdec_h):
    inner = lax.dot_general(qh, kh, (((1,), (1,)), ((), ())), preferred_element_type=F32) * tril
    o = (jnp.dot(inner.astype(BF16), vh, preferred_element_type=F32)
         + jnp.dot(qh, s.astype(BF16), preferred_element_type=F32))
    s_new = (s + lax.dot_general(kh, vh, (((0,), (0,)), ((), ())),
                                 preferred_element_type=F32)) * cdec_h
    mu = jnp.mean(o, axis=-1, keepdims=True)
    oc = o - mu
    var = jnp.mean(oc * oc, axis=-1, keepdims=True)
    return oc * lax.rsqrt(var + EPS) * gn, s_new


def _ret_core_kernel(cdec, has_state, has_alias, *refs):
    refs = list(refs)
    q_ref, k_ref, v_ref, g_ref = refs[:4]
    s0_ref = refs[4] if has_state else None
    tril_ref, gn_ref = refs[4 + has_state:6 + has_state]
    y_ref, s_ref = refs[6 + has_state + has_alias:]

    @pl.when(pl.program_id(1) == 0)
    def _():
        if has_state:
            s_ref[...] = s0_ref[...]
        else:
            s_ref[...] = jnp.zeros(s_ref.shape, F32)

    for hd in range(RET_HEADS):
        ksl = slice(hd * RET_DK, (hd + 1) * RET_DK)
        vsl = slice(hd * RET_DV, (hd + 1) * RET_DV)
        on, s_new = _ret_head(q_ref[:, ksl], k_ref[:, ksl], v_ref[:, vsl], s_ref[hd],
                              tril_ref[...], gn_ref[:, vsl], cdec[hd])
        s_ref[hd] = s_new
        y_ref[:, vsl] = (g_ref[:, vsl].astype(F32) * on).astype(BF16)


def _state_specs(n_states, j, b, s_all):
    spec = pl.BlockSpec((None, None, RET_HEADS, RET_DK, RET_DV), lambda b_, t_: (j, b_, 0, 0, 0))
    shape = jax.ShapeDtypeStruct((n_states, b, RET_HEADS, RET_DK, RET_DV), F32)
    return spec, shape


def _ret_core(q, k, v, g, s0_all, s_all, n_states, j, gn_g, lc, name):
    b, t, d = q.shape
    cdec = tuple(float(c) for c in _ret_gamma() ** float(lc))
    tril = jnp.asarray(np.tril(np.ones((lc, lc))), F32)
    has_state = s0_all is not None
    has_alias = s_all is not None
    row = lambda width: pl.BlockSpec((None, lc, width), lambda b_, t_: (b_, t_, 0))
    s_spec, s_shape = _state_specs(n_states, j, b, s_all)
    in_specs = [row(d), row(d), row(2 * d), row(2 * d)]
    args = [q, k, v, g]
    if has_state:
        in_specs.append(s_spec)
        args.append(s0_all)
    in_specs += [_resident((lc, lc), lambda b_, t_: (0, 0)),
                 pl.BlockSpec((1, 2 * d), lambda b_, t_: (0, 0))]
    args += [tril, gn_g]
    aliases = {}
    if has_alias:
        in_specs.append(pl.BlockSpec(memory_space=pl.ANY))
        args.append(s_all)
        aliases = {len(args) - 1: 1}
    return pl.pallas_call(
        functools.partial(_ret_core_kernel, cdec, has_state, has_alias),
        grid=(b, t // lc),
        in_specs=in_specs,
        out_specs=[row(2 * d), s_spec],
        out_shape=[jax.ShapeDtypeStruct((b, t, 2 * d), BF16), s_shape],
        input_output_aliases=aliases,
        compiler_params=_cparams(2),
        name=name,
    )(*args)


def _ret_fused_kernel(cdec, lc, has_alias, *refs):
    refs = list(refs)
    (x_ref, sh_ref, sc_ref, gm_ref, ng_ref, win_ref, wout_ref, cos_ref, sin_ref, qs_ref, ks_ref,
     tril_ref, gn_ref) = refs[:13]
    o_ref, s_ref, q_scr, k_scr, v_scr, y_scr = refs[13 + has_alias:]
    tt, d = x_ref.shape

    @pl.when(pl.program_id(1) == 0)
    def _():
        s_ref[...] = jnp.zeros(s_ref.shape, F32)

    hb = _norm_mod(x_ref[...][None], ng_ref[...], sc_ref[...], sh_ref[...])[0].astype(BF16)
    cos = cos_ref[...]
    sin = sin_ref[...]
    for idx, (scr, tab) in enumerate(((q_scr, qs_ref), (k_scr, ks_ref))):
        y = jnp.dot(hb, win_ref[:, idx * d:(idx + 1) * d], preferred_element_type=F32)
        for hd in range(RET_HEADS):
            lo = hd * RET_DK
            mid = lo + ROPE_HALF
            hi = lo + RET_DK
            x1 = y[:, lo:mid]
            x2 = y[:, mid:hi]
            scr[:, lo:mid] = ((x1 * cos - x2 * sin) * tab[:, lo:mid]).astype(BF16)
            scr[:, mid:hi] = ((x1 * sin + x2 * cos) * tab[:, mid:hi]).astype(BF16)
    v_scr[...] = jnp.dot(hb, win_ref[:, 2 * d:4 * d], preferred_element_type=F32).astype(BF16)
    for hd in range(RET_HEADS):
        ksl = slice(hd * RET_DK, (hd + 1) * RET_DK)
        vsl = slice(hd * RET_DV, (hd + 1) * RET_DV)
        g = jnp.dot(hb, win_ref[:, 4 * d + hd * RET_DV:4 * d + (hd + 1) * RET_DV],
                    preferred_element_type=F32)
        for c in range(tt // lc):
            rows = slice(c * lc, (c + 1) * lc)
            on, s_new = _ret_head(q_scr[rows, ksl], k_scr[rows, ksl], v_scr[rows, vsl], s_ref[hd],
                                  tril_ref[...], gn_ref[:, vsl], cdec[hd])
            s_ref[hd] = s_new
            y_scr[rows, vsl] = (_silu(g[rows]) * on).astype(BF16)
    out = jnp.dot(y_scr[...], wout_ref[...], preferred_element_type=F32)
    o_ref[...] = x_ref[...] + gm_ref[0] * out


def _ret_fused(x, mod, norm_g4, w_in, w_out, gn_g, cos, sin, s_all, n_states, lc, tt,
               layer, widx, row0):
    b, t, d = x.shape
    assert tt % lc == 0 and t % tt == 0
    cdec = tuple(float(c) for c in _ret_gamma() ** float(lc))
    tril = jnp.asarray(np.tril(np.ones((lc, lc))), F32)
    qs, ks = _ret_scale_tables(tt, lc)
    has_alias = s_all is not None
    xs = pl.BlockSpec((None, tt, d), lambda b_, t_: (b_, t_, 0))
    tab = pl.BlockSpec((tt, ROPE_HALF), lambda b_, t_: (t_, 0))
    const2 = lambda shape: _resident(shape, lambda b_, t_: (0, 0))
    s_spec, s_shape = _state_specs(n_states, widx, b, s_all)
    in_specs = [
        xs,
        _mod_spec(layer, 3, 1, row0), _mod_spec(layer, 4, 1, row0), _mod_spec(layer, 5, 1, row0),
        _ng_spec(layer, 1),
        _resident((None, d, 6 * d), lambda b_, t_: (widx, 0, 0)),
        _resident((None, 2 * d, d), lambda b_, t_: (widx, 0, 0)),
        tab, tab, const2((tt, d)), const2((tt, d)), const2((lc, lc)),
        pl.BlockSpec((1, 2 * d), lambda b_, t_: (0, 0)),
    ]
    args = [x, mod, mod, mod, norm_g4, w_in, w_out, cos, sin, qs, ks, tril, gn_g]
    aliases = {}
    if has_alias:
        in_specs.append(pl.BlockSpec(memory_space=pl.ANY))
        args.append(s_all)
        aliases = {len(args) - 1: 1}
    return pl.pallas_call(
        functools.partial(_ret_fused_kernel, cdec, lc, has_alias),
        grid=(b, t // tt),
        in_specs=in_specs,
        out_specs=[xs, s_spec],
        out_shape=[jax.ShapeDtypeStruct(x.shape, F32), s_shape],
        scratch_shapes=[pltpu.VMEM((tt, d), BF16), pltpu.VMEM((tt, d), BF16),
                        pltpu.VMEM((tt, 2 * d), BF16), pltpu.VMEM((tt, 2 * d), BF16)],
        input_output_aliases=aliases,
        compiler_params=_cparams(2),
        name=f"ret_l{layer}",
    )(*args)


def _pool_kernel(pos0, x_ref, sh_ref, sc_ref, gm_ref, ng_ref, buf_ref, w_ref, pb_ref, ps_ref,
                 o_ref, nb_ref, carry_ref):
    tt, d = x_ref.shape
    t_idx = pl.program_id(1)

    @pl.when(t_idx == 0)
    def _():
        carry_ref[...] = buf_ref[...]

    x = x_ref[...]
    h = _norm_mod(x[None], ng_ref[...], sc_ref[...], sh_ref[...])[0]
    ext = jnp.concatenate([carry_ref[...], h], axis=0)
    carry_ref[...] = h[tt - POOL_CARRY:, :]
    nb_ref[...] = h[tt - POOL_BUF:, :]

    pos = pos0 + t_idx * tt + lax.broadcasted_iota(jnp.int32, (tt, 1), 0)
    run = ext
    w = 1
    outs = []
    for gi, win in enumerate(POOL_WINDOWS):
        while w < win:
            n = run.shape[0]
            run = run[w:, :] + run[:n - w, :]
            w *= 2
        wsum = run[run.shape[0] - tt:, :POOL_GC]
        inv_cnt = 1.0 / jnp.minimum(pos + 1, win).astype(F32)
        lo = gi * POOL_GC
        pooled = wsum * inv_cnt - h[:, lo:lo + POOL_GC]
        outs.append(jnp.dot(pooled.astype(BF16), w_ref[gi], preferred_element_type=F32))
        run = run[:, POOL_GC:]
    y = (jnp.concatenate(outs, axis=1) + pb_ref[...]) * ps_ref[...]
    o_ref[...] = x + gm_ref[0] * y


def _pool(x, mod, norm_g4, buf16, w, pb, ps, layer, widx, tt, row0, pos0):
    b, t, d = x.shape
    xs = pl.BlockSpec((None, tt, d), lambda b_, t_: (b_, t_, 0))
    vec = pl.BlockSpec((None, 1, d), lambda b_, t_: (widx, 0, 0))
    return pl.pallas_call(
        functools.partial(_pool_kernel, pos0),
        grid=(b, t // tt),
        in_specs=[
            xs,
            _mod_spec(layer, 3, 1, row0),
            _mod_spec(layer, 4, 1, row0),
            _mod_spec(layer, 5, 1, row0),
            _ng_spec(layer, 1),
            pl.BlockSpec((None, POOL_CARRY, d), lambda b_, t_: (b_, 0, 0)),
            pl.BlockSpec((None, POOL_GROUPS, POOL_GC, POOL_GC), lambda b_, t_: (widx, 0, 0, 0)),
            vec, vec,
        ],
        out_specs=[xs, pl.BlockSpec((None, POOL_BUF, d), lambda b_, t_: (b_, 0, 0))],
        out_shape=[jax.ShapeDtypeStruct(x.shape, F32),
                   jax.ShapeDtypeStruct((b, POOL_BUF, d), F32)],
        scratch_shapes=[pltpu.VMEM((POOL_CARRY, d), F32)],
        compiler_params=_cparams(2),
        name=f"pool_l{layer}",
    )(x, mod, mod, mod, norm_g4, buf16, w, pb, ps)


def _head_norm(y, pavg, gain):
    y2 = (y * y).astype(BF16)
    ms = jnp.concatenate(
        [jnp.dot(y2[:, g * ATT_GW:(g + 1) * ATT_GW], pavg, preferred_element_type=F32)
         for g in range(ATT_NGROUPS)], axis=1)
    return y * lax.rsqrt(ms + EPS) * gain


def _att_proj_kernel(first_keep_tile, x_ref, sh_ref, sc_ref, ng_ref, w_ref, p_ref, qg_ref, kg_ref,
                     q_ref, kb_ref, vb_ref, kk_ref, vk_ref):
    bb, tt, d = x_ref.shape
    h = _norm_mod(x_ref[...], ng_ref[...], sc_ref[...], sh_ref[...])
    hb = h.reshape(bb * tt, d).astype(BF16)
    q = jnp.dot(hb, w_ref[:, 0:d], preferred_element_type=F32)
    q = _head_norm(q, p_ref[...], qg_ref[...]) * ATT_Q_SCALE
    q_ref[...] = q.reshape(bb, tt, d).astype(BF16)
    k = jnp.dot(hb, w_ref[:, d:2 * d], preferred_element_type=F32)
    k = _head_norm(k, p_ref[...], kg_ref[...]).reshape(bb, tt, d)
    kb_ref[...] = k.astype(BF16)
    v = jnp.dot(hb, w_ref[:, 2 * d:3 * d], preferred_element_type=F32).reshape(bb, tt, d)
    vb_ref[...] = v.astype(BF16)

    @pl.when(pl.program_id(1) >= first_keep_tile)
    def _():
        kk_ref[...] = k
        vk_ref[...] = v


def _att_proj(x, mod, norm_g4, w_qkv, pavg, qg, kg, keep, layer, widx, bb, tt, row0):
    b, t, d = x.shape
    assert keep % tt == 0 and t % tt == 0
    first_keep_tile = (t - keep) // tt
    vec = pl.BlockSpec((None, 1, d), lambda b_, t_: (widx, 0, 0))
    keep_spec = pl.BlockSpec((bb, tt, d),
                             lambda b_, t_: (b_, jnp.maximum(t_ - first_keep_tile, 0), 0))
    b16o = jax.ShapeDtypeStruct((b, t, d), BF16)
    f32k = jax.ShapeDtypeStruct((b, keep, d), F32)
    return pl.pallas_call(
        functools.partial(_att_proj_kernel, first_keep_tile),
        grid=(b // bb, t // tt),
        in_specs=[
            _x_spec(bb, tt),
            _mod_spec(layer, 3, bb, row0),
            _mod_spec(layer, 4, bb, row0),
            _ng_spec(layer, 1),
            _resident((None, d, 3 * d), lambda b_, t_: (widx, 0, 0)),
            _resident((ATT_GW, ATT_GW), lambda b_, t_: (0, 0)),
            vec, vec,
        ],
        out_specs=[_x_spec(bb, tt)] * 3 + [keep_spec] * 2,
        out_shape=[b16o, b16o, b16o, f32k, f32k],
        compiler_params=_cparams(2),
        name=f"att_proj_l{layer}",
    )(x, mod, mod, norm_g4, w_qkv, pavg, qg, kg)


def _head_block_diag():
    rb = lax.broadcasted_iota(jnp.int32, (ATT_GW, ATT_GW), 0) // ATT_HD
    cb = lax.broadcasted_iota(jnp.int32, (ATT_GW, ATT_GW), 1) // ATT_HD
    lane_head = lax.broadcasted_iota(jnp.int32, (CHUNK, ATT_GW), 1) // ATT_HD
    return rb == cb, lane_head


def _att_chunk_group(qc, kb, vb, bias, masks, first_kpos):
    diag, lane_head = masks
    qbd = jnp.where(diag, jnp.concatenate([qc] * ATT_GROUP, axis=0), jnp.zeros((), BF16))
    s = lax.dot_general(qbd, kb, (((1,), (1,)), ((), ())), preferred_element_type=F32) + bias
    if first_kpos is not None:
        kpos = first_kpos + lax.broadcasted_iota(jnp.int32, (1, BAND), 1)
        s = jnp.where(kpos >= 0, s, NEG_INF)
    m = jnp.max(s, axis=-1, keepdims=True)
    p = jnp.exp2(s - m)
    inv = 1.0 / jnp.sum(p, axis=-1, keepdims=True)
    o = jnp.dot(p.astype(BF16), vb, preferred_element_type=F32)
    last = ATT_GROUP - 1
    og = o[last * CHUNK:] * inv[last * CHUNK:]
    for hh in range(last - 1, -1, -1):
        rows = slice(hh * CHUNK, (hh + 1) * CHUNK)
        og = jnp.where(lane_head == hh, o[rows] * inv[rows], og)
    return og


def _att_core_kernel(first_pos, has_past, *refs):
    if has_past:
        q_ref, k_ref, v_ref, pk_ref, pv_ref, bias_ref, o_ref, kbuf, vbuf = refs
    else:
        q_ref, k_ref, v_ref, bias_ref, o_ref, kbuf, vbuf = refs
    t = q_ref.shape[0]
    if has_past:
        kbuf[0:BAND_ROWS, :] = pk_ref[...].astype(BF16)
        vbuf[0:BAND_ROWS, :] = pv_ref[...].astype(BF16)
    else:
        kbuf[0:BAND_ROWS, :] = jnp.zeros((BAND_ROWS, ATT_GW), BF16)
        vbuf[0:BAND_ROWS, :] = jnp.zeros((BAND_ROWS, ATT_GW), BF16)
    kbuf[BAND_ROWS:, :] = k_ref[...]
    vbuf[BAND_ROWS:, :] = v_ref[...]
    masks = _head_block_diag()

    def make_chunk(masked):
        def chunk(n, carry):
            r0 = pl.multiple_of(n * CHUNK, CHUNK)
            og = _att_chunk_group(q_ref[pl.ds(r0, CHUNK), :], kbuf[pl.ds(r0, BAND), :],
                                  vbuf[pl.ds(r0, BAND), :], bias_ref[...], masks,
                                  first_pos + r0 if masked else None)
            o_ref[pl.ds(r0, CHUNK), :] = og.astype(BF16)
            return carry
        return chunk

    n_chunks = t // CHUNK
    n_masked = min(n_chunks, max(0, -(first_pos // CHUNK)))
    for lo, hi, masked in ((0, n_masked, True), (n_masked, n_chunks, False)):
        if hi > lo:
            trips = hi - lo
            unroll = next(u for u in (ATT_UNROLL, 2, 1) if trips % u == 0)
            lax.fori_loop(lo, hi, make_chunk(masked), 0, unroll=unroll)


def _att_core(q, k, v, past_k, past_v, bias, first_pos, name):
    b, t, d = q.shape
    has_past = past_k is not None
    grp = lambda rows: pl.BlockSpec((None, rows, ATT_GW), lambda b_, g_: (b_, 0, g_))
    in_specs = [grp(t), grp(t), grp(t)]
    args = [q, k, v]
    if has_past:
        in_specs += [grp(BAND_ROWS), grp(BAND_ROWS)]
        args += [past_k, past_v]
    in_specs.append(pl.BlockSpec((None, ATT_GW, BAND), lambda b_, g_: (g_, 0, 0)))
    args.append(bias)
    return pl.pallas_call(
        functools.partial(_att_core_kernel, first_pos, has_past),
        grid=(b, ATT_NGROUPS),
        in_specs=in_specs,
        out_specs=grp(t),
        out_shape=jax.ShapeDtypeStruct((b, t, d), BF16),
        scratch_shapes=[pltpu.VMEM((BAND_ROWS + t, ATT_GW), BF16)] * 2,
        compiler_params=_cparams(2),
        name=name,
    )(*args)


def _att_fused_kernel(first_keep_tile, x_ref, sh_ref, sc_ref, gm_ref, ng_ref, wqkv_ref, p_ref,
                      qg_ref, kg_ref, bias_ref, wo_ref, o_ref, kk_ref, vk_ref,
                      q_scr, kbuf, vbuf, a_scr):
    tt, d = x_ref.shape
    t_idx = pl.program_id(1)

    @pl.when(t_idx == 0)
    def _():
        kbuf[0:BAND_ROWS, :] = jnp.zeros((BAND_ROWS, d), BF16)
        vbuf[0:BAND_ROWS, :] = jnp.zeros((BAND_ROWS, d), BF16)

    @pl.when(t_idx > 0)
    def _():
        kbuf[0:BAND_ROWS, :] = kbuf[tt:tt + BAND_ROWS, :]
        vbuf[0:BAND_ROWS, :] = vbuf[tt:tt + BAND_ROWS, :]

    hb = _norm_mod(x_ref[...][None], ng_ref[...], sc_ref[...], sh_ref[...])[0].astype(BF16)
    q = jnp.dot(hb, wqkv_ref[:, 0:d], preferred_element_type=F32)
    q_scr[...] = (_head_norm(q, p_ref[...], qg_ref[...]) * ATT_Q_SCALE).astype(BF16)
    k = jnp.dot(hb, wqkv_ref[:, d:2 * d], preferred_element_type=F32)
    k = _head_norm(k, p_ref[...], kg_ref[...])
    kbuf[BAND_ROWS:, :] = k.astype(BF16)
    v = jnp.dot(hb, wqkv_ref[:, 2 * d:3 * d], preferred_element_type=F32)
    vbuf[BAND_ROWS:, :] = v.astype(BF16)

    @pl.when(t_idx >= first_keep_tile)
    def _():
        kk_ref[...] = k
        vk_ref[...] = v

    masks = _head_block_diag()

    def make_chunk(masked):
        def chunk(n, carry):
            r0 = pl.multiple_of(n * CHUNK, CHUNK)
            for g in range(ATT_NGROUPS):
                gsl = slice(g * ATT_GW, (g + 1) * ATT_GW)
                og = _att_chunk_group(q_scr[pl.ds(r0, CHUNK), gsl], kbuf[pl.ds(r0, BAND), gsl],
                                      vbuf[pl.ds(r0, BAND), gsl], bias_ref[g], masks,
                                      r0 - BAND_ROWS if masked else None)
                a_scr[pl.ds(r0, CHUNK), gsl] = og.astype(BF16)
            return carry
        return chunk

    @pl.when(t_idx == 0)
    def _():
        lax.fori_loop(0, tt // CHUNK, make_chunk(True), 0, unroll=2)

    @pl.when(t_idx > 0)
    def _():
        lax.fori_loop(0, tt // CHUNK, make_chunk(False), 0, unroll=2)

    out = jnp.dot(a_scr[...], wo_ref[...], preferred_element_type=F32)
    o_ref[...] = x_ref[...] + gm_ref[0] * out


def _att_fused(x, mod, norm_g4, w_qkv, w_o, pavg, qg, kg, bias, keep, layer, widx, tt, row0):
    b, t, d = x.shape
    assert tt >= BAND_ROWS and tt % CHUNK == 0 and (tt // CHUNK) % 2 == 0
    assert keep % tt == 0 and t % tt == 0
    first_keep_tile = (t - keep) // tt
    xs = pl.BlockSpec((None, tt, d), lambda b_, t_: (b_, t_, 0))
    vec = pl.BlockSpec((None, 1, d), lambda b_, t_: (widx, 0, 0))
    keep_spec = pl.BlockSpec((None, tt, d),
                             lambda b_, t_: (b_, jnp.maximum(t_ - first_keep_tile, 0), 0))
    f32k = jax.ShapeDtypeStruct((b, keep, d), F32)
    return pl.pallas_call(
        functools.partial(_att_fused_kernel, first_keep_tile),
        grid=(b, t // tt),
        in_specs=[
            xs,
            _mod_spec(layer, 3, 1, row0), _mod_spec(layer, 4, 1, row0), _mod_spec(layer, 5, 1, row0),
            _ng_spec(layer, 1),
            _resident((None, d, 3 * d), lambda b_, t_: (widx, 0, 0)),
            _resident((ATT_GW, ATT_GW), lambda b_, t_: (0, 0)),
            vec, vec,
            _resident((ATT_NGROUPS, ATT_GW, BAND), lambda b_, t_: (0, 0, 0)),
            _resident((None, d, d), lambda b_, t_: (widx, 0, 0)),
        ],
        out_specs=[xs, keep_spec, keep_spec],
        out_shape=[jax.ShapeDtypeStruct(x.shape, F32), f32k, f32k],
        scratch_shapes=[pltpu.VMEM((tt, d), BF16), pltpu.VMEM((BAND_ROWS + tt, d), BF16),
                        pltpu.VMEM((BAND_ROWS + tt, d), BF16), pltpu.VMEM((tt, d), BF16)],
        compiler_params=_cparams(2),
        name=f"att_l{layer}",
    )(x, mod, mod, mod, norm_g4, w_qkv, pavg, qg, kg, bias, w_o)


def _att_bias_kernel(rb_ref, o_ref):
    rb = rb_ref[...]
    hi = rb.astype(BF16)
    r1 = rb - hi.astype(F32)
    mid = r1.astype(BF16)
    lo = (r1 - mid.astype(F32)).astype(BF16)
    r = lax.broadcasted_iota(jnp.int32, (REL_SIZE, REL_BASE_W), 0)
    u = lax.broadcasted_iota(jnp.int32, (REL_SIZE, REL_BASE_W), 1)
    rel = jnp.clip(BAND_ROWS + CHUNK - 1 - u, REL_MIN, REL_MAX) - REL_MIN
    sel = jnp.where(r == rel, 1.0, 0.0).astype(BF16)
    base = (jnp.dot(hi, sel, preferred_element_type=F32)
            + jnp.dot(mid, sel, preferred_element_type=F32)
            + jnp.dot(lo, sel, preferred_element_type=F32)) * LOG2E
    for i in range(CHUNK):
        o_ref[i] = base[:, CHUNK - 1 - i:CHUNK - 1 - i + BAND]


def _att_bias_table(rel_bias):
    tab = pl.pallas_call(
        _att_bias_kernel,
        out_shape=jax.ShapeDtypeStruct((CHUNK, ATT_HEADS, BAND), F32),
        name="att_bias",
    )(rel_bias)
    return tab.transpose(1, 0, 2).reshape(ATT_NGROUPS, ATT_GW, BAND)


def _rope_tables(pos0, t):
    inv = ROPE_BASE ** (-np.arange(ROPE_HALF, dtype=np.float64) / ROPE_HALF)
    ang = (pos0 + np.arange(t, dtype=np.float64))[:, None] * inv[None, :]
    return jnp.asarray(np.cos(ang), F32), jnp.asarray(np.sin(ang), F32)


def _trunk(x, mod, row0, pos0, bb, tt, ret_lc, ret_states, pool_bufs, att_k_cache, att_v_cache, p):
    b, t, d = x.shape
    new_pool, new_k, new_v = [], [], []
    ret_all = None
    n_ret = len(range(0, DEPTH, N_MIXERS))
    cos, sin = _rope_tables(pos0, t)
    tt_ffn = FFN_ROWS if (bb == 1 and t % FFN_ROWS == 0) else tt
    bb_ffn = bb
    if tt == t and FFN_ROWS % t == 0:
        wide = min(b, FFN_ROWS // t)
        if b % wide == 0 and row0 % wide == 0:
            bb_ffn = wide
    for i in range(DEPTH):
        x = _ffn(x, mod, p['norm_g'], p['ffn1_w_in'], p['ffn1_w_out'], i, 1, bb_ffn, tt_ffn, row0)
        j = i // N_MIXERS
        kind = i % N_MIXERS
        if kind == 0:
            gn = p['ret_gn_g'][j].reshape(1, 2 * d)
            if ret_states is None:
                x, ret_all = _ret_fused(x, mod, p['norm_g'], p['ret_w_in'], p['ret_w_out'], gn,
                                        cos, sin, ret_all, n_ret, ret_lc, max(tt, ret_lc),
                                        i, j, row0)
            else:
                q, k, v, g = _ret_proj(x, mod, p['norm_g'], p['ret_w_in'], cos, sin, ret_lc,
                                       i, j, bb, tt, row0)
                y, ret_all = _ret_core(q, k, v, g, ret_states, ret_all, n_ret, j, gn, ret_lc,
                                       f"ret_core_l{i}")
                x = _matres(y, x, mod, p['ret_w_out'], i, j, bb, tt, row0, f"ret_out_l{i}")
        elif kind == 1:
            if pool_bufs is None:
                buf16 = jnp.zeros((b, POOL_CARRY, d), F32)
            else:
                buf16 = jnp.pad(pool_bufs[j], ((0, 0), (POOL_CARRY - POOL_BUF, 0), (0, 0)))
            x, nb = _pool(x, mod, p['norm_g'], buf16, p['pool_w'], p['pool_b'], p['pool_scale'],
                          i, j, min(tt, t), row0, pos0)
            new_pool.append(nb)
        else:
            keep = min(BAND_ROWS, t)
            if att_k_cache is None and pos0 == 0 and tt >= BAND_ROWS:
                x, k_keep, v_keep = _att_fused(
                    x, mod, p['norm_g'], p['att_w_qkv'], p['att_w_o'], p['att_pavg'],
                    p['att_q_g'], p['att_k_g'], p['att_bias'][j], keep, i, j, tt, row0)
            else:
                q, kb, vb, k_keep, v_keep = _att_proj(
                    x, mod, p['norm_g'], p['att_w_qkv'], p['att_pavg'], p['att_q_g'],
                    p['att_k_g'], keep, i, j, bb, tt, row0)
                if att_k_cache is None:
                    past_k = past_v = None
                else:
                    past_k = att_k_cache[j].reshape(b, BAND_ROWS, d)
                    past_v = att_v_cache[j].reshape(b, BAND_ROWS, d)
                o = _att_core(q, kb, vb, past_k, past_v, p['att_bias'][j], pos0 - BAND_ROWS,
                              f"att_core_l{i}")
                x = _matres(o, x, mod, p['att_w_o'], i, j, bb, tt, row0, f"att_out_l{i}")
            new_k.append(k_keep.reshape(b, keep, ATT_HEADS, ATT_HD))
            new_v.append(v_keep.reshape(b, keep, ATT_HEADS, ATT_HD))
        x = _ffn(x, mod, p['norm_g'], p['ffn2_w_in'], p['ffn2_w_out'], i, 2, bb_ffn, tt_ffn, row0)
    return x, ret_all, jnp.stack(new_pool), jnp.stack(new_k), jnp.stack(new_v)


def kernel(x_prompt, x_sample, c_prompt, c_sample, state_ret, state_pool, cache_att_k, cache_att_v, norm_g, ada_w, ada_b, ffn1_w_in, ffn1_w_out, ffn2_w_in, ffn2_w_out, ret_w_in, ret_w_out, ret_gn_g, pool_w, pool_b, pool_scale, att_w_qkv, att_w_o, att_q_g, att_k_g, att_rel_bias):
    assert cache_att_k.shape[2] == BAND_ROWS, "key cache must hold exactly the left band"
    bp, tp, d = x_prompt.shape
    bs, ts, _ = x_sample.shape
    n_att = att_w_qkv.shape[0]
    head_id = np.arange(ATT_GW) // ATT_HD
    pavg = jnp.asarray((head_id[:, None] == head_id[None, :]) / ATT_HD, BF16)
    p = {
        'norm_g': norm_g.reshape(DEPTH, 3, 1, d),
        'ffn1_w_in': ffn1_w_in.astype(BF16), 'ffn1_w_out': ffn1_w_out.astype(BF16),
        'ffn2_w_in': ffn2_w_in.astype(BF16), 'ffn2_w_out': ffn2_w_out.astype(BF16),
        'ret_w_in': ret_w_in.astype(BF16), 'ret_w_out': ret_w_out.astype(BF16),
        'ret_gn_g': ret_gn_g,
        'pool_w': pool_w.astype(BF16),
        'pool_b': pool_b.reshape(-1, 1, d), 'pool_scale': pool_scale.reshape(-1, 1, d),
        'att_w_qkv': att_w_qkv.astype(BF16), 'att_w_o': att_w_o.astype(BF16),
        'att_q_g': jnp.tile(att_q_g, (1, ATT_HEADS)).reshape(n_att, 1, d),
        'att_k_g': jnp.tile(att_k_g, (1, ATT_HEADS)).reshape(n_att, 1, d),
        'att_pavg': pavg,
        'att_bias': [_att_bias_table(att_rel_bias[j]) for j in range(n_att)],
    }
    mod = _ada(jnp.concatenate([c_prompt, c_sample], axis=0), ada_w, ada_b)

    tt_p = min(PROMPT_ROWS, tp)
    y_prompt, ret_p, pool_p, k_p, v_p = _trunk(
        x_prompt, mod, 0, 0, 1, tt_p, min(RET_CHUNK_PROMPT, tp), None, None, None, None, p)
    bb_s = max(1, min(bs, PROMPT_ROWS // ts))
    y_sample, ret_s, pool_s, k_s, v_s = _trunk(
        x_sample, mod, bp, PAST_LEN, bb_s, ts, ts, state_ret, state_pool,
        cache_att_k, cache_att_v, p)
    return (y_prompt, y_sample, ret_p, ret_s, pool_p, pool_s, k_p, v_p, k_s, v_s)
```

```python
import functools

import numpy as np
import jax
import jax.numpy as jnp
from jax import lax
from jax.experimental import pallas as pl
from jax.experimental.pallas import tpu as pltpu

F32 = jnp.float32
BF16 = jnp.bfloat16

LANES = 128
D_MODEL = 1024
DEPTH = 4
D_FF = 2816
EPS = 1e-6
CHUNK = 64
PAST_LEN = 2048
N_MIXERS = 3
RET_HEADS = 4
RET_DK = D_MODEL // RET_HEADS
RET_DV = 2 * D_MODEL // RET_HEADS
ROPE_BASE = 10000.0
ROPE_HALF = RET_DK // 2
POOL_WINDOWS = (2, 4, 8, 16)
POOL_GROUPS = 4
POOL_GC = D_MODEL // POOL_GROUPS
POOL_BUF = 15
POOL_CARRY = 16
ATT_HEADS = 16
ATT_HD = D_MODEL // ATT_HEADS
ATT_GROUP = 4
ATT_GW = ATT_GROUP * ATT_HD
ATT_NGROUPS = ATT_HEADS // ATT_GROUP
LEFT_CHUNKS = 8
BAND_ROWS = LEFT_CHUNKS * CHUNK
BAND = BAND_ROWS + CHUNK
REL_MIN = -(CHUNK - 1)
REL_MAX = 256
REL_SIZE = REL_MAX - REL_MIN + 1
REL_BASE_W = 640
NEG_INF = -1e30
LOG2E = 1.4426950408889634
ATT_Q_SCALE = ATT_HD ** -0.5 * LOG2E
N_MOD = 9

VMEM_LIMIT_BYTES = 56 * 1024 * 1024
PROMPT_ROWS = 512
FFN_ROWS = 1024
RET_CHUNK_PROMPT = 256
ATT_UNROLL = 8


def _cparams(n_grid):
    return pltpu.CompilerParams(
        dimension_semantics=("arbitrary",) * n_grid,
        vmem_limit_bytes=VMEM_LIMIT_BYTES,
    )


def _resident(block_shape, index_map):
    return pl.BlockSpec(block_shape, index_map, pipeline_mode=pl.Buffered(1))


def _norm_mod(x, ng, sc, sh):
    ms = jnp.mean(x * x, axis=-1, keepdims=True)
    y = x * lax.rsqrt(ms + EPS) * ng
    return y * (1.0 + sc) + sh


def _silu(x):
    return x * (1.0 / (1.0 + jnp.exp(-x)))


def _ada_kernel(c_ref, w_ref, b_ref, o_ref):
    c = _silu(c_ref[...]).astype(BF16)
    w = w_ref[...].astype(BF16)
    o_ref[...] = jnp.dot(c, w, preferred_element_type=F32) + b_ref[...]


def _ada(c_all, ada_w, ada_b):
    n = c_all.shape[0]
    b4 = ada_b.reshape(DEPTH, N_MOD, 1, D_MODEL)
    out = pl.pallas_call(
        _ada_kernel,
        grid=(DEPTH, N_MOD),
        in_specs=[
            pl.BlockSpec((n, D_MODEL), lambda i, j: (0, 0)),
            pl.BlockSpec((None, D_MODEL, D_MODEL), lambda i, j: (i, 0, j)),
            pl.BlockSpec((None, None, 1, D_MODEL), lambda i, j: (i, j, 0, 0)),
        ],
        out_specs=pl.BlockSpec((None, None, n, D_MODEL), lambda i, j: (i, j, 0, 0)),
        out_shape=jax.ShapeDtypeStruct((DEPTH, N_MOD, n, D_MODEL), F32),
        compiler_params=_cparams(2),
        name="ada_mod",
    )(c_all, ada_w, b4)
    return out.reshape(DEPTH, N_MOD, n, 1, D_MODEL)


def _mod_spec(layer, j, bb, row0):
    blk0 = row0 // bb
    return pl.BlockSpec((None, None, bb, 1, D_MODEL),
                        lambda b, t: (layer, j, blk0 + b, 0, 0))


def _ng_spec(layer, j):
    return pl.BlockSpec((None, None, 1, D_MODEL), lambda b, t: (layer, j, 0, 0))


def _x_spec(bb, tt, width=D_MODEL):
    return pl.BlockSpec((bb, tt, width), lambda b, t: (b, t, 0))


def _interleave_gate_up(w_in):
    nl, d, _ = w_in.shape
    w = w_in.reshape(nl, d, 2, D_FF // LANES, LANES).transpose(0, 1, 3, 2, 4)
    return w.reshape(nl, d, 2 * D_FF).astype(BF16)


def _ffn_kernel(x_ref, sh_ref, sc_ref, g_ref, ng_ref, win_ref, wout_ref, o_ref):
    bb, tt, d = x_ref.shape
    x = x_ref[...]
    h = _norm_mod(x, ng_ref[...], sc_ref[...], sh_ref[...])
    hb = h.reshape(bb * tt, d).astype(BF16)
    gu = jnp.dot(hb, win_ref[...], preferred_element_type=F32)
    act = jnp.concatenate(
        [_silu(gu[:, 2 * j * LANES:(2 * j + 1) * LANES]) * gu[:, (2 * j + 1) * LANES:(2 * j + 2) * LANES]
         for j in range(D_FF // LANES)], axis=1).astype(BF16)
    y = jnp.dot(act, wout_ref[...], preferred_element_type=F32)
    o_ref[...] = x + (0.5 * g_ref[...]) * y.reshape(bb, tt, d)


def _ffn(x, mod, norm_g4, w_in, w_out, layer, which, bb, tt, row0):
    b, t, d = x.shape
    j0 = 0 if which == 1 else 6
    nj = 0 if which == 1 else 2
    return pl.pallas_call(
        _ffn_kernel,
        grid=(b // bb, t // tt),
        in_specs=[
            _x_spec(bb, tt),
            _mod_spec(layer, j0, bb, row0),
            _mod_spec(layer, j0 + 1, bb, row0),
            _mod_spec(layer, j0 + 2, bb, row0),
            _ng_spec(layer, nj),
            _resident((None, D_MODEL, 2 * D_FF), lambda b_, t_: (layer, 0, 0)),
            _resident((None, D_FF, D_MODEL), lambda b_, t_: (layer, 0, 0)),
        ],
        out_specs=_x_spec(bb, tt),
        out_shape=jax.ShapeDtypeStruct(x.shape, F32),
        compiler_params=_cparams(2),
        name=f"ffn{which}_l{layer}",
    )(x, mod, mod, mod, norm_g4, w_in, w_out)


def _matres_kernel(a_ref, x_ref, g_ref, w_ref, o_ref):
    bb, tt, k = a_ref.shape
    y = jnp.dot(a_ref[...].reshape(bb * tt, k), w_ref[...], preferred_element_type=F32)
    o_ref[...] = x_ref[...] + g_ref[...] * y.reshape(bb, tt, D_MODEL)


def _matres(a, x, mod, w, layer, widx, bb, tt, row0, name):
    b, t, k = a.shape
    return pl.pallas_call(
        _matres_kernel,
        grid=(b // bb, t // tt),
        in_specs=[
            _x_spec(bb, tt, k),
            _x_spec(bb, tt),
            _mod_spec(layer, 5, bb, row0),
            _resident((None, k, D_MODEL), lambda b_, t_: (widx, 0, 0)),
        ],
        out_specs=_x_spec(bb, tt),
        out_shape=jax.ShapeDtypeStruct(x.shape, F32),
        compiler_params=_cparams(2),
        name=name,
    )(a, x, mod, w)


def _ret_proj_kernel(x_ref, sh_ref, sc_ref, ng_ref, w_ref, cos_ref, sin_ref, qs_ref, ks_ref,
                     q_ref, k_ref, v_ref, g_ref):
    bb, tt, d = x_ref.shape
    h = _norm_mod(x_ref[...], ng_ref[...], sc_ref[...], sh_ref[...])
    hb = h.reshape(bb * tt, d).astype(BF16)
    cos = cos_ref[...][None]
    sin = sin_ref[...][None]
    for idx, (o_ref, s_ref) in enumerate(((q_ref, qs_ref), (k_ref, ks_ref))):
        y = jnp.dot(hb, w_ref[:, idx * d:(idx + 1) * d], preferred_element_type=F32)
        y = y.reshape(bb, tt, d)
        for hd in range(RET_HEADS):
            lo = hd * RET_DK
            mid = lo + ROPE_HALF
            x1 = y[:, :, lo:mid]
            x2 = y[:, :, mid:lo + RET_DK]
            o_ref[:, :, lo:mid] = ((x1 * cos - x2 * sin) * s_ref[:, lo:mid][None]).astype(BF16)
            o_ref[:, :, mid:lo + RET_DK] = (
                (x1 * sin + x2 * cos) * s_ref[:, mid:lo + RET_DK][None]).astype(BF16)
    v = jnp.dot(hb, w_ref[:, 2 * d:4 * d], preferred_element_type=F32)
    v_ref[...] = v.reshape(bb, tt, 2 * d).astype(BF16)
    g = jnp.dot(hb, w_ref[:, 4 * d:6 * d], preferred_element_type=F32)
    g_ref[...] = _silu(g).reshape(bb, tt, 2 * d).astype(BF16)


def _ret_gamma():
    return 1.0 - 2.0 ** (-5.0 - np.arange(RET_HEADS, dtype=np.float64))


def _ret_scale_tables(tt, lc):
    gamma = _ret_gamma()
    i = (np.arange(tt) % lc).astype(np.float64)[:, None]
    g_lane = np.repeat(gamma, RET_DK)[None, :]
    qs = g_lane ** (i + 1.0)
    ks = (RET_DK ** -0.5) * g_lane ** (-(i + 1.0))
    return jnp.asarray(qs, F32), jnp.asarray(ks, F32)


def _ret_proj(x, mod, norm_g4, w_in, cos, sin, lc, layer, widx, bb, tt, row0):
    b, t, d = x.shape
    assert tt % lc == 0
    qs, ks = _ret_scale_tables(tt, lc)
    tab = pl.BlockSpec((tt, ROPE_HALF), lambda b_, t_: (t_, 0))
    scale = _resident((tt, d), lambda b_, t_: (0, 0))
    return pl.pallas_call(
        _ret_proj_kernel,
        grid=(b // bb, t // tt),
        in_specs=[
            _x_spec(bb, tt),
            _mod_spec(layer, 3, bb, row0),
            _mod_spec(layer, 4, bb, row0),
            _ng_spec(layer, 1),
            _resident((None, D_MODEL, 6 * D_MODEL), lambda b_, t_: (widx, 0, 0)),
            tab, tab, scale, scale,
        ],
        out_specs=[_x_spec(bb, tt), _x_spec(bb, tt),
                   _x_spec(bb, tt, 2 * d), _x_spec(bb, tt, 2 * d)],
        out_shape=[jax.ShapeDtypeStruct((b, t, d), BF16), jax.ShapeDtypeStruct((b, t, d), BF16),
                   jax.ShapeDtypeStruct((b, t, 2 * d), BF16), jax.ShapeDtypeStruct((b, t, 2 * d), BF16)],
        compiler_params=_cparams(2),
        name=f"ret_proj_l{layer}",
    )(x, mod, mod, norm_g4, w_in, cos, sin, qs, ks)


def _ret_head(qh, kh, vh, s, tril, gn, cdec_h):
    inner = lax.dot_general(qh, kh, (((1,), (1,)), ((), ())), preferred_element_type=F32) * tril
    o = (jnp.dot(inner.astype(BF16), vh, preferred_element_type=F32)
         + jnp.dot(qh, s.astype(BF16), preferred_element_type=F32))
    s_new = (s + lax.dot_general(kh, vh, (((0,), (0,)), ((), ())),
                                 preferred_element_type=F32)) * cdec_h
    mu = jnp.mean(o, axis=-1, keepdims=True)
    oc = o - mu
    var = jnp.mean(oc * oc, axis=-1, keepdims=True)
    return oc * lax.rsqrt(var + EPS) * gn, s_new


def _init_state_stack(s_ref, slot, prev_ref, s0_ref):
    if slot:
        s_ref[0:slot] = prev_ref[...]
    if s0_ref is None:
        s_ref[slot] = jnp.zeros(s_ref.shape[1:], F32)
    else:
        s_ref[slot] = s0_ref[...]


def _ret_core_kernel(cdec, has_state, slot, *refs):
    refs = list(refs)
    q_ref, k_ref, v_ref, g_ref = refs[:4]
    s0_ref = refs[4] if has_state else None
    tril_ref, gn_ref = refs[4 + has_state:6 + has_state]
    prev_ref = refs[6 + has_state] if slot else None
    y_ref, s_ref = refs[6 + has_state + bool(slot):]

    @pl.when(pl.program_id(1) == 0)
    def _():
        _init_state_stack(s_ref, slot, prev_ref, s0_ref)

    for hd in range(RET_HEADS):
        ksl = slice(hd * RET_DK, (hd + 1) * RET_DK)
        vsl = slice(hd * RET_DV, (hd + 1) * RET_DV)
        on, s_new = _ret_head(q_ref[:, ksl], k_ref[:, ksl], v_ref[:, vsl], s_ref[slot, hd],
                              tril_ref[...], gn_ref[:, vsl], cdec[hd])
        s_ref[slot, hd] = s_new
        y_ref[:, vsl] = (g_ref[:, vsl].astype(F32) * on).astype(BF16)


def _state_stack_specs(slot, b):
    blk = lambda n: pl.BlockSpec((n, None, RET_HEADS, RET_DK, RET_DV),
                                 lambda b_, t_: (0, b_, 0, 0, 0))
    shape = jax.ShapeDtypeStruct((slot + 1, b, RET_HEADS, RET_DK, RET_DV), F32)
    return blk(slot + 1), shape, (blk(slot) if slot else None)


def _ret_core(q, k, v, g, s0_all, s_prev, slot, gn_g, lc, name):
    b, t, d = q.shape
    cdec = tuple(float(c) for c in _ret_gamma() ** float(lc))
    tril = jnp.asarray(np.tril(np.ones((lc, lc))), F32)
    has_state = s0_all is not None
    row = lambda width: pl.BlockSpec((None, lc, width), lambda b_, t_: (b_, t_, 0))
    s_spec, s_shape, prev_spec = _state_stack_specs(slot, b)
    in_specs = [row(d), row(d), row(2 * d), row(2 * d)]
    args = [q, k, v, g]
    if has_state:
        in_specs.append(pl.BlockSpec((None, None, RET_HEADS, RET_DK, RET_DV),
                                     lambda b_, t_: (slot, b_, 0, 0, 0)))
        args.append(s0_all)
    in_specs += [_resident((lc, lc), lambda b_, t_: (0, 0)),
                 pl.BlockSpec((1, 2 * d), lambda b_, t_: (0, 0))]
    args += [tril, gn_g]
    if slot:
        in_specs.append(prev_spec)
        args.append(s_prev)
    return pl.pallas_call(
        functools.partial(_ret_core_kernel, cdec, has_state, slot),
        grid=(b, t // lc),
        in_specs=in_specs,
        out_specs=[row(2 * d), s_spec],
        out_shape=[jax.ShapeDtypeStruct((b, t, 2 * d), BF16), s_shape],
        compiler_params=_cparams(2),
        name=name,
    )(*args)


def _ret_fused_kernel(cdec, lc, slot, *refs):
    refs = list(refs)
    (x_ref, sh_ref, sc_ref, gm_ref, ng_ref, win_ref, wout_ref, cos_ref, sin_ref, qs_ref, ks_ref,
     tril_ref, gn_ref) = refs[:13]
    prev_ref = refs[13] if slot else None
    o_ref, s_ref, q_scr, k_scr, v_scr, y_scr = refs[13 + bool(slot):]
    tt, d = x_ref.shape

    @pl.when(pl.program_id(1) == 0)
    def _():
        _init_state_stack(s_ref, slot, prev_ref, None)

    hb = _norm_mod(x_ref[...][None], ng_ref[...], sc_ref[...], sh_ref[...])[0].astype(BF16)
    cos = cos_ref[...]
    sin = sin_ref[...]
    for idx, (scr, tab) in enumerate(((q_scr, qs_ref), (k_scr, ks_ref))):
        y = jnp.dot(hb, win_ref[:, idx * d:(idx + 1) * d], preferred_element_type=F32)
        for hd in range(RET_HEADS):
            lo = hd * RET_DK
            mid = lo + ROPE_HALF
            hi = lo + RET_DK
            x1 = y[:, lo:mid]
            x2 = y[:, mid:hi]
            scr[:, lo:mid] = ((x1 * cos - x2 * sin) * tab[:, lo:mid]).astype(BF16)
            scr[:, mid:hi] = ((x1 * sin + x2 * cos) * tab[:, mid:hi]).astype(BF16)
    v_scr[...] = jnp.dot(hb, win_ref[:, 2 * d:4 * d], preferred_element_type=F32).astype(BF16)
    for hd in range(RET_HEADS):
        ksl = slice(hd * RET_DK, (hd + 1) * RET_DK)
        vsl = slice(hd * RET_DV, (hd + 1) * RET_DV)
        g = jnp.dot(hb, win_ref[:, 4 * d + hd * RET_DV:4 * d + (hd + 1) * RET_DV],
                    preferred_element_type=F32)
        for c in range(tt // lc):
            rows = slice(c * lc, (c + 1) * lc)
            on, s_new = _ret_head(q_scr[rows, ksl], k_scr[rows, ksl], v_scr[rows, vsl],
                                  s_ref[slot, hd], tril_ref[...], gn_ref[:, vsl], cdec[hd])
            s_ref[slot, hd] = s_new
            y_scr[rows, vsl] = (_silu(g[rows]) * on).astype(BF16)
    out = jnp.dot(y_scr[...], wout_ref[...], preferred_element_type=F32)
    o_ref[...] = x_ref[...] + gm_ref[0] * out


def _ret_fused(x, mod, norm_g4, w_in, w_out, gn_g, cos, sin, s_prev, lc, tt, layer, widx, row0):
    b, t, d = x.shape
    assert tt % lc == 0 and t % tt == 0
    cdec = tuple(float(c) for c in _ret_gamma() ** float(lc))
    tril = jnp.asarray(np.tril(np.ones((lc, lc))), F32)
    qs, ks = _ret_scale_tables(tt, lc)
    slot = widx
    xs = pl.BlockSpec((None, tt, d), lambda b_, t_: (b_, t_, 0))
    tab = pl.BlockSpec((tt, ROPE_HALF), lambda b_, t_: (t_, 0))
    const2 = lambda shape: _resident(shape, lambda b_, t_: (0, 0))
    s_spec, s_shape, prev_spec = _state_stack_specs(slot, b)
    in_specs = [
        xs,
        _mod_spec(layer, 3, 1, row0), _mod_spec(layer, 4, 1, row0), _mod_spec(layer, 5, 1, row0),
        _ng_spec(layer, 1),
        _resident((None, d, 6 * d), lambda b_, t_: (widx, 0, 0)),
        _resident((None, 2 * d, d), lambda b_, t_: (widx, 0, 0)),
        tab, tab, const2((tt, d)), const2((tt, d)), const2((lc, lc)),
        pl.BlockSpec((1, 2 * d), lambda b_, t_: (0, 0)),
    ]
    args = [x, mod, mod, mod, norm_g4, w_in, w_out, cos, sin, qs, ks, tril, gn_g]
    if slot:
        in_specs.append(prev_spec)
        args.append(s_prev)
    return pl.pallas_call(
        functools.partial(_ret_fused_kernel, cdec, lc, slot),
        grid=(b, t // tt),
        in_specs=in_specs,
        out_specs=[xs, s_spec],
        out_shape=[jax.ShapeDtypeStruct(x.shape, F32), s_shape],
        scratch_shapes=[pltpu.VMEM((tt, d), BF16), pltpu.VMEM((tt, d), BF16),
                        pltpu.VMEM((tt, 2 * d), BF16), pltpu.VMEM((tt, 2 * d), BF16)],
        compiler_params=_cparams(2),
        name=f"ret_l{layer}",
    )(*args)


def _pool_kernel(pos0, x_ref, sh_ref, sc_ref, gm_ref, ng_ref, buf_ref, w_ref, pb_ref, ps_ref,
                 o_ref, nb_ref, carry_ref):
    tt, d = x_ref.shape
    t_idx = pl.program_id(1)

    @pl.when(t_idx == 0)
    def _():
        carry_ref[...] = buf_ref[...]

    x = x_ref[...]
    h = _norm_mod(x[None], ng_ref[...], sc_ref[...], sh_ref[...])[0]
    ext = jnp.concatenate([carry_ref[...], h], axis=0)
    carry_ref[...] = h[tt - POOL_CARRY:, :]
    nb_ref[...] = h[tt - POOL_BUF:, :]

    pos = pos0 + t_idx * tt + lax.broadcasted_iota(jnp.int32, (tt, 1), 0)
    run = ext
    w = 1
    outs = []
    for gi, win in enumerate(POOL_WINDOWS):
        while w < win:
            n = run.shape[0]
            run = run[w:, :] + run[:n - w, :]
            w *= 2
        wsum = run[run.shape[0] - tt:, :POOL_GC]
        inv_cnt = 1.0 / jnp.minimum(pos + 1, win).astype(F32)
        lo = gi * POOL_GC
        pooled = wsum * inv_cnt - h[:, lo:lo + POOL_GC]
        outs.append(jnp.dot(pooled.astype(BF16), w_ref[gi], preferred_element_type=F32))
        run = run[:, POOL_GC:]
    y = (jnp.concatenate(outs, axis=1) + pb_ref[...]) * ps_ref[...]
    o_ref[...] = x + gm_ref[0] * y


def _pool(x, mod, norm_g4, buf16, w, pb, ps, layer, widx, tt, row0, pos0):
    b, t, d = x.shape
    xs = pl.BlockSpec((None, tt, d), lambda b_, t_: (b_, t_, 0))
    vec = pl.BlockSpec((None, 1, d), lambda b_, t_: (widx, 0, 0))
    return pl.pallas_call(
        functools.partial(_pool_kernel, pos0),
        grid=(b, t // tt),
        in_specs=[
            xs,
            _mod_spec(layer, 3, 1, row0),
            _mod_spec(layer, 4, 1, row0),
            _mod_spec(layer, 5, 1, row0),
            _ng_spec(layer, 1),
            pl.BlockSpec((None, POOL_CARRY, d), lambda b_, t_: (b_, 0, 0)),
            pl.BlockSpec((None, POOL_GROUPS, POOL_GC, POOL_GC), lambda b_, t_: (widx, 0, 0, 0)),
            vec, vec,
        ],
        out_specs=[xs, pl.BlockSpec((None, POOL_BUF, d), lambda b_, t_: (b_, 0, 0))],
        out_shape=[jax.ShapeDtypeStruct(x.shape, F32),
                   jax.ShapeDtypeStruct((b, POOL_BUF, d), F32)],
        scratch_shapes=[pltpu.VMEM((POOL_CARRY, d), F32)],
        compiler_params=_cparams(2),
        name=f"pool_l{layer}",
    )(x, mod, mod, mod, norm_g4, buf16, w, pb, ps)


def _head_norm(y, pavg, gain):
    y2 = (y * y).astype(BF16)
    ms = jnp.concatenate(
        [jnp.dot(y2[:, g * ATT_GW:(g + 1) * ATT_GW], pavg, preferred_element_type=F32)
         for g in range(ATT_NGROUPS)], axis=1)
    return y * lax.rsqrt(ms + EPS) * gain


def _att_proj_kernel(first_keep_tile, x_ref, sh_ref, sc_ref, ng_ref, w_ref, p_ref, qg_ref, kg_ref,
                     q_ref, kb_ref, vb_ref, kk_ref, vk_ref):
    bb, tt, d = x_ref.shape
    h = _norm_mod(x_ref[...], ng_ref[...], sc_ref[...], sh_ref[...])
    hb = h.reshape(bb * tt, d).astype(BF16)
    q = jnp.dot(hb, w_ref[:, 0:d], preferred_element_type=F32)
    q = _head_norm(q, p_ref[...], qg_ref[...]) * ATT_Q_SCALE
    q_ref[...] = q.reshape(bb, tt, d).astype(BF16)
    k = jnp.dot(hb, w_ref[:, d:2 * d], preferred_element_type=F32)
    k = _head_norm(k, p_ref[...], kg_ref[...]).reshape(bb, tt, d)
    kb_ref[...] = k.astype(BF16)
    v = jnp.dot(hb, w_ref[:, 2 * d:3 * d], preferred_element_type=F32).reshape(bb, tt, d)
    vb_ref[...] = v.astype(BF16)

    @pl.when(pl.program_id(1) >= first_keep_tile)
    def _():
        kk_ref[...] = k
        vk_ref[...] = v


def _att_proj(x, mod, norm_g4, w_qkv, pavg, qg, kg, keep, layer, widx, bb, tt, row0):
    b, t, d = x.shape
    assert keep % tt == 0 and t % tt == 0
    first_keep_tile = (t - keep) // tt
    vec = pl.BlockSpec((None, 1, d), lambda b_, t_: (widx, 0, 0))
    keep_spec = pl.BlockSpec((bb, tt, d),
                             lambda b_, t_: (b_, jnp.maximum(t_ - first_keep_tile, 0), 0))
    b16o = jax.ShapeDtypeStruct((b, t, d), BF16)
    f32k = jax.ShapeDtypeStruct((b, keep, d), F32)
    return pl.pallas_call(
        functools.partial(_att_proj_kernel, first_keep_tile),
        grid=(b // bb, t // tt),
        in_specs=[
            _x_spec(bb, tt),
            _mod_spec(layer, 3, bb, row0),
            _mod_spec(layer, 4, bb, row0),
            _ng_spec(layer, 1),
            _resident((None, d, 3 * d), lambda b_, t_: (widx, 0, 0)),
            _resident((ATT_GW, ATT_GW), lambda b_, t_: (0, 0)),
            vec, vec,
        ],
        out_specs=[_x_spec(bb, tt)] * 3 + [keep_spec] * 2,
        out_shape=[b16o, b16o, b16o, f32k, f32k],
        compiler_params=_cparams(2),
        name=f"att_proj_l{layer}",
    )(x, mod, mod, norm_g4, w_qkv, pavg, qg, kg)


def _head_block_diag():
    rb = lax.broadcasted_iota(jnp.int32, (ATT_GW, ATT_GW), 0) // ATT_HD
    cb = lax.broadcasted_iota(jnp.int32, (ATT_GW, ATT_GW), 1) // ATT_HD
    lane_head = lax.broadcasted_iota(jnp.int32, (CHUNK, ATT_GW), 1) // ATT_HD
    return rb == cb, lane_head


def _att_chunk_group(qc, kb, vb, bias, masks, first_kpos):
    diag, lane_head = masks
    qbd = jnp.where(diag, jnp.concatenate([qc] * ATT_GROUP, axis=0), jnp.zeros((), BF16))
    s = lax.dot_general(qbd, kb, (((1,), (1,)), ((), ())), preferred_element_type=F32) + bias
    if first_kpos is not None:
        kpos = first_kpos + lax.broadcasted_iota(jnp.int32, (1, BAND), 1)
        s = jnp.where(kpos >= 0, s, NEG_INF)
    m = jnp.max(s, axis=-1, keepdims=True)
    p = jnp.exp2(s - m)
    inv = 1.0 / jnp.sum(p, axis=-1, keepdims=True)
    o = jnp.dot(p.astype(BF16), vb, preferred_element_type=F32)
    last = ATT_GROUP - 1
    og = o[last * CHUNK:] * inv[last * CHUNK:]
    for hh in range(last - 1, -1, -1):
        rows = slice(hh * CHUNK, (hh + 1) * CHUNK)
        og = jnp.where(lane_head == hh, o[rows] * inv[rows], og)
    return og


def _att_core_kernel(first_pos, has_past, *refs):
    if has_past:
        q_ref, k_ref, v_ref, pk_ref, pv_ref, bias_ref, o_ref, kbuf, vbuf = refs
    else:
        q_ref, k_ref, v_ref, bias_ref, o_ref, kbuf, vbuf = refs
    t = q_ref.shape[0]
    if has_past:
        kbuf[0:BAND_ROWS, :] = pk_ref[...].astype(BF16)
        vbuf[0:BAND_ROWS, :] = pv_ref[...].astype(BF16)
    else:
        kbuf[0:BAND_ROWS, :] = jnp.zeros((BAND_ROWS, ATT_GW), BF16)
        vbuf[0:BAND_ROWS, :] = jnp.zeros((BAND_ROWS, ATT_GW), BF16)
    kbuf[BAND_ROWS:, :] = k_ref[...]
    vbuf[BAND_ROWS:, :] = v_ref[...]
    masks = _head_block_diag()

    def make_chunk(masked):
        def chunk(n, carry):
            r0 = pl.multiple_of(n * CHUNK, CHUNK)
            og = _att_chunk_group(q_ref[pl.ds(r0, CHUNK), :], kbuf[pl.ds(r0, BAND), :],
                                  vbuf[pl.ds(r0, BAND), :], bias_ref[...], masks,
                                  first_pos + r0 if masked else None)
            o_ref[pl.ds(r0, CHUNK), :] = og.astype(BF16)
            return carry
        return chunk

    n_chunks = t // CHUNK
    n_masked = min(n_chunks, max(0, -(first_pos // CHUNK)))
    for lo, hi, masked in ((0, n_masked, True), (n_masked, n_chunks, False)):
        if hi > lo:
            trips = hi - lo
            unroll = next(u for u in (ATT_UNROLL, 2, 1) if trips % u == 0)
            lax.fori_loop(lo, hi, make_chunk(masked), 0, unroll=unroll)


def _att_core(q, k, v, past_k, past_v, bias, first_pos, name):
    b, t, d = q.shape
    has_past = past_k is not None
    grp = lambda rows: pl.BlockSpec((None, rows, ATT_GW), lambda b_, g_: (b_, 0, g_))
    in_specs = [grp(t), grp(t), grp(t)]
    args = [q, k, v]
    if has_past:
        in_specs += [grp(BAND_ROWS), grp(BAND_ROWS)]
        args += [past_k, past_v]
    in_specs.append(pl.BlockSpec((None, ATT_GW, BAND), lambda b_, g_: (g_, 0, 0)))
    args.append(bias)
    return pl.pallas_call(
        functools.partial(_att_core_kernel, first_pos, has_past),
        grid=(b, ATT_NGROUPS),
        in_specs=in_specs,
        out_specs=grp(t),
        out_shape=jax.ShapeDtypeStruct((b, t, d), BF16),
        scratch_shapes=[pltpu.VMEM((BAND_ROWS + t, ATT_GW), BF16)] * 2,
        compiler_params=_cparams(2),
        name=name,
    )(*args)


def _att_fused_kernel(first_keep_tile, x_ref, sh_ref, sc_ref, gm_ref, ng_ref, wqkv_ref, p_ref,
                      qg_ref, kg_ref, bias_ref, wo_ref, o_ref, kk_ref, vk_ref,
                      q_scr, kbuf, vbuf, a_scr):
    tt, d = x_ref.shape
    t_idx = pl.program_id(1)

    @pl.when(t_idx == 0)
    def _():
        kbuf[0:BAND_ROWS, :] = jnp.zeros((BAND_ROWS, d), BF16)
        vbuf[0:BAND_ROWS, :] = jnp.zeros((BAND_ROWS, d), BF16)

    @pl.when(t_idx > 0)
    def _():
        kbuf[0:BAND_ROWS, :] = kbuf[tt:tt + BAND_ROWS, :]
        vbuf[0:BAND_ROWS, :] = vbuf[tt:tt + BAND_ROWS, :]

    hb = _norm_mod(x_ref[...][None], ng_ref[...], sc_ref[...], sh_ref[...])[0].astype(BF16)
    q = jnp.dot(hb, wqkv_ref[:, 0:d], preferred_element_type=F32)
    q_scr[...] = (_head_norm(q, p_ref[...], qg_ref[...]) * ATT_Q_SCALE).astype(BF16)
    k = jnp.dot(hb, wqkv_ref[:, d:2 * d], preferred_element_type=F32)
    k = _head_norm(k, p_ref[...], kg_ref[...])
    kbuf[BAND_ROWS:, :] = k.astype(BF16)
    v = jnp.dot(hb, wqkv_ref[:, 2 * d:3 * d], preferred_element_type=F32)
    vbuf[BAND_ROWS:, :] = v.astype(BF16)

    @pl.when(t_idx >= first_keep_tile)
    def _():
        kk_ref[...] = k
        vk_ref[...] = v

    masks = _head_block_diag()

    def make_chunk(masked):
        def chunk(n, carry):
            r0 = pl.multiple_of(n * CHUNK, CHUNK)
            for g in range(ATT_NGROUPS):
                gsl = slice(g * ATT_GW, (g + 1) * ATT_GW)
                og = _att_chunk_group(q_scr[pl.ds(r0, CHUNK), gsl], kbuf[pl.ds(r0, BAND), gsl],
                                      vbuf[pl.ds(r0, BAND), gsl], bias_ref[g], masks,
                                      r0 - BAND_ROWS if masked else None)
                a_scr[pl.ds(r0, CHUNK), gsl] = og.astype(BF16)
            return carry
        return chunk

    @pl.when(t_idx == 0)
    def _():
        lax.fori_loop(0, tt // CHUNK, make_chunk(True), 0, unroll=2)

    @pl.when(t_idx > 0)
    def _():
        lax.fori_loop(0, tt // CHUNK, make_chunk(False), 0, unroll=2)

    out = jnp.dot(a_scr[...], wo_ref[...], preferred_element_type=F32)
    o_ref[...] = x_ref[...] + gm_ref[0] * out


def _att_fused(x, mod, norm_g4, w_qkv, w_o, pavg, qg, kg, bias, keep, layer, widx, tt, row0):
    b, t, d = x.shape
    assert tt >= BAND_ROWS and tt % CHUNK == 0 and (tt // CHUNK) % 2 == 0
    assert keep % tt == 0 and t % tt == 0
    first_keep_tile = (t - keep) // tt
    xs = pl.BlockSpec((None, tt, d), lambda b_, t_: (b_, t_, 0))
    vec = pl.BlockSpec((None, 1, d), lambda b_, t_: (widx, 0, 0))
    keep_spec = pl.BlockSpec((None, tt, d),
                             lambda b_, t_: (b_, jnp.maximum(t_ - first_keep_tile, 0), 0))
    f32k = jax.ShapeDtypeStruct((b, keep, d), F32)
    return pl.pallas_call(
        functools.partial(_att_fused_kernel, first_keep_tile),
        grid=(b, t // tt),
        in_specs=[
            xs,
            _mod_spec(layer, 3, 1, row0), _mod_spec(layer, 4, 1, row0), _mod_spec(layer, 5, 1, row0),
            _ng_spec(layer, 1),
            _resident((None, d, 3 * d), lambda b_, t_: (widx, 0, 0)),
            _resident((ATT_GW, ATT_GW), lambda b_, t_: (0, 0)),
            vec, vec,
            _resident((ATT_NGROUPS, ATT_GW, BAND), lambda b_, t_: (0, 0, 0)),
            _resident((None, d, d), lambda b_, t_: (widx, 0, 0)),
        ],
        out_specs=[xs, keep_spec, keep_spec],
        out_shape=[jax.ShapeDtypeStruct(x.shape, F32), f32k, f32k],
        scratch_shapes=[pltpu.VMEM((tt, d), BF16), pltpu.VMEM((BAND_ROWS + tt, d), BF16),
                        pltpu.VMEM((BAND_ROWS + tt, d), BF16), pltpu.VMEM((tt, d), BF16)],
        compiler_params=_cparams(2),
        name=f"att_l{layer}",
    )(x, mod, mod, mod, norm_g4, w_qkv, pavg, qg, kg, bias, w_o)


def _att_bias_kernel(rb_ref, o_ref):
    rb = rb_ref[...]
    hi = rb.astype(BF16)
    r1 = rb - hi.astype(F32)
    mid = r1.astype(BF16)
    lo = (r1 - mid.astype(F32)).astype(BF16)
    r = lax.broadcasted_iota(jnp.int32, (REL_SIZE, REL_BASE_W), 0)
    u = lax.broadcasted_iota(jnp.int32, (REL_SIZE, REL_BASE_W), 1)
    rel = jnp.clip(BAND_ROWS + CHUNK - 1 - u, REL_MIN, REL_MAX) - REL_MIN
    sel = jnp.where(r == rel, 1.0, 0.0).astype(BF16)
    base = (jnp.dot(hi, sel, preferred_element_type=F32)
            + jnp.dot(mid, sel, preferred_element_type=F32)
            + jnp.dot(lo, sel, preferred_element_type=F32)) * LOG2E
    for i in range(CHUNK):
        o_ref[i] = base[:, CHUNK - 1 - i:CHUNK - 1 - i + BAND]


def _att_bias_table(rel_bias):
    tab = pl.pallas_call(
        _att_bias_kernel,
        out_shape=jax.ShapeDtypeStruct((CHUNK, ATT_HEADS, BAND), F32),
        name="att_bias",
    )(rel_bias)
    return tab.transpose(1, 0, 2).reshape(ATT_NGROUPS, ATT_GW, BAND)


def _rope_tables(pos0, t):
    inv = ROPE_BASE ** (-np.arange(ROPE_HALF, dtype=np.float64) / ROPE_HALF)
    ang = (pos0 + np.arange(t, dtype=np.float64))[:, None] * inv[None, :]
    return jnp.asarray(np.cos(ang), F32), jnp.asarray(np.sin(ang), F32)


def _trunk(x, mod, row0, pos0, bb, tt, ret_lc, ret_states, pool_bufs, att_k_cache, att_v_cache, p):
    b, t, d = x.shape
    new_pool, new_k, new_v = [], [], []
    ret_all = None
    cos, sin = _rope_tables(pos0, t)
    tt_ffn = FFN_ROWS if (bb == 1 and t % FFN_ROWS == 0) else tt
    bb_ffn = bb
    if tt == t and FFN_ROWS % t == 0:
        wide = min(b, FFN_ROWS // t)
        if b % wide == 0 and row0 % wide == 0:
            bb_ffn = wide
    for i in range(DEPTH):
        x = _ffn(x, mod, p['norm_g'], p['ffn1_w_in'], p['ffn1_w_out'], i, 1, bb_ffn, tt_ffn, row0)
        j = i // N_MIXERS
        kind = i % N_MIXERS
        if kind == 0:
            gn = p['ret_gn_g'][j].reshape(1, 2 * d)
            if ret_states is None:
                x, ret_all = _ret_fused(x, mod, p['norm_g'], p['ret_w_in'], p['ret_w_out'], gn,
                                        cos, sin, ret_all, ret_lc, max(tt, ret_lc),
                                        i, j, row0)
            else:
                q, k, v, g = _ret_proj(x, mod, p['norm_g'], p['ret_w_in'], cos, sin, ret_lc,
                                       i, j, bb, tt, row0)
                y, ret_all = _ret_core(q, k, v, g, ret_states, ret_all, j, gn, ret_lc,
                                       f"ret_core_l{i}")
                x = _matres(y, x, mod, p['ret_w_out'], i, j, bb, tt, row0, f"ret_out_l{i}")
        elif kind == 1:
            if pool_bufs is None:
                buf16 = jnp.zeros((b, POOL_CARRY, d), F32)
            else:
                buf16 = jnp.pad(pool_bufs[j], ((0, 0), (POOL_CARRY - POOL_BUF, 0), (0, 0)))
            x, nb = _pool(x, mod, p['norm_g'], buf16, p['pool_w'], p['pool_b'], p['pool_scale'],
                          i, j, min(tt, t), row0, pos0)
            new_pool.append(nb)
        else:
            keep = min(BAND_ROWS, t)
            if att_k_cache is None and pos0 == 0 and tt >= BAND_ROWS:
                x, k_keep, v_keep = _att_fused(
                    x, mod, p['norm_g'], p['att_w_qkv'], p['att_w_o'], p['att_pavg'],
                    p['att_q_g'], p['att_k_g'], p['att_bias'][j], keep, i, j, tt, row0)
            else:
                q, kb, vb, k_keep, v_keep = _att_proj(
                    x, mod, p['norm_g'], p['att_w_qkv'], p['att_pavg'], p['att_q_g'],
                    p['att_k_g'], keep, i, j, bb, tt, row0)
                if att_k_cache is None:
                    past_k = past_v = None
                else:
                    past_k = att_k_cache[j].reshape(b, BAND_ROWS, d)
                    past_v = att_v_cache[j].reshape(b, BAND_ROWS, d)
                o = _att_core(q, kb, vb, past_k, past_v, p['att_bias'][j], pos0 - BAND_ROWS,
                              f"att_core_l{i}")
                x = _matres(o, x, mod, p['att_w_o'], i, j, bb, tt, row0, f"att_out_l{i}")
            new_k.append(k_keep.reshape(b, keep, ATT_HEADS, ATT_HD))
            new_v.append(v_keep.reshape(b, keep, ATT_HEADS, ATT_HD))
        x = _ffn(x, mod, p['norm_g'], p['ffn2_w_in'], p['ffn2_w_out'], i, 2, bb_ffn, tt_ffn, row0)
    return x, ret_all, jnp.stack(new_pool), jnp.stack(new_k), jnp.stack(new_v)


def kernel(x_prompt, x_sample, c_prompt, c_sample, state_ret, state_pool, cache_att_k, cache_att_v, norm_g, ada_w, ada_b, ffn1_w_in, ffn1_w_out, ffn2_w_in, ffn2_w_out, ret_w_in, ret_w_out, ret_gn_g, pool_w, pool_b, pool_scale, att_w_qkv, att_w_o, att_q_g, att_k_g, att_rel_bias):
    assert cache_att_k.shape[2] == BAND_ROWS, "key cache must hold exactly the left band"
    bp, tp, d = x_prompt.shape
    bs, ts, _ = x_sample.shape
    n_att = att_w_qkv.shape[0]
    head_id = np.arange(ATT_GW) // ATT_HD
    pavg = jnp.asarray((head_id[:, None] == head_id[None, :]) / ATT_HD, BF16)
    p = {
        'norm_g': norm_g.reshape(DEPTH, 3, 1, d),
        'ffn1_w_in': _interleave_gate_up(ffn1_w_in), 'ffn1_w_out': ffn1_w_out.astype(BF16),
        'ffn2_w_in': _interleave_gate_up(ffn2_w_in), 'ffn2_w_out': ffn2_w_out.astype(BF16),
        'ret_w_in': ret_w_in.astype(BF16), 'ret_w_out': ret_w_out.astype(BF16),
        'ret_gn_g': ret_gn_g,
        'pool_w': pool_w.astype(BF16),
        'pool_b': pool_b.reshape(-1, 1, d), 'pool_scale': pool_scale.reshape(-1, 1, d),
        'att_w_qkv': att_w_qkv.astype(BF16), 'att_w_o': att_w_o.astype(BF16),
        'att_q_g': jnp.tile(att_q_g, (1, ATT_HEADS)).reshape(n_att, 1, d),
        'att_k_g': jnp.tile(att_k_g, (1, ATT_HEADS)).reshape(n_att, 1, d),
        'att_pavg': pavg,
        'att_bias': [_att_bias_table(att_rel_bias[j]) for j in range(n_att)],
    }
    mod = _ada(jnp.concatenate([c_prompt, c_sample], axis=0), ada_w, ada_b)

    tt_p = min(PROMPT_ROWS, tp)
    y_prompt, ret_p, pool_p, k_p, v_p = _trunk(
        x_prompt, mod, 0, 0, 1, tt_p, min(RET_CHUNK_PROMPT, tp), None, None, None, None, p)
    bb_s = max(1, min(bs, PROMPT_ROWS // ts))
    y_sample, ret_s, pool_s, k_s, v_s = _trunk(
        x_sample, mod, bp, PAST_LEN, bb_s, ts, ts, state_ret, state_pool,
        cache_att_k, cache_att_v, p)
    return (y_prompt, y_sample, ret_p, ret_s, pool_p, pool_s, k_p, v_p, k_s, v_s)
```

```python
import functools

import numpy as np
import jax
import jax.numpy as jnp
from jax import lax
from jax.experimental import pallas as pl
from jax.experimental.pallas import tpu as pltpu

F32 = jnp.float32
BF16 = jnp.bfloat16

LANES = 128
D_MODEL = 1024
DEPTH = 4
D_FF = 2816
EPS = 1e-6
CHUNK = 64
PAST_LEN = 2048
N_MIXERS = 3
RET_HEADS = 4
RET_DK = D_MODEL // RET_HEADS
RET_DV = 2 * D_MODEL // RET_HEADS
ROPE_BASE = 10000.0
ROPE_HALF = RET_DK // 2
POOL_WINDOWS = (2, 4, 8, 16)
POOL_GROUPS = 4
POOL_GC = D_MODEL // POOL_GROUPS
POOL_BUF = 15
POOL_CARRY = 16
ATT_HEADS = 16
ATT_HD = D_MODEL // ATT_HEADS
ATT_GROUP = 4
ATT_GW = ATT_GROUP * ATT_HD
ATT_NGROUPS = ATT_HEADS // ATT_GROUP
LEFT_CHUNKS = 8
BAND_ROWS = LEFT_CHUNKS * CHUNK
BAND = BAND_ROWS + CHUNK
REL_MIN = -(CHUNK - 1)
REL_MAX = 256
REL_SIZE = REL_MAX - REL_MIN + 1
REL_BASE_W = 640
NEG_INF = -1e30
LOG2E = 1.4426950408889634
ATT_Q_SCALE = ATT_HD ** -0.5 * LOG2E
N_MOD = 9

VMEM_LIMIT_BYTES = 56 * 1024 * 1024
PROMPT_ROWS = 512
FFN_ROWS = 1024
RET_CHUNK_PROMPT = 256
ATT_UNROLL = 8


def _cparams(n_grid):
    return pltpu.CompilerParams(
        dimension_semantics=("arbitrary",) * n_grid,
        vmem_limit_bytes=VMEM_LIMIT_BYTES,
    )


def _resident(block_shape, index_map):
    return pl.BlockSpec(block_shape, index_map, pipeline_mode=pl.Buffered(1))


def _norm_mod(x, ng, sc, sh):
    ms = jnp.mean(x * x, axis=-1, keepdims=True)
    y = x * lax.rsqrt(ms + EPS) * ng
    return y * (1.0 + sc) + sh


def _silu(x):
    return x * (1.0 / (1.0 + jnp.exp(-x)))


def _ada_kernel(c_ref, w_ref, b_ref, o_ref):
    c = _silu(c_ref[...]).astype(BF16)
    w = w_ref[...].astype(BF16)
    o_ref[...] = jnp.dot(c, w, preferred_element_type=F32) + b_ref[...]


def _ada(c_all, ada_w, ada_b):
    n = c_all.shape[0]
    b4 = ada_b.reshape(DEPTH, N_MOD, 1, D_MODEL)
    out = pl.pallas_call(
        _ada_kernel,
        grid=(DEPTH, N_MOD),
        in_specs=[
            pl.BlockSpec((n, D_MODEL), lambda i, j: (0, 0)),
            pl.BlockSpec((None, D_MODEL, D_MODEL), lambda i, j: (i, 0, j)),
            pl.BlockSpec((None, None, 1, D_MODEL), lambda i, j: (i, j, 0, 0)),
        ],
        out_specs=pl.BlockSpec((None, None, n, D_MODEL), lambda i, j: (i, j, 0, 0)),
        out_shape=jax.ShapeDtypeStruct((DEPTH, N_MOD, n, D_MODEL), F32),
        compiler_params=_cparams(2),
        name="ada_mod",
    )(c_all, ada_w, b4)
    return out.reshape(DEPTH, N_MOD, n, 1, D_MODEL)


def _mod_spec(layer, j, bb, row0):
    blk0 = row0 // bb
    return pl.BlockSpec((None, None, bb, 1, D_MODEL),
                        lambda b, t: (layer, j, blk0 + b, 0, 0))


def _ng_spec(layer, j):
    return pl.BlockSpec((None, None, 1, D_MODEL), lambda b, t: (layer, j, 0, 0))


def _x_spec(bb, tt, width=D_MODEL):
    return pl.BlockSpec((bb, tt, width), lambda b, t: (b, t, 0))


GATE_UP_TILES = 2


def _gate_up_kernel(g_ref, u_ref, o_ref):
    for i in range(GATE_UP_TILES):
        src = slice(i * LANES, (i + 1) * LANES)
        o_ref[:, 2 * i * LANES:(2 * i + 1) * LANES] = g_ref[:, src].astype(BF16)
        o_ref[:, (2 * i + 1) * LANES:(2 * i + 2) * LANES] = u_ref[:, src].astype(BF16)


def _interleave_gate_up(w_in):
    nl, d, _ = w_in.shape
    width = GATE_UP_TILES * LANES
    steps = D_FF // width
    assert D_FF % width == 0
    return pl.pallas_call(
        _gate_up_kernel,
        grid=(nl, steps),
        in_specs=[pl.BlockSpec((None, d, width), lambda l, j: (l, 0, j)),
                  pl.BlockSpec((None, d, width), lambda l, j: (l, 0, steps + j))],
        out_specs=pl.BlockSpec((None, d, 2 * width), lambda l, j: (l, 0, j)),
        out_shape=jax.ShapeDtypeStruct((nl, d, 2 * D_FF), BF16),
        compiler_params=_cparams(2),
        name="ffn_gate_up_layout",
    )(w_in, w_in)


def _ffn_kernel(x_ref, sh_ref, sc_ref, g_ref, ng_ref, win_ref, wout_ref, o_ref):
    bb, tt, d = x_ref.shape
    x = x_ref[...]
    h = _norm_mod(x, ng_ref[...], sc_ref[...], sh_ref[...])
    hb = h.reshape(bb * tt, d).astype(BF16)
    gu = jnp.dot(hb, win_ref[...], preferred_element_type=F32)
    act = jnp.concatenate(
        [_silu(gu[:, 2 * j * LANES:(2 * j + 1) * LANES]) * gu[:, (2 * j + 1) * LANES:(2 * j + 2) * LANES]
         for j in range(D_FF // LANES)], axis=1).astype(BF16)
    y = jnp.dot(act, wout_ref[...], preferred_element_type=F32)
    o_ref[...] = x + (0.5 * g_ref[...]) * y.reshape(bb, tt, d)


def _ffn(x, mod, norm_g4, w_in, w_out, layer, which, bb, tt, row0):
    b, t, d = x.shape
    j0 = 0 if which == 1 else 6
    nj = 0 if which == 1 else 2
    return pl.pallas_call(
        _ffn_kernel,
        grid=(b // bb, t // tt),
        in_specs=[
            _x_spec(bb, tt),
            _mod_spec(layer, j0, bb, row0),
            _mod_spec(layer, j0 + 1, bb, row0),
            _mod_spec(layer, j0 + 2, bb, row0),
            _ng_spec(layer, nj),
            _resident((None, D_MODEL, 2 * D_FF), lambda b_, t_: (layer, 0, 0)),
            _resident((None, D_FF, D_MODEL), lambda b_, t_: (layer, 0, 0)),
        ],
        out_specs=_x_spec(bb, tt),
        out_shape=jax.ShapeDtypeStruct(x.shape, F32),
        compiler_params=_cparams(2),
        name=f"ffn{which}_l{layer}",
    )(x, mod, mod, mod, norm_g4, w_in, w_out)


def _matres_kernel(a_ref, x_ref, g_ref, w_ref, o_ref):
    bb, tt, k = a_ref.shape
    y = jnp.dot(a_ref[...].reshape(bb * tt, k), w_ref[...], preferred_element_type=F32)
    o_ref[...] = x_ref[...] + g_ref[...] * y.reshape(bb, tt, D_MODEL)


def _matres(a, x, mod, w, layer, widx, bb, tt, row0, name):
    b, t, k = a.shape
    return pl.pallas_call(
        _matres_kernel,
        grid=(b // bb, t // tt),
        in_specs=[
            _x_spec(bb, tt, k),
            _x_spec(bb, tt),
            _mod_spec(layer, 5, bb, row0),
            _resident((None, k, D_MODEL), lambda b_, t_: (widx, 0, 0)),
        ],
        out_specs=_x_spec(bb, tt),
        out_shape=jax.ShapeDtypeStruct(x.shape, F32),
        compiler_params=_cparams(2),
        name=name,
    )(a, x, mod, w)


def _ret_proj_kernel(x_ref, sh_ref, sc_ref, ng_ref, w_ref, cos_ref, sin_ref, qs_ref, ks_ref,
                     q_ref, k_ref, v_ref, g_ref):
    bb, tt, d = x_ref.shape
    h = _norm_mod(x_ref[...], ng_ref[...], sc_ref[...], sh_ref[...])
    hb = h.reshape(bb * tt, d).astype(BF16)
    cos = cos_ref[...][None]
    sin = sin_ref[...][None]
    for idx, (o_ref, s_ref) in enumerate(((q_ref, qs_ref), (k_ref, ks_ref))):
        y = jnp.dot(hb, w_ref[:, idx * d:(idx + 1) * d], preferred_element_type=F32)
        y = y.reshape(bb, tt, d)
        for hd in range(RET_HEADS):
            lo = hd * RET_DK
            mid = lo + ROPE_HALF
            x1 = y[:, :, lo:mid]
            x2 = y[:, :, mid:lo + RET_DK]
            o_ref[:, :, lo:mid] = ((x1 * cos - x2 * sin) * s_ref[:, lo:mid][None]).astype(BF16)
            o_ref[:, :, mid:lo + RET_DK] = (
                (x1 * sin + x2 * cos) * s_ref[:, mid:lo + RET_DK][None]).astype(BF16)
    v = jnp.dot(hb, w_ref[:, 2 * d:4 * d], preferred_element_type=F32)
    v_ref[...] = v.reshape(bb, tt, 2 * d).astype(BF16)
    g = jnp.dot(hb, w_ref[:, 4 * d:6 * d], preferred_element_type=F32)
    g_ref[...] = _silu(g).reshape(bb, tt, 2 * d).astype(BF16)


def _ret_gamma():
    return 1.0 - 2.0 ** (-5.0 - np.arange(RET_HEADS, dtype=np.float64))


def _ret_scale_tables(tt, lc):
    gamma = _ret_gamma()
    i = (np.arange(tt) % lc).astype(np.float64)[:, None]
    g_lane = np.repeat(gamma, RET_DK)[None, :]
    qs = g_lane ** (i + 1.0)
    ks = (RET_DK ** -0.5) * g_lane ** (-(i + 1.0))
    return jnp.asarray(qs, F32), jnp.asarray(ks, F32)


def _ret_proj(x, mod, norm_g4, w_in, cos, sin, lc, layer, widx, bb, tt, row0):
    b, t, d = x.shape
    assert tt % lc == 0
    qs, ks = _ret_scale_tables(tt, lc)
    tab = pl.BlockSpec((tt, ROPE_HALF), lambda b_, t_: (t_, 0))
    scale = _resident((tt, d), lambda b_, t_: (0, 0))
    return pl.pallas_call(
        _ret_proj_kernel,
        grid=(b // bb, t // tt),
        in_specs=[
            _x_spec(bb, tt),
            _mod_spec(layer, 3, bb, row0),
            _mod_spec(layer, 4, bb, row0),
            _ng_spec(layer, 1),
            _resident((None, D_MODEL, 6 * D_MODEL), lambda b_, t_: (widx, 0, 0)),
            tab, tab, scale, scale,
        ],
        out_specs=[_x_spec(bb, tt), _x_spec(bb, tt),
                   _x_spec(bb, tt, 2 * d), _x_spec(bb, tt, 2 * d)],
        out_shape=[jax.ShapeDtypeStruct((b, t, d), BF16), jax.ShapeDtypeStruct((b, t, d), BF16),
                   jax.ShapeDtypeStruct((b, t, 2 * d), BF16), jax.ShapeDtypeStruct((b, t, 2 * d), BF16)],
        compiler_params=_cparams(2),
        name=f"ret_proj_l{layer}",
    )(x, mod, mod, norm_g4, w_in, cos, sin, qs, ks)


def _ret_head(qh, kh, vh, s, tril, gn, cdec_h):
    inner = lax.dot_general(qh, kh, (((1,), (1,)), ((), ())), preferred_element_type=F32) * tril
    o = (jnp.dot(inner.astype(BF16), vh, preferred_element_type=F32)
         + jnp.dot(qh, s.astype(BF16), preferred_element_type=F32))
    s_new = (s + lax.dot_general(kh, vh, (((0,), (0,)), ((), ())),
                                 preferred_element_type=F32)) * cdec_h
    mu = jnp.mean(o, axis=-1, keepdims=True)
    oc = o - mu
    var = jnp.mean(oc * oc, axis=-1, keepdims=True)
    return oc * lax.rsqrt(var + EPS) * gn, s_new


def _init_state_stack(s_ref, slot, prev_ref, s0_ref):
    if slot:
        s_ref[0:slot] = prev_ref[...]
    if s0_ref is None:
        s_ref[slot] = jnp.zeros(s_ref.shape[1:], F32)
    else:
        s_ref[slot] = s0_ref[...]


def _ret_core_kernel(cdec, has_state, slot, *refs):
    refs = list(refs)
    q_ref, k_ref, v_ref, g_ref = refs[:4]
    s0_ref = refs[4] if has_state else None
    tril_ref, gn_ref = refs[4 + has_state:6 + has_state]
    prev_ref = refs[6 + has_state] if slot else None
    y_ref, s_ref = refs[6 + has_state + bool(slot):]

    @pl.when(pl.program_id(1) == 0)
    def _():
        _init_state_stack(s_ref, slot, prev_ref, s0_ref)

    for hd in range(RET_HEADS):
        ksl = slice(hd * RET_DK, (hd + 1) * RET_DK)
        vsl = slice(hd * RET_DV, (hd + 1) * RET_DV)
        on, s_new = _ret_head(q_ref[:, ksl], k_ref[:, ksl], v_ref[:, vsl], s_ref[slot, hd],
                              tril_ref[...], gn_ref[:, vsl], cdec[hd])
        s_ref[slot, hd] = s_new
        y_ref[:, vsl] = (g_ref[:, vsl].astype(F32) * on).astype(BF16)


def _state_stack_specs(slot, b):
    blk = lambda n: pl.BlockSpec((n, None, RET_HEADS, RET_DK, RET_DV),
                                 lambda b_, t_: (0, b_, 0, 0, 0))
    shape = jax.ShapeDtypeStruct((slot + 1, b, RET_HEADS, RET_DK, RET_DV), F32)
    return blk(slot + 1), shape, (blk(slot) if slot else None)


def _ret_core(q, k, v, g, s0_all, s_prev, slot, gn_g, lc, name):
    b, t, d = q.shape
    cdec = tuple(float(c) for c in _ret_gamma() ** float(lc))
    tril = jnp.asarray(np.tril(np.ones((lc, lc))), F32)
    has_state = s0_all is not None
    row = lambda width: pl.BlockSpec((None, lc, width), lambda b_, t_: (b_, t_, 0))
    s_spec, s_shape, prev_spec = _state_stack_specs(slot, b)
    in_specs = [row(d), row(d), row(2 * d), row(2 * d)]
    args = [q, k, v, g]
    if has_state:
        in_specs.append(pl.BlockSpec((None, None, RET_HEADS, RET_DK, RET_DV),
                                     lambda b_, t_: (slot, b_, 0, 0, 0)))
        args.append(s0_all)
    in_specs += [_resident((lc, lc), lambda b_, t_: (0, 0)),
                 pl.BlockSpec((1, 2 * d), lambda b_, t_: (0, 0))]
    args += [tril, gn_g]
    if slot:
        in_specs.append(prev_spec)
        args.append(s_prev)
    return pl.pallas_call(
        functools.partial(_ret_core_kernel, cdec, has_state, slot),
        grid=(b, t // lc),
        in_specs=in_specs,
        out_specs=[row(2 * d), s_spec],
        out_shape=[jax.ShapeDtypeStruct((b, t, 2 * d), BF16), s_shape],
        compiler_params=_cparams(2),
        name=name,
    )(*args)


def _ret_fused_kernel(cdec, lc, slot, *refs):
    refs = list(refs)
    (x_ref, sh_ref, sc_ref, gm_ref, ng_ref, win_ref, wout_ref, cos_ref, sin_ref, qs_ref, ks_ref,
     tril_ref, gn_ref) = refs[:13]
    prev_ref = refs[13] if slot else None
    o_ref, s_ref, q_scr, k_scr, v_scr, y_scr = refs[13 + bool(slot):]
    tt, d = x_ref.shape

    @pl.when(pl.program_id(1) == 0)
    def _():
        _init_state_stack(s_ref, slot, prev_ref, None)

    hb = _norm_mod(x_ref[...][None], ng_ref[...], sc_ref[...], sh_ref[...])[0].astype(BF16)
    cos = cos_ref[...]
    sin = sin_ref[...]
    for idx, (scr, tab) in enumerate(((q_scr, qs_ref), (k_scr, ks_ref))):
        y = jnp.dot(hb, win_ref[:, idx * d:(idx + 1) * d], preferred_element_type=F32)
        for hd in range(RET_HEADS):
            lo = hd * RET_DK
            mid = lo + ROPE_HALF
            hi = lo + RET_DK
            x1 = y[:, lo:mid]
            x2 = y[:, mid:hi]
            scr[:, lo:mid] = ((x1 * cos - x2 * sin) * tab[:, lo:mid]).astype(BF16)
            scr[:, mid:hi] = ((x1 * sin + x2 * cos) * tab[:, mid:hi]).astype(BF16)
    v_scr[...] = jnp.dot(hb, win_ref[:, 2 * d:4 * d], preferred_element_type=F32).astype(BF16)
    for hd in range(RET_HEADS):
        ksl = slice(hd * RET_DK, (hd + 1) * RET_DK)
        vsl = slice(hd * RET_DV, (hd + 1) * RET_DV)
        g = jnp.dot(hb, win_ref[:, 4 * d + hd * RET_DV:4 * d + (hd + 1) * RET_DV],
                    preferred_element_type=F32)
        for c in range(tt // lc):
            rows = slice(c * lc, (c + 1) * lc)
            on, s_new = _ret_head(q_scr[rows, ksl], k_scr[rows, ksl], v_scr[rows, vsl],
                                  s_ref[slot, hd], tril_ref[...], gn_ref[:, vsl], cdec[hd])
            s_ref[slot, hd] = s_new
            y_scr[rows, vsl] = (_silu(g[rows]) * on).astype(BF16)
    out = jnp.dot(y_scr[...], wout_ref[...], preferred_element_type=F32)
    o_ref[...] = x_ref[...] + gm_ref[0] * out


def _ret_fused(x, mod, norm_g4, w_in, w_out, gn_g, cos, sin, s_prev, lc, tt, layer, widx, row0):
    b, t, d = x.shape
    assert tt % lc == 0 and t % tt == 0
    cdec = tuple(float(c) for c in _ret_gamma() ** float(lc))
    tril = jnp.asarray(np.tril(np.ones((lc, lc))), F32)
    qs, ks = _ret_scale_tables(tt, lc)
    slot = widx
    xs = pl.BlockSpec((None, tt, d), lambda b_, t_: (b_, t_, 0))
    tab = pl.BlockSpec((tt, ROPE_HALF), lambda b_, t_: (t_, 0))
    const2 = lambda shape: _resident(shape, lambda b_, t_: (0, 0))
    s_spec, s_shape, prev_spec = _state_stack_specs(slot, b)
    in_specs = [
        xs,
        _mod_spec(layer, 3, 1, row0), _mod_spec(layer, 4, 1, row0), _mod_spec(layer, 5, 1, row0),
        _ng_spec(layer, 1),
        _resident((None, d, 6 * d), lambda b_, t_: (widx, 0, 0)),
        _resident((None, 2 * d, d), lambda b_, t_: (widx, 0, 0)),
        tab, tab, const2((tt, d)), const2((tt, d)), const2((lc, lc)),
        pl.BlockSpec((1, 2 * d), lambda b_, t_: (0, 0)),
    ]
    args = [x, mod, mod, mod, norm_g4, w_in, w_out, cos, sin, qs, ks, tril, gn_g]
    if slot:
        in_specs.append(prev_spec)
        args.append(s_prev)
    return pl.pallas_call(
        functools.partial(_ret_fused_kernel, cdec, lc, slot),
        grid=(b, t // tt),
        in_specs=in_specs,
        out_specs=[xs, s_spec],
        out_shape=[jax.ShapeDtypeStruct(x.shape, F32), s_shape],
        scratch_shapes=[pltpu.VMEM((tt, d), BF16), pltpu.VMEM((tt, d), BF16),
                        pltpu.VMEM((tt, 2 * d), BF16), pltpu.VMEM((tt, 2 * d), BF16)],
        compiler_params=_cparams(2),
        name=f"ret_l{layer}",
    )(*args)


def _pool_kernel(pos0, x_ref, sh_ref, sc_ref, gm_ref, ng_ref, buf_ref, w_ref, pb_ref, ps_ref,
                 o_ref, nb_ref, carry_ref):
    tt, d = x_ref.shape
    t_idx = pl.program_id(1)

    @pl.when(t_idx == 0)
    def _():
        carry_ref[...] = buf_ref[...]

    x = x_ref[...]
    h = _norm_mod(x[None], ng_ref[...], sc_ref[...], sh_ref[...])[0]
    ext = jnp.concatenate([carry_ref[...], h], axis=0)
    carry_ref[...] = h[tt - POOL_CARRY:, :]
    nb_ref[...] = h[tt - POOL_BUF:, :]

    pos = pos0 + t_idx * tt + lax.broadcasted_iota(jnp.int32, (tt, 1), 0)
    run = ext
    w = 1
    outs = []
    for gi, win in enumerate(POOL_WINDOWS):
        while w < win:
            n = run.shape[0]
            run = run[w:, :] + run[:n - w, :]
            w *= 2
        wsum = run[run.shape[0] - tt:, :POOL_GC]
        inv_cnt = 1.0 / jnp.minimum(pos + 1, win).astype(F32)
        lo = gi * POOL_GC
        pooled = wsum * inv_cnt - h[:, lo:lo + POOL_GC]
        outs.append(jnp.dot(pooled.astype(BF16), w_ref[gi], preferred_element_type=F32))
        run = run[:, POOL_GC:]
    y = (jnp.concatenate(outs, axis=1) + pb_ref[...]) * ps_ref[...]
    o_ref[...] = x + gm_ref[0] * y


def _pool(x, mod, norm_g4, buf16, w, pb, ps, layer, widx, tt, row0, pos0):
    b, t, d = x.shape
    xs = pl.BlockSpec((None, tt, d), lambda b_, t_: (b_, t_, 0))
    vec = pl.BlockSpec((None, 1, d), lambda b_, t_: (widx, 0, 0))
    return pl.pallas_call(
        functools.partial(_pool_kernel, pos0),
        grid=(b, t // tt),
        in_specs=[
            xs,
            _mod_spec(layer, 3, 1, row0),
            _mod_spec(layer, 4, 1, row0),
            _mod_spec(layer, 5, 1, row0),
            _ng_spec(layer, 1),
            pl.BlockSpec((None, POOL_CARRY, d), lambda b_, t_: (b_, 0, 0)),
            pl.BlockSpec((None, POOL_GROUPS, POOL_GC, POOL_GC), lambda b_, t_: (widx, 0, 0, 0)),
            vec, vec,
        ],
        out_specs=[xs, pl.BlockSpec((None, POOL_BUF, d), lambda b_, t_: (b_, 0, 0))],
        out_shape=[jax.ShapeDtypeStruct(x.shape, F32),
                   jax.ShapeDtypeStruct((b, POOL_BUF, d), F32)],
        scratch_shapes=[pltpu.VMEM((POOL_CARRY, d), F32)],
        compiler_params=_cparams(2),
        name=f"pool_l{layer}",
    )(x, mod, mod, mod, norm_g4, buf16, w, pb, ps)


def _head_norm(y, pavg, gain):
    y2 = (y * y).astype(BF16)
    ms = jnp.concatenate(
        [jnp.dot(y2[:, g * ATT_GW:(g + 1) * ATT_GW], pavg, preferred_element_type=F32)
         for g in range(ATT_NGROUPS)], axis=1)
    return y * lax.rsqrt(ms + EPS) * gain


def _att_proj_kernel(first_keep_tile, x_ref, sh_ref, sc_ref, ng_ref, w_ref, p_ref, qg_ref, kg_ref,
                     q_ref, kb_ref, vb_ref, kk_ref, vk_ref):
    bb, tt, d = x_ref.shape
    h = _norm_mod(x_ref[...], ng_ref[...], sc_ref[...], sh_ref[...])
    hb = h.reshape(bb * tt, d).astype(BF16)
    q = jnp.dot(hb, w_ref[:, 0:d], preferred_element_type=F32)
    q = _head_norm(q, p_ref[...], qg_ref[...]) * ATT_Q_SCALE
    q_ref[...] = q.reshape(bb, tt, d).astype(BF16)
    k = jnp.dot(hb, w_ref[:, d:2 * d], preferred_element_type=F32)
    k = _head_norm(k, p_ref[...], kg_ref[...]).reshape(bb, tt, d)
    kb_ref[...] = k.astype(BF16)
    v = jnp.dot(hb, w_ref[:, 2 * d:3 * d], preferred_element_type=F32).reshape(bb, tt, d)
    vb_ref[...] = v.astype(BF16)

    @pl.when(pl.program_id(1) >= first_keep_tile)
    def _():
        kk_ref[...] = k
        vk_ref[...] = v


def _att_proj(x, mod, norm_g4, w_qkv, pavg, qg, kg, keep, layer, widx, bb, tt, row0):
    b, t, d = x.shape
    assert keep % tt == 0 and t % tt == 0
    first_keep_tile = (t - keep) // tt
    vec = pl.BlockSpec((None, 1, d), lambda b_, t_: (widx, 0, 0))
    keep_spec = pl.BlockSpec((bb, tt, d),
                             lambda b_, t_: (b_, jnp.maximum(t_ - first_keep_tile, 0), 0))
    b16o = jax.ShapeDtypeStruct((b, t, d), BF16)
    f32k = jax.ShapeDtypeStruct((b, keep, d), F32)
    return pl.pallas_call(
        functools.partial(_att_proj_kernel, first_keep_tile),
        grid=(b // bb, t // tt),
        in_specs=[
            _x_spec(bb, tt),
            _mod_spec(layer, 3, bb, row0),
            _mod_spec(layer, 4, bb, row0),
            _ng_spec(layer, 1),
            _resident((None, d, 3 * d), lambda b_, t_: (widx, 0, 0)),
            _resident((ATT_GW, ATT_GW), lambda b_, t_: (0, 0)),
            vec, vec,
        ],
        out_specs=[_x_spec(bb, tt)] * 3 + [keep_spec] * 2,
        out_shape=[b16o, b16o, b16o, f32k, f32k],
        compiler_params=_cparams(2),
        name=f"att_proj_l{layer}",
    )(x, mod, mod, norm_g4, w_qkv, pavg, qg, kg)


def _head_block_diag():
    rb = lax.broadcasted_iota(jnp.int32, (ATT_GW, ATT_GW), 0) // ATT_HD
    cb = lax.broadcasted_iota(jnp.int32, (ATT_GW, ATT_GW), 1) // ATT_HD
    lane_head = lax.broadcasted_iota(jnp.int32, (CHUNK, ATT_GW), 1) // ATT_HD
    return rb == cb, lane_head


def _att_chunk_group(qc, kb, vb, bias, masks, first_kpos):
    diag, lane_head = masks
    qbd = jnp.where(diag, jnp.concatenate([qc] * ATT_GROUP, axis=0), jnp.zeros((), BF16))
    s = lax.dot_general(qbd, kb, (((1,), (1,)), ((), ())), preferred_element_type=F32) + bias
    if first_kpos is not None:
        kpos = first_kpos + lax.broadcasted_iota(jnp.int32, (1, BAND), 1)
        s = jnp.where(kpos >= 0, s, NEG_INF)
    m = jnp.max(s, axis=-1, keepdims=True)
    p = jnp.exp2(s - m)
    inv = 1.0 / jnp.sum(p, axis=-1, keepdims=True)
    o = jnp.dot(p.astype(BF16), vb, preferred_element_type=F32)
    last = ATT_GROUP - 1
    og = o[last * CHUNK:] * inv[last * CHUNK:]
    for hh in range(last - 1, -1, -1):
        rows = slice(hh * CHUNK, (hh + 1) * CHUNK)
        og = jnp.where(lane_head == hh, o[rows] * inv[rows], og)
    return og


def _att_core_kernel(first_pos, has_past, *refs):
    if has_past:
        q_ref, k_ref, v_ref, pk_ref, pv_ref, bias_ref, o_ref, kbuf, vbuf = refs
    else:
        q_ref, k_ref, v_ref, bias_ref, o_ref, kbuf, vbuf = refs
    t = q_ref.shape[0]
    if has_past:
        kbuf[0:BAND_ROWS, :] = pk_ref[...].astype(BF16)
        vbuf[0:BAND_ROWS, :] = pv_ref[...].astype(BF16)
    else:
        kbuf[0:BAND_ROWS, :] = jnp.zeros((BAND_ROWS, ATT_GW), BF16)
        vbuf[0:BAND_ROWS, :] = jnp.zeros((BAND_ROWS, ATT_GW), BF16)
    kbuf[BAND_ROWS:, :] = k_ref[...]
    vbuf[BAND_ROWS:, :] = v_ref[...]
    masks = _head_block_diag()

    def make_chunk(masked):
        def chunk(n, carry):
            r0 = pl.multiple_of(n * CHUNK, CHUNK)
            og = _att_chunk_group(q_ref[pl.ds(r0, CHUNK), :], kbuf[pl.ds(r0, BAND), :],
                                  vbuf[pl.ds(r0, BAND), :], bias_ref[...], masks,
                                  first_pos + r0 if masked else None)
            o_ref[pl.ds(r0, CHUNK), :] = og.astype(BF16)
            return carry
        return chunk

    n_chunks = t // CHUNK
    n_masked = min(n_chunks, max(0, -(first_pos // CHUNK)))
    for lo, hi, masked in ((0, n_masked, True), (n_masked, n_chunks, False)):
        if hi > lo:
            trips = hi - lo
            unroll = next(u for u in (ATT_UNROLL, 2, 1) if trips % u == 0)
            lax.fori_loop(lo, hi, make_chunk(masked), 0, unroll=unroll)


def _att_core(q, k, v, past_k, past_v, bias, first_pos, name):
    b, t, d = q.shape
    has_past = past_k is not None
    grp = lambda rows: pl.BlockSpec((None, rows, ATT_GW), lambda b_, g_: (b_, 0, g_))
    in_specs = [grp(t), grp(t), grp(t)]
    args = [q, k, v]
    if has_past:
        in_specs += [grp(BAND_ROWS), grp(BAND_ROWS)]
        args += [past_k, past_v]
    in_specs.append(pl.BlockSpec((None, ATT_GW, BAND), lambda b_, g_: (g_, 0, 0)))
    args.append(bias)
    return pl.pallas_call(
        functools.partial(_att_core_kernel, first_pos, has_past),
        grid=(b, ATT_NGROUPS),
        in_specs=in_specs,
        out_specs=grp(t),
        out_shape=jax.ShapeDtypeStruct((b, t, d), BF16),
        scratch_shapes=[pltpu.VMEM((BAND_ROWS + t, ATT_GW), BF16)] * 2,
        compiler_params=_cparams(2),
        name=name,
    )(*args)


def _att_fused_kernel(first_keep_tile, x_ref, sh_ref, sc_ref, gm_ref, ng_ref, wqkv_ref, p_ref,
                      qg_ref, kg_ref, bias_ref, wo_ref, o_ref, kk_ref, vk_ref,
                      q_scr, kbuf, vbuf, a_scr):
    tt, d = x_ref.shape
    t_idx = pl.program_id(1)

    @pl.when(t_idx == 0)
    def _():
        kbuf[0:BAND_ROWS, :] = jnp.zeros((BAND_ROWS, d), BF16)
        vbuf[0:BAND_ROWS, :] = jnp.zeros((BAND_ROWS, d), BF16)

    @pl.when(t_idx > 0)
    def _():
        kbuf[0:BAND_ROWS, :] = kbuf[tt:tt + BAND_ROWS, :]
        vbuf[0:BAND_ROWS, :] = vbuf[tt:tt + BAND_ROWS, :]

    hb = _norm_mod(x_ref[...][None], ng_ref[...], sc_ref[...], sh_ref[...])[0].astype(BF16)
    q = jnp.dot(hb, wqkv_ref[:, 0:d], preferred_element_type=F32)
    q_scr[...] = (_head_norm(q, p_ref[...], qg_ref[...]) * ATT_Q_SCALE).astype(BF16)
    k = jnp.dot(hb, wqkv_ref[:, d:2 * d], preferred_element_type=F32)
    k = _head_norm(k, p_ref[...], kg_ref[...])
    kbuf[BAND_ROWS:, :] = k.astype(BF16)
    v = jnp.dot(hb, wqkv_ref[:, 2 * d:3 * d], preferred_element_type=F32)
    vbuf[BAND_ROWS:, :] = v.astype(BF16)

    @pl.when(t_idx >= first_keep_tile)
    def _():
        kk_ref[...] = k
        vk_ref[...] = v

    masks = _head_block_diag()

    def make_chunk(masked):
        def chunk(n, carry):
            r0 = pl.multiple_of(n * CHUNK, CHUNK)
            for g in range(ATT_NGROUPS):
                gsl = slice(g * ATT_GW, (g + 1) * ATT_GW)
                og = _att_chunk_group(q_scr[pl.ds(r0, CHUNK), gsl], kbuf[pl.ds(r0, BAND), gsl],
                                      vbuf[pl.ds(r0, BAND), gsl], bias_ref[g], masks,
                                      r0 - BAND_ROWS if masked else None)
                a_scr[pl.ds(r0, CHUNK), gsl] = og.astype(BF16)
            return carry
        return chunk

    @pl.when(t_idx == 0)
    def _():
        lax.fori_loop(0, tt // CHUNK, make_chunk(True), 0, unroll=2)

    @pl.when(t_idx > 0)
    def _():
        lax.fori_loop(0, tt // CHUNK, make_chunk(False), 0, unroll=2)

    out = jnp.dot(a_scr[...], wo_ref[...], preferred_element_type=F32)
    o_ref[...] = x_ref[...] + gm_ref[0] * out


def _att_fused(x, mod, norm_g4, w_qkv, w_o, pavg, qg, kg, bias, keep, layer, widx, tt, row0):
    b, t, d = x.shape
    assert tt >= BAND_ROWS and tt % CHUNK == 0 and (tt // CHUNK) % 2 == 0
    assert keep % tt == 0 and t % tt == 0
    first_keep_tile = (t - keep) // tt
    xs = pl.BlockSpec((None, tt, d), lambda b_, t_: (b_, t_, 0))
    vec = pl.BlockSpec((None, 1, d), lambda b_, t_: (widx, 0, 0))
    keep_spec = pl.BlockSpec((None, tt, d),
                             lambda b_, t_: (b_, jnp.maximum(t_ - first_keep_tile, 0), 0))
    f32k = jax.ShapeDtypeStruct((b, keep, d), F32)
    return pl.pallas_call(
        functools.partial(_att_fused_kernel, first_keep_tile),
        grid=(b, t // tt),
        in_specs=[
            xs,
            _mod_spec(layer, 3, 1, row0), _mod_spec(layer, 4, 1, row0), _mod_spec(layer, 5, 1, row0),
            _ng_spec(layer, 1),
            _resident((None, d, 3 * d), lambda b_, t_: (widx, 0, 0)),
            _resident((ATT_GW, ATT_GW), lambda b_, t_: (0, 0)),
            vec, vec,
            _resident((ATT_NGROUPS, ATT_GW, BAND), lambda b_, t_: (0, 0, 0)),
            _resident((None, d, d), lambda b_, t_: (widx, 0, 0)),
        ],
        out_specs=[xs, keep_spec, keep_spec],
        out_shape=[jax.ShapeDtypeStruct(x.shape, F32), f32k, f32k],
        scratch_shapes=[pltpu.VMEM((tt, d), BF16), pltpu.VMEM((BAND_ROWS + tt, d), BF16),
                        pltpu.VMEM((BAND_ROWS + tt, d), BF16), pltpu.VMEM((tt, d), BF16)],
        compiler_params=_cparams(2),
        name=f"att_l{layer}",
    )(x, mod, mod, mod, norm_g4, w_qkv, pavg, qg, kg, bias, w_o)


def _att_bias_kernel(rb_ref, o_ref):
    rb = rb_ref[...]
    hi = rb.astype(BF16)
    r1 = rb - hi.astype(F32)
    mid = r1.astype(BF16)
    lo = (r1 - mid.astype(F32)).astype(BF16)
    r = lax.broadcasted_iota(jnp.int32, (REL_SIZE, REL_BASE_W), 0)
    u = lax.broadcasted_iota(jnp.int32, (REL_SIZE, REL_BASE_W), 1)
    rel = jnp.clip(BAND_ROWS + CHUNK - 1 - u, REL_MIN, REL_MAX) - REL_MIN
    sel = jnp.where(r == rel, 1.0, 0.0).astype(BF16)
    base = (jnp.dot(hi, sel, preferred_element_type=F32)
            + jnp.dot(mid, sel, preferred_element_type=F32)
            + jnp.dot(lo, sel, preferred_element_type=F32)) * LOG2E
    for i in range(CHUNK):
        o_ref[i] = base[:, CHUNK - 1 - i:CHUNK - 1 - i + BAND]


def _att_bias_table(rel_bias):
    tab = pl.pallas_call(
        _att_bias_kernel,
        out_shape=jax.ShapeDtypeStruct((CHUNK, ATT_HEADS, BAND), F32),
        name="att_bias",
    )(rel_bias)
    return tab.transpose(1, 0, 2).reshape(ATT_NGROUPS, ATT_GW, BAND)


def _rope_tables(pos0, t):
    inv = ROPE_BASE ** (-np.arange(ROPE_HALF, dtype=np.float64) / ROPE_HALF)
    ang = (pos0 + np.arange(t, dtype=np.float64))[:, None] * inv[None, :]
    return jnp.asarray(np.cos(ang), F32), jnp.asarray(np.sin(ang), F32)


def _trunk(x, mod, row0, pos0, bb, tt, ret_lc, ret_states, pool_bufs, att_k_cache, att_v_cache, p):
    b, t, d = x.shape
    new_pool, new_k, new_v = [], [], []
    ret_all = None
    cos, sin = _rope_tables(pos0, t)
    tt_ffn = FFN_ROWS if (bb == 1 and t % FFN_ROWS == 0) else tt
    bb_ffn = bb
    if tt == t and FFN_ROWS % t == 0:
        wide = min(b, FFN_ROWS // t)
        if b % wide == 0 and row0 % wide == 0:
            bb_ffn = wide
    for i in range(DEPTH):
        x = _ffn(x, mod, p['norm_g'], p['ffn1_w_in'], p['ffn1_w_out'], i, 1, bb_ffn, tt_ffn, row0)
        j = i // N_MIXERS
        kind = i % N_MIXERS
        if kind == 0:
            gn = p['ret_gn_g'][j].reshape(1, 2 * d)
            if ret_states is None:
                x, ret_all = _ret_fused(x, mod, p['norm_g'], p['ret_w_in'], p['ret_w_out'], gn,
                                        cos, sin, ret_all, ret_lc, max(tt, ret_lc),
                                        i, j, row0)
            else:
                q, k, v, g = _ret_proj(x, mod, p['norm_g'], p['ret_w_in'], cos, sin, ret_lc,
                                       i, j, bb, tt, row0)
                y, ret_all = _ret_core(q, k, v, g, ret_states, ret_all, j, gn, ret_lc,
                                       f"ret_core_l{i}")
                x = _matres(y, x, mod, p['ret_w_out'], i, j, bb, tt, row0, f"ret_out_l{i}")
        elif kind == 1:
            if pool_bufs is None:
                buf16 = jnp.zeros((b, POOL_CARRY, d), F32)
            else:
                buf16 = jnp.pad(pool_bufs[j], ((0, 0), (POOL_CARRY - POOL_BUF, 0), (0, 0)))
            x, nb = _pool(x, mod, p['norm_g'], buf16, p['pool_w'], p['pool_b'], p['pool_scale'],
                          i, j, min(tt, t), row0, pos0)
            new_pool.append(nb)
        else:
            keep = min(BAND_ROWS, t)
            if att_k_cache is None and pos0 == 0 and tt >= BAND_ROWS:
                x, k_keep, v_keep = _att_fused(
                    x, mod, p['norm_g'], p['att_w_qkv'], p['att_w_o'], p['att_pavg'],
                    p['att_q_g'], p['att_k_g'], p['att_bias'][j], keep, i, j, tt, row0)
            else:
                q, kb, vb, k_keep, v_keep = _att_proj(
                    x, mod, p['norm_g'], p['att_w_qkv'], p['att_pavg'], p['att_q_g'],
                    p['att_k_g'], keep, i, j, bb, tt, row0)
                if att_k_cache is None:
                    past_k = past_v = None
                else:
                    past_k = att_k_cache[j].reshape(b, BAND_ROWS, d)
                    past_v = att_v_cache[j].reshape(b, BAND_ROWS, d)
                o = _att_core(q, kb, vb, past_k, past_v, p['att_bias'][j], pos0 - BAND_ROWS,
                              f"att_core_l{i}")
                x = _matres(o, x, mod, p['att_w_o'], i, j, bb, tt, row0, f"att_out_l{i}")
            new_k.append(k_keep.reshape(b, keep, ATT_HEADS, ATT_HD))
            new_v.append(v_keep.reshape(b, keep, ATT_HEADS, ATT_HD))
        x = _ffn(x, mod, p['norm_g'], p['ffn2_w_in'], p['ffn2_w_out'], i, 2, bb_ffn, tt_ffn, row0)
    return x, ret_all, jnp.stack(new_pool), jnp.stack(new_k), jnp.stack(new_v)


def kernel(x_prompt, x_sample, c_prompt, c_sample, state_ret, state_pool, cache_att_k, cache_att_v, norm_g, ada_w, ada_b, ffn1_w_in, ffn1_w_out, ffn2_w_in, ffn2_w_out, ret_w_in, ret_w_out, ret_gn_g, pool_w, pool_b, pool_scale, att_w_qkv, att_w_o, att_q_g, att_k_g, att_rel_bias):
    assert cache_att_k.shape[2] == BAND_ROWS, "key cache must hold exactly the left band"
    bp, tp, d = x_prompt.shape
    bs, ts, _ = x_sample.shape
    n_att = att_w_qkv.shape[0]
    head_id = np.arange(ATT_GW) // ATT_HD
    pavg = jnp.asarray((head_id[:, None] == head_id[None, :]) / ATT_HD, BF16)
    p = {
        'norm_g': norm_g.reshape(DEPTH, 3, 1, d),
        'ffn1_w_in': _interleave_gate_up(ffn1_w_in), 'ffn1_w_out': ffn1_w_out.astype(BF16),
        'ffn2_w_in': _interleave_gate_up(ffn2_w_in), 'ffn2_w_out': ffn2_w_out.astype(BF16),
        'ret_w_in': ret_w_in.astype(BF16), 'ret_w_out': ret_w_out.astype(BF16),
        'ret_gn_g': ret_gn_g,
        'pool_w': pool_w.astype(BF16),
        'pool_b': pool_b.reshape(-1, 1, d), 'pool_scale': pool_scale.reshape(-1, 1, d),
        'att_w_qkv': att_w_qkv.astype(BF16), 'att_w_o': att_w_o.astype(BF16),
        'att_q_g': jnp.tile(att_q_g, (1, ATT_HEADS)).reshape(n_att, 1, d),
        'att_k_g': jnp.tile(att_k_g, (1, ATT_HEADS)).reshape(n_att, 1, d),
        'att_pavg': pavg,
        'att_bias': [_att_bias_table(att_rel_bias[j]) for j in range(n_att)],
    }
    mod = _ada(jnp.concatenate([c_prompt, c_sample], axis=0), ada_w, ada_b)

    tt_p = min(PROMPT_ROWS, tp)
    y_prompt, ret_p, pool_p, k_p, v_p = _trunk(
        x_prompt, mod, 0, 0, 1, tt_p, min(RET_CHUNK_PROMPT, tp), None, None, None, None, p)
    bb_s = max(1, min(bs, PROMPT_ROWS // ts))
    y_sample, ret_s, pool_s, k_s, v_s = _trunk(
        x_sample, mod, bp, PAST_LEN, bb_s, ts, ts, state_ret, state_pool,
        cache_att_k, cache_att_v, p)
    return (y_prompt, y_sample, ret_p, ret_s, pool_p, pool_s, k_p, v_p, k_s, v_s)
```

```python
import functools

import numpy as np
import jax
import jax.numpy as jnp
from jax import lax
from jax.experimental import pallas as pl
from jax.experimental.pallas import tpu as pltpu

F32 = jnp.float32
BF16 = jnp.bfloat16

LANES = 128
D_MODEL = 1024
DEPTH = 4
D_FF = 2816
EPS = 1e-6
CHUNK = 64
PAST_LEN = 2048
N_MIXERS = 3
RET_HEADS = 4
RET_DK = D_MODEL // RET_HEADS
RET_DV = 2 * D_MODEL // RET_HEADS
ROPE_BASE = 10000.0
ROPE_HALF = RET_DK // 2
POOL_WINDOWS = (2, 4, 8, 16)
POOL_GROUPS = 4
POOL_GC = D_MODEL // POOL_GROUPS
POOL_BUF = 15
POOL_CARRY = 16
ATT_HEADS = 16
ATT_HD = D_MODEL // ATT_HEADS
ATT_GROUP = 4
ATT_GW = ATT_GROUP * ATT_HD
ATT_NGROUPS = ATT_HEADS // ATT_GROUP
LEFT_CHUNKS = 8
BAND_ROWS = LEFT_CHUNKS * CHUNK
BAND = BAND_ROWS + CHUNK
REL_MIN = -(CHUNK - 1)
REL_MAX = 256
REL_SIZE = REL_MAX - REL_MIN + 1
REL_BASE_W = 640
NEG_INF = -1e30
LOG2E = 1.4426950408889634
ATT_Q_SCALE = ATT_HD ** -0.5 * LOG2E
N_MOD = 9

VMEM_LIMIT_BYTES = 56 * 1024 * 1024
PROMPT_ROWS = 512
FFN_ROWS = 1024
RET_CHUNK_PROMPT = 256
ATT_UNROLL = 8


def _cparams(n_grid):
    return pltpu.CompilerParams(
        dimension_semantics=("arbitrary",) * n_grid,
        vmem_limit_bytes=VMEM_LIMIT_BYTES,
    )


def _resident(block_shape, index_map):
    return pl.BlockSpec(block_shape, index_map, pipeline_mode=pl.Buffered(1))


def _norm_mod(x, ng, sc, sh):
    ms = jnp.mean(x * x, axis=-1, keepdims=True)
    y = x * lax.rsqrt(ms + EPS) * ng
    return y * (1.0 + sc) + sh


def _silu(x):
    return x * (1.0 / (1.0 + jnp.exp(-x)))


def _ada_kernel(c_ref, w_ref, b_ref, o_ref):
    c = _silu(c_ref[...]).astype(BF16)
    w = w_ref[...].astype(BF16)
    o_ref[...] = jnp.dot(c, w, preferred_element_type=F32) + b_ref[...]


def _ada(c_all, ada_w, ada_b):
    n = c_all.shape[0]
    b4 = ada_b.reshape(DEPTH, N_MOD, 1, D_MODEL)
    out = pl.pallas_call(
        _ada_kernel,
        grid=(DEPTH, N_MOD),
        in_specs=[
            pl.BlockSpec((n, D_MODEL), lambda i, j: (0, 0)),
            pl.BlockSpec((None, D_MODEL, D_MODEL), lambda i, j: (i, 0, j)),
            pl.BlockSpec((None, None, 1, D_MODEL), lambda i, j: (i, j, 0, 0)),
        ],
        out_specs=pl.BlockSpec((None, None, n, D_MODEL), lambda i, j: (i, j, 0, 0)),
        out_shape=jax.ShapeDtypeStruct((DEPTH, N_MOD, n, D_MODEL), F32),
        compiler_params=_cparams(2),
        name="ada_mod",
    )(c_all, ada_w, b4)
    return out.reshape(DEPTH, N_MOD, n, 1, D_MODEL)


def _mod_spec(layer, j, bb, row0):
    blk0 = row0 // bb
    return pl.BlockSpec((None, None, bb, 1, D_MODEL),
                        lambda b, t: (layer, j, blk0 + b, 0, 0))


def _ng_spec(layer, j):
    return pl.BlockSpec((None, None, 1, D_MODEL), lambda b, t: (layer, j, 0, 0))


def _x_spec(bb, tt, width=D_MODEL):
    return pl.BlockSpec((bb, tt, width), lambda b, t: (b, t, 0))


GATE_UP_TILES = 2


def _gate_up_kernel(g_ref, u_ref, o_ref):
    for i in range(GATE_UP_TILES):
        src = slice(i * LANES, (i + 1) * LANES)
        o_ref[:, 2 * i * LANES:(2 * i + 1) * LANES] = g_ref[:, src].astype(BF16)
        o_ref[:, (2 * i + 1) * LANES:(2 * i + 2) * LANES] = u_ref[:, src].astype(BF16)


def _interleave_gate_up(w_in):
    nl, d, _ = w_in.shape
    width = GATE_UP_TILES * LANES
    steps = D_FF // width
    assert D_FF % width == 0
    return pl.pallas_call(
        _gate_up_kernel,
        grid=(nl, steps),
        in_specs=[pl.BlockSpec((None, d, width), lambda l, j: (l, 0, j)),
                  pl.BlockSpec((None, d, width), lambda l, j: (l, 0, steps + j))],
        out_specs=pl.BlockSpec((None, d, 2 * width), lambda l, j: (l, 0, j)),
        out_shape=jax.ShapeDtypeStruct((nl, d, 2 * D_FF), BF16),
        compiler_params=_cparams(2),
        name="ffn_gate_up_layout",
    )(w_in, w_in)


def _swiglu(hb, win_ref, wout_ref):
    gu = jnp.dot(hb, win_ref[...], preferred_element_type=F32)
    act = jnp.concatenate(
        [_silu(gu[:, 2 * j * LANES:(2 * j + 1) * LANES]) * gu[:, (2 * j + 1) * LANES:(2 * j + 2) * LANES]
         for j in range(D_FF // LANES)], axis=1).astype(BF16)
    return jnp.dot(act, wout_ref[...], preferred_element_type=F32)


def _ffn_kernel(x_ref, sh_ref, sc_ref, g_ref, ng_ref, win_ref, wout_ref, o_ref):
    bb, tt, d = x_ref.shape
    x = x_ref[...]
    h = _norm_mod(x, ng_ref[...], sc_ref[...], sh_ref[...])
    y = _swiglu(h.reshape(bb * tt, d).astype(BF16), win_ref, wout_ref)
    o_ref[...] = x + (0.5 * g_ref[...]) * y.reshape(bb, tt, d)


def _ffn(x, mod, norm_g4, w_in, w_out, layer, which, bb, tt, row0):
    b, t, d = x.shape
    j0 = 0 if which == 1 else 6
    nj = 0 if which == 1 else 2
    return pl.pallas_call(
        _ffn_kernel,
        grid=(b // bb, t // tt),
        in_specs=[
            _x_spec(bb, tt),
            _mod_spec(layer, j0, bb, row0),
            _mod_spec(layer, j0 + 1, bb, row0),
            _mod_spec(layer, j0 + 2, bb, row0),
            _ng_spec(layer, nj),
            _resident((None, D_MODEL, 2 * D_FF), lambda b_, t_: (layer, 0, 0)),
            _resident((None, D_FF, D_MODEL), lambda b_, t_: (layer, 0, 0)),
        ],
        out_specs=_x_spec(bb, tt),
        out_shape=jax.ShapeDtypeStruct(x.shape, F32),
        compiler_params=_cparams(2),
        name=f"ffn{which}_l{layer}",
    )(x, mod, mod, mod, norm_g4, w_in, w_out)


def _matres_kernel(a_ref, x_ref, g_ref, w_ref, o_ref):
    bb, tt, k = a_ref.shape
    y = jnp.dot(a_ref[...].reshape(bb * tt, k), w_ref[...], preferred_element_type=F32)
    o_ref[...] = x_ref[...] + g_ref[...] * y.reshape(bb, tt, D_MODEL)


def _matres(a, x, mod, w, layer, widx, bb, tt, row0, name):
    b, t, k = a.shape
    return pl.pallas_call(
        _matres_kernel,
        grid=(b // bb, t // tt),
        in_specs=[
            _x_spec(bb, tt, k),
            _x_spec(bb, tt),
            _mod_spec(layer, 5, bb, row0),
            _resident((None, k, D_MODEL), lambda b_, t_: (widx, 0, 0)),
        ],
        out_specs=_x_spec(bb, tt),
        out_shape=jax.ShapeDtypeStruct(x.shape, F32),
        compiler_params=_cparams(2),
        name=name,
    )(a, x, mod, w)


def _ret_proj_kernel(x_ref, sh_ref, sc_ref, ng_ref, w_ref, cos_ref, sin_ref, qs_ref, ks_ref,
                     q_ref, k_ref, v_ref, g_ref):
    bb, tt, d = x_ref.shape
    h = _norm_mod(x_ref[...], ng_ref[...], sc_ref[...], sh_ref[...])
    hb = h.reshape(bb * tt, d).astype(BF16)
    cos = cos_ref[...][None]
    sin = sin_ref[...][None]
    for idx, (o_ref, s_ref) in enumerate(((q_ref, qs_ref), (k_ref, ks_ref))):
        y = jnp.dot(hb, w_ref[:, idx * d:(idx + 1) * d], preferred_element_type=F32)
        y = y.reshape(bb, tt, d)
        for hd in range(RET_HEADS):
            lo = hd * RET_DK
            mid = lo + ROPE_HALF
            x1 = y[:, :, lo:mid]
            x2 = y[:, :, mid:lo + RET_DK]
            o_ref[:, :, lo:mid] = ((x1 * cos - x2 * sin) * s_ref[:, lo:mid][None]).astype(BF16)
            o_ref[:, :, mid:lo + RET_DK] = (
                (x1 * sin + x2 * cos) * s_ref[:, mid:lo + RET_DK][None]).astype(BF16)
    v = jnp.dot(hb, w_ref[:, 2 * d:4 * d], preferred_element_type=F32)
    v_ref[...] = v.reshape(bb, tt, 2 * d).astype(BF16)
    g = jnp.dot(hb, w_ref[:, 4 * d:6 * d], preferred_element_type=F32)
    g_ref[...] = _silu(g).reshape(bb, tt, 2 * d).astype(BF16)


def _ret_gamma():
    return 1.0 - 2.0 ** (-5.0 - np.arange(RET_HEADS, dtype=np.float64))


def _ret_scale_tables(tt, lc):
    gamma = _ret_gamma()
    i = (np.arange(tt) % lc).astype(np.float64)[:, None]
    g_lane = np.repeat(gamma, RET_DK)[None, :]
    qs = g_lane ** (i + 1.0)
    ks = (RET_DK ** -0.5) * g_lane ** (-(i + 1.0))
    return jnp.asarray(qs, F32), jnp.asarray(ks, F32)


def _ret_proj(x, mod, norm_g4, w_in, cos, sin, lc, layer, widx, bb, tt, row0):
    b, t, d = x.shape
    assert tt % lc == 0
    qs, ks = _ret_scale_tables(tt, lc)
    tab = pl.BlockSpec((tt, ROPE_HALF), lambda b_, t_: (t_, 0))
    scale = _resident((tt, d), lambda b_, t_: (0, 0))
    return pl.pallas_call(
        _ret_proj_kernel,
        grid=(b // bb, t // tt),
        in_specs=[
            _x_spec(bb, tt),
            _mod_spec(layer, 3, bb, row0),
            _mod_spec(layer, 4, bb, row0),
            _ng_spec(layer, 1),
            _resident((None, D_MODEL, 6 * D_MODEL), lambda b_, t_: (widx, 0, 0)),
            tab, tab, scale, scale,
        ],
        out_specs=[_x_spec(bb, tt), _x_spec(bb, tt),
                   _x_spec(bb, tt, 2 * d), _x_spec(bb, tt, 2 * d)],
        out_shape=[jax.ShapeDtypeStruct((b, t, d), BF16), jax.ShapeDtypeStruct((b, t, d), BF16),
                   jax.ShapeDtypeStruct((b, t, 2 * d), BF16), jax.ShapeDtypeStruct((b, t, 2 * d), BF16)],
        compiler_params=_cparams(2),
        name=f"ret_proj_l{layer}",
    )(x, mod, mod, norm_g4, w_in, cos, sin, qs, ks)


def _ret_head(qh, kh, vh, s, tril, gn, cdec_h):
    inner = lax.dot_general(qh, kh, (((1,), (1,)), ((), ())), preferred_element_type=F32) * tril
    o = (jnp.dot(inner.astype(BF16), vh, preferred_element_type=F32)
         + jnp.dot(qh, s.astype(BF16), preferred_element_type=F32))
    s_new = (s + lax.dot_general(kh, vh, (((0,), (0,)), ((), ())),
                                 preferred_element_type=F32)) * cdec_h
    mu = jnp.mean(o, axis=-1, keepdims=True)
    oc = o - mu
    var = jnp.mean(oc * oc, axis=-1, keepdims=True)
    return oc * lax.rsqrt(var + EPS) * gn, s_new


def _init_state_stack(s_ref, slot, prev_ref, s0_ref):
    if slot:
        s_ref[0:slot] = prev_ref[...]
    if s0_ref is None:
        s_ref[slot] = jnp.zeros(s_ref.shape[1:], F32)
    else:
        s_ref[slot] = s0_ref[...]


def _ret_core_kernel(cdec, has_state, slot, *refs):
    refs = list(refs)
    q_ref, k_ref, v_ref, g_ref = refs[:4]
    s0_ref = refs[4] if has_state else None
    tril_ref, gn_ref = refs[4 + has_state:6 + has_state]
    prev_ref = refs[6 + has_state] if slot else None
    y_ref, s_ref = refs[6 + has_state + bool(slot):]

    @pl.when(pl.program_id(1) == 0)
    def _():
        _init_state_stack(s_ref, slot, prev_ref, s0_ref)

    for hd in range(RET_HEADS):
        ksl = slice(hd * RET_DK, (hd + 1) * RET_DK)
        vsl = slice(hd * RET_DV, (hd + 1) * RET_DV)
        on, s_new = _ret_head(q_ref[:, ksl], k_ref[:, ksl], v_ref[:, vsl], s_ref[slot, hd],
                              tril_ref[...], gn_ref[:, vsl], cdec[hd])
        s_ref[slot, hd] = s_new
        y_ref[:, vsl] = (g_ref[:, vsl].astype(F32) * on).astype(BF16)


def _state_stack_specs(slot, b):
    blk = lambda n: pl.BlockSpec((n, None, RET_HEADS, RET_DK, RET_DV),
                                 lambda b_, t_: (0, b_, 0, 0, 0))
    shape = jax.ShapeDtypeStruct((slot + 1, b, RET_HEADS, RET_DK, RET_DV), F32)
    return blk(slot + 1), shape, (blk(slot) if slot else None)


def _ret_core(q, k, v, g, s0_all, s_prev, slot, gn_g, lc, name):
    b, t, d = q.shape
    cdec = tuple(float(c) for c in _ret_gamma() ** float(lc))
    tril = jnp.asarray(np.tril(np.ones((lc, lc))), F32)
    has_state = s0_all is not None
    row = lambda width: pl.BlockSpec((None, lc, width), lambda b_, t_: (b_, t_, 0))
    s_spec, s_shape, prev_spec = _state_stack_specs(slot, b)
    in_specs = [row(d), row(d), row(2 * d), row(2 * d)]
    args = [q, k, v, g]
    if has_state:
        in_specs.append(pl.BlockSpec((None, None, RET_HEADS, RET_DK, RET_DV),
                                     lambda b_, t_: (slot, b_, 0, 0, 0)))
        args.append(s0_all)
    in_specs += [_resident((lc, lc), lambda b_, t_: (0, 0)),
                 pl.BlockSpec((1, 2 * d), lambda b_, t_: (0, 0))]
    args += [tril, gn_g]
    if slot:
        in_specs.append(prev_spec)
        args.append(s_prev)
    return pl.pallas_call(
        functools.partial(_ret_core_kernel, cdec, has_state, slot),
        grid=(b, t // lc),
        in_specs=in_specs,
        out_specs=[row(2 * d), s_spec],
        out_shape=[jax.ShapeDtypeStruct((b, t, 2 * d), BF16), s_shape],
        compiler_params=_cparams(2),
        name=name,
    )(*args)


def _ret_fused_kernel(cdec, lc, slot, *refs):
    refs = list(refs)
    (x_ref, sh_ref, sc_ref, gm_ref, ng_ref, win_ref, wout_ref, cos_ref, sin_ref, qs_ref, ks_ref,
     tril_ref, gn_ref) = refs[:13]
    prev_ref = refs[13] if slot else None
    o_ref, s_ref, q_scr, k_scr, v_scr, y_scr = refs[13 + bool(slot):]
    tt, d = x_ref.shape

    @pl.when(pl.program_id(1) == 0)
    def _():
        _init_state_stack(s_ref, slot, prev_ref, None)

    hb = _norm_mod(x_ref[...][None], ng_ref[...], sc_ref[...], sh_ref[...])[0].astype(BF16)
    cos = cos_ref[...]
    sin = sin_ref[...]
    for idx, (scr, tab) in enumerate(((q_scr, qs_ref), (k_scr, ks_ref))):
        y = jnp.dot(hb, win_ref[:, idx * d:(idx + 1) * d], preferred_element_type=F32)
        for hd in range(RET_HEADS):
            lo = hd * RET_DK
            mid = lo + ROPE_HALF
            hi = lo + RET_DK
            x1 = y[:, lo:mid]
            x2 = y[:, mid:hi]
            scr[:, lo:mid] = ((x1 * cos - x2 * sin) * tab[:, lo:mid]).astype(BF16)
            scr[:, mid:hi] = ((x1 * sin + x2 * cos) * tab[:, mid:hi]).astype(BF16)
    v_scr[...] = jnp.dot(hb, win_ref[:, 2 * d:4 * d], preferred_element_type=F32).astype(BF16)
    for hd in range(RET_HEADS):
        ksl = slice(hd * RET_DK, (hd + 1) * RET_DK)
        vsl = slice(hd * RET_DV, (hd + 1) * RET_DV)
        g = jnp.dot(hb, win_ref[:, 4 * d + hd * RET_DV:4 * d + (hd + 1) * RET_DV],
                    preferred_element_type=F32)
        for c in range(tt // lc):
            rows = slice(c * lc, (c + 1) * lc)
            on, s_new = _ret_head(q_scr[rows, ksl], k_scr[rows, ksl], v_scr[rows, vsl],
                                  s_ref[slot, hd], tril_ref[...], gn_ref[:, vsl], cdec[hd])
            s_ref[slot, hd] = s_new
            y_scr[rows, vsl] = (_silu(g[rows]) * on).astype(BF16)
    out = jnp.dot(y_scr[...], wout_ref[...], preferred_element_type=F32)
    o_ref[...] = x_ref[...] + gm_ref[0] * out


def _ret_fused(x, mod, norm_g4, w_in, w_out, gn_g, cos, sin, s_prev, lc, tt, layer, widx, row0):
    b, t, d = x.shape
    assert tt % lc == 0 and t % tt == 0
    cdec = tuple(float(c) for c in _ret_gamma() ** float(lc))
    tril = jnp.asarray(np.tril(np.ones((lc, lc))), F32)
    qs, ks = _ret_scale_tables(tt, lc)
    slot = widx
    xs = pl.BlockSpec((None, tt, d), lambda b_, t_: (b_, t_, 0))
    tab = pl.BlockSpec((tt, ROPE_HALF), lambda b_, t_: (t_, 0))
    const2 = lambda shape: _resident(shape, lambda b_, t_: (0, 0))
    s_spec, s_shape, prev_spec = _state_stack_specs(slot, b)
    in_specs = [
        xs,
        _mod_spec(layer, 3, 1, row0), _mod_spec(layer, 4, 1, row0), _mod_spec(layer, 5, 1, row0),
        _ng_spec(layer, 1),
        _resident((None, d, 6 * d), lambda b_, t_: (widx, 0, 0)),
        _resident((None, 2 * d, d), lambda b_, t_: (widx, 0, 0)),
        tab, tab, const2((tt, d)), const2((tt, d)), const2((lc, lc)),
        pl.BlockSpec((1, 2 * d), lambda b_, t_: (0, 0)),
    ]
    args = [x, mod, mod, mod, norm_g4, w_in, w_out, cos, sin, qs, ks, tril, gn_g]
    if slot:
        in_specs.append(prev_spec)
        args.append(s_prev)
    return pl.pallas_call(
        functools.partial(_ret_fused_kernel, cdec, lc, slot),
        grid=(b, t // tt),
        in_specs=in_specs,
        out_specs=[xs, s_spec],
        out_shape=[jax.ShapeDtypeStruct(x.shape, F32), s_shape],
        scratch_shapes=[pltpu.VMEM((tt, d), BF16), pltpu.VMEM((tt, d), BF16),
                        pltpu.VMEM((tt, 2 * d), BF16), pltpu.VMEM((tt, 2 * d), BF16)],
        compiler_params=_cparams(2),
        name=f"ret_l{layer}",
    )(*args)


def _pool_mix(pos0, x, sh_ref, sc_ref, gm_ref, ng_ref, buf_ref, w_ref, pb_ref, ps_ref,
              nb_ref, carry_ref):
    tt, d = x.shape
    t_idx = pl.program_id(1)

    @pl.when(t_idx == 0)
    def _():
        carry_ref[...] = buf_ref[...]

    h = _norm_mod(x[None], ng_ref[...], sc_ref[...], sh_ref[...])[0]
    ext = jnp.concatenate([carry_ref[...], h], axis=0)
    carry_ref[...] = h[tt - POOL_CARRY:, :]
    nb_ref[...] = h[tt - POOL_BUF:, :]

    pos = pos0 + t_idx * tt + lax.broadcasted_iota(jnp.int32, (tt, 1), 0)
    run = ext
    w = 1
    outs = []
    for gi, win in enumerate(POOL_WINDOWS):
        while w < win:
            n = run.shape[0]
            run = run[w:, :] + run[:n - w, :]
            w *= 2
        wsum = run[run.shape[0] - tt:, :POOL_GC]
        inv_cnt = 1.0 / jnp.minimum(pos + 1, win).astype(F32)
        lo = gi * POOL_GC
        pooled = wsum * inv_cnt - h[:, lo:lo + POOL_GC]
        outs.append(jnp.dot(pooled.astype(BF16), w_ref[gi], preferred_element_type=F32))
        run = run[:, POOL_GC:]
    y = (jnp.concatenate(outs, axis=1) + pb_ref[...]) * ps_ref[...]
    return x + gm_ref[0] * y


def _pool_kernel(pos0, x_ref, sh_ref, sc_ref, gm_ref, ng_ref, buf_ref, w_ref, pb_ref, ps_ref,
                 o_ref, nb_ref, carry_ref):
    o_ref[...] = _pool_mix(pos0, x_ref[...], sh_ref, sc_ref, gm_ref, ng_ref, buf_ref, w_ref,
                           pb_ref, ps_ref, nb_ref, carry_ref)


def _pool_ffn_kernel(pos0, x_ref, sh_ref, sc_ref, gm_ref, ng_ref, buf_ref, w_ref, pb_ref, ps_ref,
                     fsh_ref, fsc_ref, fg_ref, fng_ref, win_ref, wout_ref,
                     o_ref, nb_ref, carry_ref):
    x = _pool_mix(pos0, x_ref[...], sh_ref, sc_ref, gm_ref, ng_ref, buf_ref, w_ref,
                  pb_ref, ps_ref, nb_ref, carry_ref)
    hb = _norm_mod(x[None], fng_ref[...], fsc_ref[...], fsh_ref[...])[0].astype(BF16)
    o_ref[...] = x + (0.5 * fg_ref[0]) * _swiglu(hb, win_ref, wout_ref)


def _pool_specs(x, layer, widx, tt, row0):
    b, t, d = x.shape
    xs = pl.BlockSpec((None, tt, d), lambda b_, t_: (b_, t_, 0))
    vec = pl.BlockSpec((None, 1, d), lambda b_, t_: (widx, 0, 0))
    in_specs = [
        xs,
        _mod_spec(layer, 3, 1, row0),
        _mod_spec(layer, 4, 1, row0),
        _mod_spec(layer, 5, 1, row0),
        _ng_spec(layer, 1),
        pl.BlockSpec((None, POOL_CARRY, d), lambda b_, t_: (b_, 0, 0)),
        pl.BlockSpec((None, POOL_GROUPS, POOL_GC, POOL_GC), lambda b_, t_: (widx, 0, 0, 0)),
        vec, vec,
    ]
    out_specs = [xs, pl.BlockSpec((None, POOL_BUF, d), lambda b_, t_: (b_, 0, 0))]
    out_shape = [jax.ShapeDtypeStruct(x.shape, F32), jax.ShapeDtypeStruct((b, POOL_BUF, d), F32)]
    return in_specs, out_specs, out_shape


def _pool(x, mod, norm_g4, buf16, w, pb, ps, layer, widx, tt, row0, pos0):
    b, t, d = x.shape
    in_specs, out_specs, out_shape = _pool_specs(x, layer, widx, tt, row0)
    return pl.pallas_call(
        functools.partial(_pool_kernel, pos0),
        grid=(b, t // tt),
        in_specs=in_specs,
        out_specs=out_specs,
        out_shape=out_shape,
        scratch_shapes=[pltpu.VMEM((POOL_CARRY, d), F32)],
        compiler_params=_cparams(2),
        name=f"pool_l{layer}",
    )(x, mod, mod, mod, norm_g4, buf16, w, pb, ps)


def _pool_ffn(x, mod, norm_g4, buf16, w, pb, ps, w_in, w_out, layer, widx, tt, row0, pos0):
    b, t, d = x.shape
    in_specs, out_specs, out_shape = _pool_specs(x, layer, widx, tt, row0)
    in_specs += [
        _mod_spec(layer, 6, 1, row0), _mod_spec(layer, 7, 1, row0), _mod_spec(layer, 8, 1, row0),
        _ng_spec(layer, 2),
        _resident((None, D_MODEL, 2 * D_FF), lambda b_, t_: (layer, 0, 0)),
        _resident((None, D_FF, D_MODEL), lambda b_, t_: (layer, 0, 0)),
    ]
    return pl.pallas_call(
        functools.partial(_pool_ffn_kernel, pos0),
        grid=(b, t // tt),
        in_specs=in_specs,
        out_specs=out_specs,
        out_shape=out_shape,
        scratch_shapes=[pltpu.VMEM((POOL_CARRY, d), F32)],
        compiler_params=_cparams(2),
        name=f"pool_ffn2_l{layer}",
    )(x, mod, mod, mod, norm_g4, buf16, w, pb, ps, mod, mod, mod, norm_g4, w_in, w_out)


def _head_norm(y, pavg, gain):
    y2 = (y * y).astype(BF16)
    ms = jnp.concatenate(
        [jnp.dot(y2[:, g * ATT_GW:(g + 1) * ATT_GW], pavg, preferred_element_type=F32)
         for g in range(ATT_NGROUPS)], axis=1)
    return y * lax.rsqrt(ms + EPS) * gain


def _att_proj_kernel(first_keep_tile, x_ref, sh_ref, sc_ref, ng_ref, w_ref, p_ref, qg_ref, kg_ref,
                     q_ref, kb_ref, vb_ref, kk_ref, vk_ref):
    bb, tt, d = x_ref.shape
    h = _norm_mod(x_ref[...], ng_ref[...], sc_ref[...], sh_ref[...])
    hb = h.reshape(bb * tt, d).astype(BF16)
    q = jnp.dot(hb, w_ref[:, 0:d], preferred_element_type=F32)
    q = _head_norm(q, p_ref[...], qg_ref[...]) * ATT_Q_SCALE
    q_ref[...] = q.reshape(bb, tt, d).astype(BF16)
    k = jnp.dot(hb, w_ref[:, d:2 * d], preferred_element_type=F32)
    k = _head_norm(k, p_ref[...], kg_ref[...]).reshape(bb, tt, d)
    kb_ref[...] = k.astype(BF16)
    v = jnp.dot(hb, w_ref[:, 2 * d:3 * d], preferred_element_type=F32).reshape(bb, tt, d)
    vb_ref[...] = v.astype(BF16)

    @pl.when(pl.program_id(1) >= first_keep_tile)
    def _():
        kk_ref[...] = k
        vk_ref[...] = v


def _att_proj(x, mod, norm_g4, w_qkv, pavg, qg, kg, keep, layer, widx, bb, tt, row0):
    b, t, d = x.shape
    assert keep % tt == 0 and t % tt == 0
    first_keep_tile = (t - keep) // tt
    vec = pl.BlockSpec((None, 1, d), lambda b_, t_: (widx, 0, 0))
    keep_spec = pl.BlockSpec((bb, tt, d),
                             lambda b_, t_: (b_, jnp.maximum(t_ - first_keep_tile, 0), 0))
    b16o = jax.ShapeDtypeStruct((b, t, d), BF16)
    f32k = jax.ShapeDtypeStruct((b, keep, d), F32)
    return pl.pallas_call(
        functools.partial(_att_proj_kernel, first_keep_tile),
        grid=(b // bb, t // tt),
        in_specs=[
            _x_spec(bb, tt),
            _mod_spec(layer, 3, bb, row0),
            _mod_spec(layer, 4, bb, row0),
            _ng_spec(layer, 1),
            _resident((None, d, 3 * d), lambda b_, t_: (widx, 0, 0)),
            _resident((ATT_GW, ATT_GW), lambda b_, t_: (0, 0)),
            vec, vec,
        ],
        out_specs=[_x_spec(bb, tt)] * 3 + [keep_spec] * 2,
        out_shape=[b16o, b16o, b16o, f32k, f32k],
        compiler_params=_cparams(2),
        name=f"att_proj_l{layer}",
    )(x, mod, mod, norm_g4, w_qkv, pavg, qg, kg)


def _head_block_diag():
    rb = lax.broadcasted_iota(jnp.int32, (ATT_GW, ATT_GW), 0) // ATT_HD
    cb = lax.broadcasted_iota(jnp.int32, (ATT_GW, ATT_GW), 1) // ATT_HD
    lane_head = lax.broadcasted_iota(jnp.int32, (CHUNK, ATT_GW), 1) // ATT_HD
    return rb == cb, lane_head


def _att_chunk_group(qc, kb, vb, bias, masks, first_kpos):
    diag, lane_head = masks
    qbd = jnp.where(diag, jnp.concatenate([qc] * ATT_GROUP, axis=0), jnp.zeros((), BF16))
    s = lax.dot_general(qbd, kb, (((1,), (1,)), ((), ())), preferred_element_type=F32) + bias
    if first_kpos is not None:
        kpos = first_kpos + lax.broadcasted_iota(jnp.int32, (1, BAND), 1)
        s = jnp.where(kpos >= 0, s, NEG_INF)
    m = jnp.max(s, axis=-1, keepdims=True)
    p = jnp.exp2(s - m)
    inv = 1.0 / jnp.sum(p, axis=-1, keepdims=True)
    o = jnp.dot(p.astype(BF16), vb, preferred_element_type=F32)
    last = ATT_GROUP - 1
    og = o[last * CHUNK:] * inv[last * CHUNK:]
    for hh in range(last - 1, -1, -1):
        rows = slice(hh * CHUNK, (hh + 1) * CHUNK)
        og = jnp.where(lane_head == hh, o[rows] * inv[rows], og)
    return og


def _att_core_kernel(first_pos, has_past, *refs):
    if has_past:
        q_ref, k_ref, v_ref, pk_ref, pv_ref, bias_ref, o_ref, kbuf, vbuf = refs
    else:
        q_ref, k_ref, v_ref, bias_ref, o_ref, kbuf, vbuf = refs
    t = q_ref.shape[0]
    if has_past:
        kbuf[0:BAND_ROWS, :] = pk_ref[...].astype(BF16)
        vbuf[0:BAND_ROWS, :] = pv_ref[...].astype(BF16)
    else:
        kbuf[0:BAND_ROWS, :] = jnp.zeros((BAND_ROWS, ATT_GW), BF16)
        vbuf[0:BAND_ROWS, :] = jnp.zeros((BAND_ROWS, ATT_GW), BF16)
    kbuf[BAND_ROWS:, :] = k_ref[...]
    vbuf[BAND_ROWS:, :] = v_ref[...]
    masks = _head_block_diag()

    def make_chunk(masked):
        def chunk(n, carry):
            r0 = pl.multiple_of(n * CHUNK, CHUNK)
            og = _att_chunk_group(q_ref[pl.ds(r0, CHUNK), :], kbuf[pl.ds(r0, BAND), :],
                                  vbuf[pl.ds(r0, BAND), :], bias_ref[...], masks,
                                  first_pos + r0 if masked else None)
            o_ref[pl.ds(r0, CHUNK), :] = og.astype(BF16)
            return carry
        return chunk

    n_chunks = t // CHUNK
    n_masked = min(n_chunks, max(0, -(first_pos // CHUNK)))
    for lo, hi, masked in ((0, n_masked, True), (n_masked, n_chunks, False)):
        if hi > lo:
            trips = hi - lo
            unroll = next(u for u in (ATT_UNROLL, 2, 1) if trips % u == 0)
            lax.fori_loop(lo, hi, make_chunk(masked), 0, unroll=unroll)


def _att_core(q, k, v, past_k, past_v, bias, first_pos, name):
    b, t, d = q.shape
    has_past = past_k is not None
    grp = lambda rows: pl.BlockSpec((None, rows, ATT_GW), lambda b_, g_: (b_, 0, g_))
    in_specs = [grp(t), grp(t), grp(t)]
    args = [q, k, v]
    if has_past:
        in_specs += [grp(BAND_ROWS), grp(BAND_ROWS)]
        args += [past_k, past_v]
    in_specs.append(pl.BlockSpec((None, ATT_GW, BAND), lambda b_, g_: (g_, 0, 0)))
    args.append(bias)
    return pl.pallas_call(
        functools.partial(_att_core_kernel, first_pos, has_past),
        grid=(b, ATT_NGROUPS),
        in_specs=in_specs,
        out_specs=grp(t),
        out_shape=jax.ShapeDtypeStruct((b, t, d), BF16),
        scratch_shapes=[pltpu.VMEM((BAND_ROWS + t, ATT_GW), BF16)] * 2,
        compiler_params=_cparams(2),
        name=name,
    )(*args)


def _att_fused_kernel(first_keep_tile, x_ref, sh_ref, sc_ref, gm_ref, ng_ref, wqkv_ref, p_ref,
                      qg_ref, kg_ref, bias_ref, wo_ref, o_ref, kk_ref, vk_ref,
                      q_scr, kbuf, vbuf, a_scr):
    tt, d = x_ref.shape
    t_idx = pl.program_id(1)

    @pl.when(t_idx == 0)
    def _():
        kbuf[0:BAND_ROWS, :] = jnp.zeros((BAND_ROWS, d), BF16)
        vbuf[0:BAND_ROWS, :] = jnp.zeros((BAND_ROWS, d), BF16)

    @pl.when(t_idx > 0)
    def _():
        kbuf[0:BAND_ROWS, :] = kbuf[tt:tt + BAND_ROWS, :]
        vbuf[0:BAND_ROWS, :] = vbuf[tt:tt + BAND_ROWS, :]

    hb = _norm_mod(x_ref[...][None], ng_ref[...], sc_ref[...], sh_ref[...])[0].astype(BF16)
    q = jnp.dot(hb, wqkv_ref[:, 0:d], preferred_element_type=F32)
    q_scr[...] = (_head_norm(q, p_ref[...], qg_ref[...]) * ATT_Q_SCALE).astype(BF16)
    k = jnp.dot(hb, wqkv_ref[:, d:2 * d], preferred_element_type=F32)
    k = _head_norm(k, p_ref[...], kg_ref[...])
    kbuf[BAND_ROWS:, :] = k.astype(BF16)
    v = jnp.dot(hb, wqkv_ref[:, 2 * d:3 * d], preferred_element_type=F32)
    vbuf[BAND_ROWS:, :] = v.astype(BF16)

    @pl.when(t_idx >= first_keep_tile)
    def _():
        kk_ref[...] = k
        vk_ref[...] = v

    masks = _head_block_diag()

    def make_chunk(masked):
        def chunk(n, carry):
            r0 = pl.multiple_of(n * CHUNK, CHUNK)
            for g in range(ATT_NGROUPS):
                gsl = slice(g * ATT_GW, (g + 1) * ATT_GW)
                og = _att_chunk_group(q_scr[pl.ds(r0, CHUNK), gsl], kbuf[pl.ds(r0, BAND), gsl],
                                      vbuf[pl.ds(r0, BAND), gsl], bias_ref[g], masks,
                                      r0 - BAND_ROWS if masked else None)
                a_scr[pl.ds(r0, CHUNK), gsl] = og.astype(BF16)
            return carry
        return chunk

    @pl.when(t_idx == 0)
    def _():
        lax.fori_loop(0, tt // CHUNK, make_chunk(True), 0, unroll=2)

    @pl.when(t_idx > 0)
    def _():
        lax.fori_loop(0, tt // CHUNK, make_chunk(False), 0, unroll=2)

    out = jnp.dot(a_scr[...], wo_ref[...], preferred_element_type=F32)
    o_ref[...] = x_ref[...] + gm_ref[0] * out


def _att_fused(x, mod, norm_g4, w_qkv, w_o, pavg, qg, kg, bias, keep, layer, widx, tt, row0):
    b, t, d = x.shape
    assert tt >= BAND_ROWS and tt % CHUNK == 0 and (tt // CHUNK) % 2 == 0
    assert keep % tt == 0 and t % tt == 0
    first_keep_tile = (t - keep) // tt
    xs = pl.BlockSpec((None, tt, d), lambda b_, t_: (b_, t_, 0))
    vec = pl.BlockSpec((None, 1, d), lambda b_, t_: (widx, 0, 0))
    keep_spec = pl.BlockSpec((None, tt, d),
                             lambda b_, t_: (b_, jnp.maximum(t_ - first_keep_tile, 0), 0))
    f32k = jax.ShapeDtypeStruct((b, keep, d), F32)
    return pl.pallas_call(
        functools.partial(_att_fused_kernel, first_keep_tile),
        grid=(b, t // tt),
        in_specs=[
            xs,
            _mod_spec(layer, 3, 1, row0), _mod_spec(layer, 4, 1, row0), _mod_spec(layer, 5, 1, row0),
            _ng_spec(layer, 1),
            _resident((None, d, 3 * d), lambda b_, t_: (widx, 0, 0)),
            _resident((ATT_GW, ATT_GW), lambda b_, t_: (0, 0)),
            vec, vec,
            _resident((ATT_NGROUPS, ATT_GW, BAND), lambda b_, t_: (0, 0, 0)),
            _resident((None, d, d), lambda b_, t_: (widx, 0, 0)),
        ],
        out_specs=[xs, keep_spec, keep_spec],
        out_shape=[jax.ShapeDtypeStruct(x.shape, F32), f32k, f32k],
        scratch_shapes=[pltpu.VMEM((tt, d), BF16), pltpu.VMEM((BAND_ROWS + tt, d), BF16),
                        pltpu.VMEM((BAND_ROWS + tt, d), BF16), pltpu.VMEM((tt, d), BF16)],
        compiler_params=_cparams(2),
        name=f"att_l{layer}",
    )(x, mod, mod, mod, norm_g4, w_qkv, pavg, qg, kg, bias, w_o)


def _att_bias_kernel(rb_ref, o_ref):
    rb = rb_ref[...]
    hi = rb.astype(BF16)
    r1 = rb - hi.astype(F32)
    mid = r1.astype(BF16)
    lo = (r1 - mid.astype(F32)).astype(BF16)
    r = lax.broadcasted_iota(jnp.int32, (REL_SIZE, REL_BASE_W), 0)
    u = lax.broadcasted_iota(jnp.int32, (REL_SIZE, REL_BASE_W), 1)
    rel = jnp.clip(BAND_ROWS + CHUNK - 1 - u, REL_MIN, REL_MAX) - REL_MIN
    sel = jnp.where(r == rel, 1.0, 0.0).astype(BF16)
    base = (jnp.dot(hi, sel, preferred_element_type=F32)
            + jnp.dot(mid, sel, preferred_element_type=F32)
            + jnp.dot(lo, sel, preferred_element_type=F32)) * LOG2E
    for i in range(CHUNK):
        o_ref[i] = base[:, CHUNK - 1 - i:CHUNK - 1 - i + BAND]


def _att_bias_table(rel_bias):
    tab = pl.pallas_call(
        _att_bias_kernel,
        out_shape=jax.ShapeDtypeStruct((CHUNK, ATT_HEADS, BAND), F32),
        name="att_bias",
    )(rel_bias)
    return tab.transpose(1, 0, 2).reshape(ATT_NGROUPS, ATT_GW, BAND)


def _rope_tables(pos0, t):
    inv = ROPE_BASE ** (-np.arange(ROPE_HALF, dtype=np.float64) / ROPE_HALF)
    ang = (pos0 + np.arange(t, dtype=np.float64))[:, None] * inv[None, :]
    return jnp.asarray(np.cos(ang), F32), jnp.asarray(np.sin(ang), F32)


def _trunk(x, mod, row0, pos0, bb, tt, ret_lc, ret_states, pool_bufs, att_k_cache, att_v_cache, p):
    b, t, d = x.shape
    new_pool, new_k, new_v = [], [], []
    ret_all = None
    cos, sin = _rope_tables(pos0, t)
    tt_ffn = FFN_ROWS if (bb == 1 and t % FFN_ROWS == 0) else tt
    bb_ffn = bb
    if tt == t and FFN_ROWS % t == 0:
        wide = min(b, FFN_ROWS // t)
        if b % wide == 0 and row0 % wide == 0:
            bb_ffn = wide
    for i in range(DEPTH):
        x = _ffn(x, mod, p['norm_g'], p['ffn1_w_in'], p['ffn1_w_out'], i, 1, bb_ffn, tt_ffn, row0)
        j = i // N_MIXERS
        kind = i % N_MIXERS
        if kind == 0:
            gn = p['ret_gn_g'][j].reshape(1, 2 * d)
            if ret_states is None:
                x, ret_all = _ret_fused(x, mod, p['norm_g'], p['ret_w_in'], p['ret_w_out'], gn,
                                        cos, sin, ret_all, ret_lc, max(tt, ret_lc),
                                        i, j, row0)
            else:
                q, k, v, g = _ret_proj(x, mod, p['norm_g'], p['ret_w_in'], cos, sin, ret_lc,
                                       i, j, bb, tt, row0)
                y, ret_all = _ret_core(q, k, v, g, ret_states, ret_all, j, gn, ret_lc,
                                       f"ret_core_l{i}")
                x = _matres(y, x, mod, p['ret_w_out'], i, j, bb, tt, row0, f"ret_out_l{i}")
        elif kind == 1:
            if pool_bufs is None:
                buf16 = jnp.zeros((b, POOL_CARRY, d), F32)
            else:
                buf16 = jnp.pad(pool_bufs[j], ((0, 0), (POOL_CARRY - POOL_BUF, 0), (0, 0)))
            if bb_ffn == 1:
                x, nb = _pool_ffn(x, mod, p['norm_g'], buf16, p['pool_w'], p['pool_b'],
                                  p['pool_scale'], p['ffn2_w_in'], p['ffn2_w_out'],
                                  i, j, tt_ffn, row0, pos0)
                new_pool.append(nb)
                continue
            x, nb = _pool(x, mod, p['norm_g'], buf16, p['pool_w'], p['pool_b'], p['pool_scale'],
                          i, j, min(tt, t), row0, pos0)
            new_pool.append(nb)
        else:
            keep = min(BAND_ROWS, t)
            if att_k_cache is None and pos0 == 0 and tt >= BAND_ROWS:
                x, k_keep, v_keep = _att_fused(
                    x, mod, p['norm_g'], p['att_w_qkv'], p['att_w_o'], p['att_pavg'],
                    p['att_q_g'], p['att_k_g'], p['att_bias'][j], keep, i, j, tt, row0)
            else:
                q, kb, vb, k_keep, v_keep = _att_proj(
                    x, mod, p['norm_g'], p['att_w_qkv'], p['att_pavg'], p['att_q_g'],
                    p['att_k_g'], keep, i, j, bb, tt, row0)
                if att_k_cache is None:
                    past_k = past_v = None
                else:
                    past_k = att_k_cache[j].reshape(b, BAND_ROWS, d)
                    past_v = att_v_cache[j].reshape(b, BAND_ROWS, d)
                o = _att_core(q, kb, vb, past_k, past_v, p['att_bias'][j], pos0 - BAND_ROWS,
                              f"att_core_l{i}")
                x = _matres(o, x, mod, p['att_w_o'], i, j, bb, tt, row0, f"att_out_l{i}")
            new_k.append(k_keep.reshape(b, keep, ATT_HEADS, ATT_HD))
            new_v.append(v_keep.reshape(b, keep, ATT_HEADS, ATT_HD))
        x = _ffn(x, mod, p['norm_g'], p['ffn2_w_in'], p['ffn2_w_out'], i, 2, bb_ffn, tt_ffn, row0)
    return x, ret_all, jnp.stack(new_pool), jnp.stack(new_k), jnp.stack(new_v)


def kernel(x_prompt, x_sample, c_prompt, c_sample, state_ret, state_pool, cache_att_k, cache_att_v, norm_g, ada_w, ada_b, ffn1_w_in, ffn1_w_out, ffn2_w_in, ffn2_w_out, ret_w_in, ret_w_out, ret_gn_g, pool_w, pool_b, pool_scale, att_w_qkv, att_w_o, att_q_g, att_k_g, att_rel_bias):
    assert cache_att_k.shape[2] == BAND_ROWS, "key cache must hold exactly the left band"
    bp, tp, d = x_prompt.shape
    bs, ts, _ = x_sample.shape
    n_att = att_w_qkv.shape[0]
    head_id = np.arange(ATT_GW) // ATT_HD
    pavg = jnp.asarray((head_id[:, None] == head_id[None, :]) / ATT_HD, BF16)
    p = {
        'norm_g': norm_g.reshape(DEPTH, 3, 1, d),
        'ffn1_w_in': _interleave_gate_up(ffn1_w_in), 'ffn1_w_out': ffn1_w_out.astype(BF16),
        'ffn2_w_in': _interleave_gate_up(ffn2_w_in), 'ffn2_w_out': ffn2_w_out.astype(BF16),
        'ret_w_in': ret_w_in.astype(BF16), 'ret_w_out': ret_w_out.astype(BF16),
        'ret_gn_g': ret_gn_g,
        'pool_w': pool_w.astype(BF16),
        'pool_b': pool_b.reshape(-1, 1, d), 'pool_scale': pool_scale.reshape(-1, 1, d),
        'att_w_qkv': att_w_qkv.astype(BF16), 'att_w_o': att_w_o.astype(BF16),
        'att_q_g': jnp.tile(att_q_g, (1, ATT_HEADS)).reshape(n_att, 1, d),
        'att_k_g': jnp.tile(att_k_g, (1, ATT_HEADS)).reshape(n_att, 1, d),
        'att_pavg': pavg,
        'att_bias': [_att_bias_table(att_rel_bias[j]) for j in range(n_att)],
    }
    mod = _ada(jnp.concatenate([c_prompt, c_sample], axis=0), ada_w, ada_b)

    tt_p = min(PROMPT_ROWS, tp)
    y_prompt, ret_p, pool_p, k_p, v_p = _trunk(
        x_prompt, mod, 0, 0, 1, tt_p, min(RET_CHUNK_PROMPT, tp), None, None, None, None, p)
    bb_s = max(1, min(bs, PROMPT_ROWS // ts))
    y_sample, ret_s, pool_s, k_s, v_s = _trunk(
        x_sample, mod, bp, PAST_LEN, bb_s, ts, ts, state_ret, state_pool,
        cache_att_k, cache_att_v, p)
    return (y_prompt, y_sample, ret_p, ret_s, pool_p, pool_s, k_p, v_p, k_s, v_s)
```

```python
import functools

import numpy as np
import jax
import jax.numpy as jnp
from jax import lax
from jax.experimental import pallas as pl
from jax.experimental.pallas import tpu as pltpu

F32 = jnp.float32
BF16 = jnp.bfloat16

LANES = 128
D_MODEL = 1024
DEPTH = 4
D_FF = 2816
EPS = 1e-6
CHUNK = 64
PAST_LEN = 2048
N_MIXERS = 3
RET_HEADS = 4
RET_DK = D_MODEL // RET_HEADS
RET_DV = 2 * D_MODEL // RET_HEADS
ROPE_BASE = 10000.0
ROPE_HALF = RET_DK // 2
POOL_WINDOWS = (2, 4, 8, 16)
POOL_GROUPS = 4
POOL_GC = D_MODEL // POOL_GROUPS
POOL_BUF = 15
POOL_CARRY = 16
ATT_HEADS = 16
ATT_HD = D_MODEL // ATT_HEADS
ATT_GROUP = 4
ATT_GW = ATT_GROUP * ATT_HD
ATT_NGROUPS = ATT_HEADS // ATT_GROUP
LEFT_CHUNKS = 8
BAND_ROWS = LEFT_CHUNKS * CHUNK
BAND = BAND_ROWS + CHUNK
REL_MIN = -(CHUNK - 1)
REL_MAX = 256
REL_SIZE = REL_MAX - REL_MIN + 1
REL_BASE_W = 640
NEG_INF = -1e30
LOG2E = 1.4426950408889634
ATT_Q_SCALE = ATT_HD ** -0.5 * LOG2E
N_MOD = 9

VMEM_LIMIT_BYTES = 56 * 1024 * 1024
PROMPT_ROWS = 512
FFN_ROWS = 1024
RET_CHUNK_PROMPT = 256
ATT_UNROLL = 8


def _cparams(n_grid):
    return pltpu.CompilerParams(
        dimension_semantics=("arbitrary",) * n_grid,
        vmem_limit_bytes=VMEM_LIMIT_BYTES,
    )


def _resident(block_shape, index_map):
    return pl.BlockSpec(block_shape, index_map, pipeline_mode=pl.Buffered(1))


def _norm_mod(x, ng, sc, sh):
    ms = jnp.mean(x * x, axis=-1, keepdims=True)
    y = x * lax.rsqrt(ms + EPS) * ng
    return y * (1.0 + sc) + sh


def _silu(x):
    return x * (1.0 / (1.0 + jnp.exp(-x)))


def _ada_kernel(c_ref, w_ref, b_ref, o_ref):
    c = _silu(c_ref[...]).astype(BF16)
    w = w_ref[...].astype(BF16)
    o_ref[...] = jnp.dot(c, w, preferred_element_type=F32) + b_ref[...]


def _ada(c_all, ada_w, ada_b):
    n = c_all.shape[0]
    b4 = ada_b.reshape(DEPTH, N_MOD, 1, D_MODEL)
    out = pl.pallas_call(
        _ada_kernel,
        grid=(DEPTH, N_MOD),
        in_specs=[
            pl.BlockSpec((n, D_MODEL), lambda i, j: (0, 0)),
            pl.BlockSpec((None, D_MODEL, D_MODEL), lambda i, j: (i, 0, j)),
            pl.BlockSpec((None, None, 1, D_MODEL), lambda i, j: (i, j, 0, 0)),
        ],
        out_specs=pl.BlockSpec((None, None, n, D_MODEL), lambda i, j: (i, j, 0, 0)),
        out_shape=jax.ShapeDtypeStruct((DEPTH, N_MOD, n, D_MODEL), F32),
        compiler_params=_cparams(2),
        name="ada_mod",
    )(c_all, ada_w, b4)
    return out.reshape(DEPTH, N_MOD, n, 1, D_MODEL)


def _mod_spec(layer, j, bb, row0):
    blk0 = row0 // bb
    return pl.BlockSpec((None, None, bb, 1, D_MODEL),
                        lambda b, t: (layer, j, blk0 + b, 0, 0))


def _ng_spec(layer, j):
    return pl.BlockSpec((None, None, 1, D_MODEL), lambda b, t: (layer, j, 0, 0))


def _x_spec(bb, tt, width=D_MODEL):
    return pl.BlockSpec((bb, tt, width), lambda b, t: (b, t, 0))


GATE_UP_TILES = 2


def _gate_up_kernel(g_ref, u_ref, o_ref):
    for i in range(GATE_UP_TILES):
        src = slice(i * LANES, (i + 1) * LANES)
        o_ref[:, 2 * i * LANES:(2 * i + 1) * LANES] = g_ref[:, src].astype(BF16)
        o_ref[:, (2 * i + 1) * LANES:(2 * i + 2) * LANES] = u_ref[:, src].astype(BF16)


def _interleave_gate_up(w_in):
    nl, d, _ = w_in.shape
    width = GATE_UP_TILES * LANES
    steps = D_FF // width
    assert D_FF % width == 0
    return pl.pallas_call(
        _gate_up_kernel,
        grid=(nl, steps),
        in_specs=[pl.BlockSpec((None, d, width), lambda l, j: (l, 0, j)),
                  pl.BlockSpec((None, d, width), lambda l, j: (l, 0, steps + j))],
        out_specs=pl.BlockSpec((None, d, 2 * width), lambda l, j: (l, 0, j)),
        out_shape=jax.ShapeDtypeStruct((nl, d, 2 * D_FF), BF16),
        compiler_params=_cparams(2),
        name="ffn_gate_up_layout",
    )(w_in, w_in)


def _swiglu(hb, win_ref, wout_ref):
    gu = jnp.dot(hb, win_ref[...], preferred_element_type=F32)
    act = jnp.concatenate(
        [_silu(gu[:, 2 * j * LANES:(2 * j + 1) * LANES]) * gu[:, (2 * j + 1) * LANES:(2 * j + 2) * LANES]
         for j in range(D_FF // LANES)], axis=1).astype(BF16)
    return jnp.dot(act, wout_ref[...], preferred_element_type=F32)


def _ffn_kernel(x_ref, sh_ref, sc_ref, g_ref, ng_ref, win_ref, wout_ref, o_ref):
    bb, tt, d = x_ref.shape
    x = x_ref[...]
    h = _norm_mod(x, ng_ref[...], sc_ref[...], sh_ref[...])
    y = _swiglu(h.reshape(bb * tt, d).astype(BF16), win_ref, wout_ref)
    o_ref[...] = x + (0.5 * g_ref[...]) * y.reshape(bb, tt, d)


def _ffn(x, mod, norm_g4, w_in, w_out, layer, which, bb, tt, row0):
    b, t, d = x.shape
    j0 = 0 if which == 1 else 6
    nj = 0 if which == 1 else 2
    return pl.pallas_call(
        _ffn_kernel,
        grid=(b // bb, t // tt),
        in_specs=[
            _x_spec(bb, tt),
            _mod_spec(layer, j0, bb, row0),
            _mod_spec(layer, j0 + 1, bb, row0),
            _mod_spec(layer, j0 + 2, bb, row0),
            _ng_spec(layer, nj),
            _resident((None, D_MODEL, 2 * D_FF), lambda b_, t_: (layer, 0, 0)),
            _resident((None, D_FF, D_MODEL), lambda b_, t_: (layer, 0, 0)),
        ],
        out_specs=_x_spec(bb, tt),
        out_shape=jax.ShapeDtypeStruct(x.shape, F32),
        compiler_params=_cparams(2),
        name=f"ffn{which}_l{layer}",
    )(x, mod, mod, mod, norm_g4, w_in, w_out)


def _matres_kernel(a_ref, x_ref, g_ref, w_ref, o_ref):
    bb, tt, k = a_ref.shape
    y = jnp.dot(a_ref[...].reshape(bb * tt, k), w_ref[...], preferred_element_type=F32)
    o_ref[...] = x_ref[...] + g_ref[...] * y.reshape(bb, tt, D_MODEL)


def _matres(a, x, mod, w, layer, widx, bb, tt, row0, name):
    b, t, k = a.shape
    return pl.pallas_call(
        _matres_kernel,
        grid=(b // bb, t // tt),
        in_specs=[
            _x_spec(bb, tt, k),
            _x_spec(bb, tt),
            _mod_spec(layer, 5, bb, row0),
            _resident((None, k, D_MODEL), lambda b_, t_: (widx, 0, 0)),
        ],
        out_specs=_x_spec(bb, tt),
        out_shape=jax.ShapeDtypeStruct(x.shape, F32),
        compiler_params=_cparams(2),
        name=name,
    )(a, x, mod, w)


def _ret_proj_kernel(x_ref, sh_ref, sc_ref, ng_ref, w_ref, cos_ref, sin_ref, qs_ref, ks_ref,
                     q_ref, k_ref, v_ref, g_ref):
    bb, tt, d = x_ref.shape
    h = _norm_mod(x_ref[...], ng_ref[...], sc_ref[...], sh_ref[...])
    hb = h.reshape(bb * tt, d).astype(BF16)
    cos = cos_ref[...][None]
    sin = sin_ref[...][None]
    for idx, (o_ref, s_ref) in enumerate(((q_ref, qs_ref), (k_ref, ks_ref))):
        y = jnp.dot(hb, w_ref[:, idx * d:(idx + 1) * d], preferred_element_type=F32)
        y = y.reshape(bb, tt, d)
        for hd in range(RET_HEADS):
            lo = hd * RET_DK
            mid = lo + ROPE_HALF
            x1 = y[:, :, lo:mid]
            x2 = y[:, :, mid:lo + RET_DK]
            o_ref[:, :, lo:mid] = ((x1 * cos - x2 * sin) * s_ref[:, lo:mid][None]).astype(BF16)
            o_ref[:, :, mid:lo + RET_DK] = (
                (x1 * sin + x2 * cos) * s_ref[:, mid:lo + RET_DK][None]).astype(BF16)
    v = jnp.dot(hb, w_ref[:, 2 * d:4 * d], preferred_element_type=F32)
    v_ref[...] = v.reshape(bb, tt, 2 * d).astype(BF16)
    g = jnp.dot(hb, w_ref[:, 4 * d:6 * d], preferred_element_type=F32)
    g_ref[...] = _silu(g).reshape(bb, tt, 2 * d).astype(BF16)


def _ret_gamma():
    return 1.0 - 2.0 ** (-5.0 - np.arange(RET_HEADS, dtype=np.float64))


def _ret_scale_tables(tt, lc):
    gamma = _ret_gamma()
    i = (np.arange(tt) % lc).astype(np.float64)[:, None]
    g_lane = np.repeat(gamma, RET_DK)[None, :]
    qs = g_lane ** (i + 1.0)
    ks = (RET_DK ** -0.5) * g_lane ** (-(i + 1.0))
    return jnp.asarray(qs, F32), jnp.asarray(ks, F32)


def _ret_proj(x, mod, norm_g4, w_in, cos, sin, lc, layer, widx, bb, tt, row0):
    b, t, d = x.shape
    assert tt % lc == 0
    qs, ks = _ret_scale_tables(tt, lc)
    tab = pl.BlockSpec((tt, ROPE_HALF), lambda b_, t_: (t_, 0))
    scale = _resident((tt, d), lambda b_, t_: (0, 0))
    return pl.pallas_call(
        _ret_proj_kernel,
        grid=(b // bb, t // tt),
        in_specs=[
            _x_spec(bb, tt),
            _mod_spec(layer, 3, bb, row0),
            _mod_spec(layer, 4, bb, row0),
            _ng_spec(layer, 1),
            _resident((None, D_MODEL, 6 * D_MODEL), lambda b_, t_: (widx, 0, 0)),
            tab, tab, scale, scale,
        ],
        out_specs=[_x_spec(bb, tt), _x_spec(bb, tt),
                   _x_spec(bb, tt, 2 * d), _x_spec(bb, tt, 2 * d)],
        out_shape=[jax.ShapeDtypeStruct((b, t, d), BF16), jax.ShapeDtypeStruct((b, t, d), BF16),
                   jax.ShapeDtypeStruct((b, t, 2 * d), BF16), jax.ShapeDtypeStruct((b, t, 2 * d), BF16)],
        compiler_params=_cparams(2),
        name=f"ret_proj_l{layer}",
    )(x, mod, mod, norm_g4, w_in, cos, sin, qs, ks)


def _ret_head(qh, kh, vh, s, tril, gn, cdec_h):
    inner = lax.dot_general(qh, kh, (((1,), (1,)), ((), ())), preferred_element_type=F32) * tril
    o = (jnp.dot(inner.astype(BF16), vh, preferred_element_type=F32)
         + jnp.dot(qh, s.astype(BF16), preferred_element_type=F32))
    s_new = (s + lax.dot_general(kh, vh, (((0,), (0,)), ((), ())),
                                 preferred_element_type=F32)) * cdec_h
    mu = jnp.mean(o, axis=-1, keepdims=True)
    oc = o - mu
    var = jnp.mean(oc * oc, axis=-1, keepdims=True)
    return oc * lax.rsqrt(var + EPS) * gn, s_new


def _init_state_stack(s_ref, slot, prev_ref, s0_ref):
    if slot:
        s_ref[0:slot] = prev_ref[...]
    if s0_ref is None:
        s_ref[slot] = jnp.zeros(s_ref.shape[1:], F32)
    else:
        s_ref[slot] = s0_ref[...]


def _ret_core_kernel(cdec, has_state, slot, *refs):
    refs = list(refs)
    q_ref, k_ref, v_ref, g_ref = refs[:4]
    s0_ref = refs[4] if has_state else None
    tril_ref, gn_ref = refs[4 + has_state:6 + has_state]
    prev_ref = refs[6 + has_state] if slot else None
    y_ref, s_ref = refs[6 + has_state + bool(slot):]

    @pl.when(pl.program_id(1) == 0)
    def _():
        _init_state_stack(s_ref, slot, prev_ref, s0_ref)

    for hd in range(RET_HEADS):
        ksl = slice(hd * RET_DK, (hd + 1) * RET_DK)
        vsl = slice(hd * RET_DV, (hd + 1) * RET_DV)
        on, s_new = _ret_head(q_ref[:, ksl], k_ref[:, ksl], v_ref[:, vsl], s_ref[slot, hd],
                              tril_ref[...], gn_ref[:, vsl], cdec[hd])
        s_ref[slot, hd] = s_new
        y_ref[:, vsl] = (g_ref[:, vsl].astype(F32) * on).astype(BF16)


def _state_stack_specs(slot, b):
    blk = lambda n: pl.BlockSpec((n, None, RET_HEADS, RET_DK, RET_DV),
                                 lambda b_, t_: (0, b_, 0, 0, 0))
    shape = jax.ShapeDtypeStruct((slot + 1, b, RET_HEADS, RET_DK, RET_DV), F32)
    return blk(slot + 1), shape, (blk(slot) if slot else None)


def _ret_core(q, k, v, g, s0_all, s_prev, slot, gn_g, lc, name):
    b, t, d = q.shape
    cdec = tuple(float(c) for c in _ret_gamma() ** float(lc))
    tril = jnp.asarray(np.tril(np.ones((lc, lc))), F32)
    has_state = s0_all is not None
    row = lambda width: pl.BlockSpec((None, lc, width), lambda b_, t_: (b_, t_, 0))
    s_spec, s_shape, prev_spec = _state_stack_specs(slot, b)
    in_specs = [row(d), row(d), row(2 * d), row(2 * d)]
    args = [q, k, v, g]
    if has_state:
        in_specs.append(pl.BlockSpec((None, None, RET_HEADS, RET_DK, RET_DV),
                                     lambda b_, t_: (slot, b_, 0, 0, 0)))
        args.append(s0_all)
    in_specs += [_resident((lc, lc), lambda b_, t_: (0, 0)),
                 pl.BlockSpec((1, 2 * d), lambda b_, t_: (0, 0))]
    args += [tril, gn_g]
    if slot:
        in_specs.append(prev_spec)
        args.append(s_prev)
    return pl.pallas_call(
        functools.partial(_ret_core_kernel, cdec, has_state, slot),
        grid=(b, t // lc),
        in_specs=in_specs,
        out_specs=[row(2 * d), s_spec],
        out_shape=[jax.ShapeDtypeStruct((b, t, 2 * d), BF16), s_shape],
        compiler_params=_cparams(2),
        name=name,
    )(*args)


def _ret_fused_kernel(cdec, lc, slot, *refs):
    refs = list(refs)
    (x_ref, sh_ref, sc_ref, gm_ref, ng_ref, win_ref, wout_ref, cos_ref, sin_ref, qs_ref, ks_ref,
     tril_ref, gn_ref) = refs[:13]
    prev_ref = refs[13] if slot else None
    o_ref, s_ref, q_scr, k_scr, v_scr, y_scr = refs[13 + bool(slot):]
    tt, d = x_ref.shape

    @pl.when(pl.program_id(1) == 0)
    def _():
        _init_state_stack(s_ref, slot, prev_ref, None)

    hb = _norm_mod(x_ref[...][None], ng_ref[...], sc_ref[...], sh_ref[...])[0].astype(BF16)
    cos = cos_ref[...]
    sin = sin_ref[...]
    for idx, (scr, tab) in enumerate(((q_scr, qs_ref), (k_scr, ks_ref))):
        y = jnp.dot(hb, win_ref[:, idx * d:(idx + 1) * d], preferred_element_type=F32)
        for hd in range(RET_HEADS):
            lo = hd * RET_DK
            mid = lo + ROPE_HALF
            hi = lo + RET_DK
            x1 = y[:, lo:mid]
            x2 = y[:, mid:hi]
            scr[:, lo:mid] = ((x1 * cos - x2 * sin) * tab[:, lo:mid]).astype(BF16)
            scr[:, mid:hi] = ((x1 * sin + x2 * cos) * tab[:, mid:hi]).astype(BF16)
    v_scr[...] = jnp.dot(hb, win_ref[:, 2 * d:4 * d], preferred_element_type=F32).astype(BF16)
    for hd in range(RET_HEADS):
        ksl = slice(hd * RET_DK, (hd + 1) * RET_DK)
        vsl = slice(hd * RET_DV, (hd + 1) * RET_DV)
        g = jnp.dot(hb, win_ref[:, 4 * d + hd * RET_DV:4 * d + (hd + 1) * RET_DV],
                    preferred_element_type=F32)
        for c in range(tt // lc):
            rows = slice(c * lc, (c + 1) * lc)
            on, s_new = _ret_head(q_scr[rows, ksl], k_scr[rows, ksl], v_scr[rows, vsl],
                                  s_ref[slot, hd], tril_ref[...], gn_ref[:, vsl], cdec[hd])
            s_ref[slot, hd] = s_new
            y_scr[rows, vsl] = (_silu(g[rows]) * on).astype(BF16)
    out = jnp.dot(y_scr[...], wout_ref[...], preferred_element_type=F32)
    o_ref[...] = x_ref[...] + gm_ref[0] * out


def _ret_fused(x, mod, norm_g4, w_in, w_out, gn_g, cos, sin, s_prev, lc, tt, layer, widx, row0):
    b, t, d = x.shape
    assert tt % lc == 0 and t % tt == 0
    cdec = tuple(float(c) for c in _ret_gamma() ** float(lc))
    tril = jnp.asarray(np.tril(np.ones((lc, lc))), F32)
    qs, ks = _ret_scale_tables(tt, lc)
    slot = widx
    xs = pl.BlockSpec((None, tt, d), lambda b_, t_: (b_, t_, 0))
    tab = pl.BlockSpec((tt, ROPE_HALF), lambda b_, t_: (t_, 0))
    const2 = lambda shape: _resident(shape, lambda b_, t_: (0, 0))
    s_spec, s_shape, prev_spec = _state_stack_specs(slot, b)
    in_specs = [
        xs,
        _mod_spec(layer, 3, 1, row0), _mod_spec(layer, 4, 1, row0), _mod_spec(layer, 5, 1, row0),
        _ng_spec(layer, 1),
        _resident((None, d, 6 * d), lambda b_, t_: (widx, 0, 0)),
        _resident((None, 2 * d, d), lambda b_, t_: (widx, 0, 0)),
        tab, tab, const2((tt, d)), const2((tt, d)), const2((lc, lc)),
        pl.BlockSpec((1, 2 * d), lambda b_, t_: (0, 0)),
    ]
    args = [x, mod, mod, mod, norm_g4, w_in, w_out, cos, sin, qs, ks, tril, gn_g]
    if slot:
        in_specs.append(prev_spec)
        args.append(s_prev)
    return pl.pallas_call(
        functools.partial(_ret_fused_kernel, cdec, lc, slot),
        grid=(b, t // tt),
        in_specs=in_specs,
        out_specs=[xs, s_spec],
        out_shape=[jax.ShapeDtypeStruct(x.shape, F32), s_shape],
        scratch_shapes=[pltpu.VMEM((tt, d), BF16), pltpu.VMEM((tt, d), BF16),
                        pltpu.VMEM((tt, 2 * d), BF16), pltpu.VMEM((tt, 2 * d), BF16)],
        compiler_params=_cparams(2),
        name=f"ret_l{layer}",
    )(*args)


def _pool_mix(pos0, x, sh_ref, sc_ref, gm_ref, ng_ref, buf_ref, w_ref, pb_ref, ps_ref,
              nb_ref, carry_ref):
    tt, d = x.shape
    t_idx = pl.program_id(1)

    @pl.when(t_idx == 0)
    def _():
        carry_ref[...] = buf_ref[...]

    h = _norm_mod(x[None], ng_ref[...], sc_ref[...], sh_ref[...])[0]
    ext = jnp.concatenate([carry_ref[...], h], axis=0)
    carry_ref[...] = h[tt - POOL_CARRY:, :]
    nb_ref[...] = h[tt - POOL_BUF:, :]

    pos = pos0 + t_idx * tt + lax.broadcasted_iota(jnp.int32, (tt, 1), 0)
    run = ext
    w = 1
    outs = []
    for gi, win in enumerate(POOL_WINDOWS):
        while w < win:
            n = run.shape[0]
            run = run[w:, :] + run[:n - w, :]
            w *= 2
        wsum = run[run.shape[0] - tt:, :POOL_GC]
        inv_cnt = 1.0 / jnp.minimum(pos + 1, win).astype(F32)
        lo = gi * POOL_GC
        pooled = wsum * inv_cnt - h[:, lo:lo + POOL_GC]
        outs.append(jnp.dot(pooled.astype(BF16), w_ref[gi], preferred_element_type=F32))
        run = run[:, POOL_GC:]
    y = (jnp.concatenate(outs, axis=1) + pb_ref[...]) * ps_ref[...]
    return x + gm_ref[0] * y


def _pool_kernel(pos0, x_ref, sh_ref, sc_ref, gm_ref, ng_ref, buf_ref, w_ref, pb_ref, ps_ref,
                 o_ref, nb_ref, carry_ref):
    o_ref[...] = _pool_mix(pos0, x_ref[...], sh_ref, sc_ref, gm_ref, ng_ref, buf_ref, w_ref,
                           pb_ref, ps_ref, nb_ref, carry_ref)


def _pool_ffn_kernel(pos0, x_ref, sh_ref, sc_ref, gm_ref, ng_ref, buf_ref, w_ref, pb_ref, ps_ref,
                     fsh_ref, fsc_ref, fg_ref, fng_ref, win_ref, wout_ref,
                     o_ref, nb_ref, carry_ref):
    x = _pool_mix(pos0, x_ref[...], sh_ref, sc_ref, gm_ref, ng_ref, buf_ref, w_ref,
                  pb_ref, ps_ref, nb_ref, carry_ref)
    hb = _norm_mod(x[None], fng_ref[...], fsc_ref[...], fsh_ref[...])[0].astype(BF16)
    o_ref[...] = x + (0.5 * fg_ref[0]) * _swiglu(hb, win_ref, wout_ref)


def _pool_specs(x, layer, widx, tt, row0):
    b, t, d = x.shape
    xs = pl.BlockSpec((None, tt, d), lambda b_, t_: (b_, t_, 0))
    vec = pl.BlockSpec((None, 1, d), lambda b_, t_: (widx, 0, 0))
    in_specs = [
        xs,
        _mod_spec(layer, 3, 1, row0),
        _mod_spec(layer, 4, 1, row0),
        _mod_spec(layer, 5, 1, row0),
        _ng_spec(layer, 1),
        pl.BlockSpec((None, POOL_CARRY, d), lambda b_, t_: (b_, 0, 0)),
        pl.BlockSpec((None, POOL_GROUPS, POOL_GC, POOL_GC), lambda b_, t_: (widx, 0, 0, 0)),
        vec, vec,
    ]
    out_specs = [xs, pl.BlockSpec((None, POOL_BUF, d), lambda b_, t_: (b_, 0, 0))]
    out_shape = [jax.ShapeDtypeStruct(x.shape, F32), jax.ShapeDtypeStruct((b, POOL_BUF, d), F32)]
    return in_specs, out_specs, out_shape


def _pool(x, mod, norm_g4, buf16, w, pb, ps, layer, widx, tt, row0, pos0):
    b, t, d = x.shape
    in_specs, out_specs, out_shape = _pool_specs(x, layer, widx, tt, row0)
    return pl.pallas_call(
        functools.partial(_pool_kernel, pos0),
        grid=(b, t // tt),
        in_specs=in_specs,
        out_specs=out_specs,
        out_shape=out_shape,
        scratch_shapes=[pltpu.VMEM((POOL_CARRY, d), F32)],
        compiler_params=_cparams(2),
        name=f"pool_l{layer}",
    )(x, mod, mod, mod, norm_g4, buf16, w, pb, ps)


def _pool_ffn(x, mod, norm_g4, buf16, w, pb, ps, w_in, w_out, layer, widx, tt, row0, pos0):
    b, t, d = x.shape
    in_specs, out_specs, out_shape = _pool_specs(x, layer, widx, tt, row0)
    in_specs += [
        _mod_spec(layer, 6, 1, row0), _mod_spec(layer, 7, 1, row0), _mod_spec(layer, 8, 1, row0),
        _ng_spec(layer, 2),
        _resident((None, D_MODEL, 2 * D_FF), lambda b_, t_: (layer, 0, 0)),
        _resident((None, D_FF, D_MODEL), lambda b_, t_: (layer, 0, 0)),
    ]
    return pl.pallas_call(
        functools.partial(_pool_ffn_kernel, pos0),
        grid=(b, t // tt),
        in_specs=in_specs,
        out_specs=out_specs,
        out_shape=out_shape,
        scratch_shapes=[pltpu.VMEM((POOL_CARRY, d), F32)],
        compiler_params=_cparams(2),
        name=f"pool_ffn2_l{layer}",
    )(x, mod, mod, mod, norm_g4, buf16, w, pb, ps, mod, mod, mod, norm_g4, w_in, w_out)


def _head_norm(y, pavg, gain):
    y2 = (y * y).astype(BF16)
    ms = jnp.concatenate(
        [jnp.dot(y2[:, g * ATT_GW:(g + 1) * ATT_GW], pavg, preferred_element_type=F32)
         for g in range(ATT_NGROUPS)], axis=1)
    return y * lax.rsqrt(ms + EPS) * gain


def _att_proj_kernel(first_keep_tile, x_ref, sh_ref, sc_ref, ng_ref, w_ref, p_ref, qg_ref, kg_ref,
                     q_ref, kb_ref, vb_ref, kk_ref, vk_ref):
    bb, tt, d = x_ref.shape
    h = _norm_mod(x_ref[...], ng_ref[...], sc_ref[...], sh_ref[...])
    hb = h.reshape(bb * tt, d).astype(BF16)
    q = jnp.dot(hb, w_ref[:, 0:d], preferred_element_type=F32)
    q = _head_norm(q, p_ref[...], qg_ref[...]) * ATT_Q_SCALE
    q_ref[...] = q.reshape(bb, tt, d).astype(BF16)
    k = jnp.dot(hb, w_ref[:, d:2 * d], preferred_element_type=F32)
    k = _head_norm(k, p_ref[...], kg_ref[...]).reshape(bb, tt, d)
    kb_ref[...] = k.astype(BF16)
    v = jnp.dot(hb, w_ref[:, 2 * d:3 * d], preferred_element_type=F32).reshape(bb, tt, d)
    vb_ref[...] = v.astype(BF16)

    @pl.when(pl.program_id(1) >= first_keep_tile)
    def _():
        kk_ref[...] = k
        vk_ref[...] = v


def _att_proj(x, mod, norm_g4, w_qkv, pavg, qg, kg, keep, layer, widx, bb, tt, row0):
    b, t, d = x.shape
    assert keep % tt == 0 and t % tt == 0
    first_keep_tile = (t - keep) // tt
    vec = pl.BlockSpec((None, 1, d), lambda b_, t_: (widx, 0, 0))
    keep_spec = pl.BlockSpec((bb, tt, d),
                             lambda b_, t_: (b_, jnp.maximum(t_ - first_keep_tile, 0), 0))
    b16o = jax.ShapeDtypeStruct((b, t, d), BF16)
    f32k = jax.ShapeDtypeStruct((b, keep, d), F32)
    return pl.pallas_call(
        functools.partial(_att_proj_kernel, first_keep_tile),
        grid=(b // bb, t // tt),
        in_specs=[
            _x_spec(bb, tt),
            _mod_spec(layer, 3, bb, row0),
            _mod_spec(layer, 4, bb, row0),
            _ng_spec(layer, 1),
            _resident((None, d, 3 * d), lambda b_, t_: (widx, 0, 0)),
            _resident((ATT_GW, ATT_GW), lambda b_, t_: (0, 0)),
            vec, vec,
        ],
        out_specs=[_x_spec(bb, tt)] * 3 + [keep_spec] * 2,
        out_shape=[b16o, b16o, b16o, f32k, f32k],
        compiler_params=_cparams(2),
        name=f"att_proj_l{layer}",
    )(x, mod, mod, norm_g4, w_qkv, pavg, qg, kg)


def _head_block_diag():
    rb = lax.broadcasted_iota(jnp.int32, (ATT_GW, ATT_GW), 0) // ATT_HD
    cb = lax.broadcasted_iota(jnp.int32, (ATT_GW, ATT_GW), 1) // ATT_HD
    lane_head = lax.broadcasted_iota(jnp.int32, (CHUNK, ATT_GW), 1) // ATT_HD
    return rb == cb, lane_head


def _att_chunk_group(qc, kb, vb, bias, masks, first_kpos):
    diag, lane_head = masks
    qbd = jnp.where(diag, jnp.concatenate([qc] * ATT_GROUP, axis=0), jnp.zeros((), BF16))
    s = lax.dot_general(qbd, kb, (((1,), (1,)), ((), ())), preferred_element_type=F32) + bias
    if first_kpos is not None:
        kpos = first_kpos + lax.broadcasted_iota(jnp.int32, (1, BAND), 1)
        s = jnp.where(kpos >= 0, s, NEG_INF)
    m = jnp.max(s, axis=-1, keepdims=True)
    p = jnp.exp2(s - m)
    inv = 1.0 / jnp.sum(p, axis=-1, keepdims=True)
    o = jnp.dot(p.astype(BF16), vb, preferred_element_type=F32)
    last = ATT_GROUP - 1
    og = o[last * CHUNK:] * inv[last * CHUNK:]
    for hh in range(last - 1, -1, -1):
        rows = slice(hh * CHUNK, (hh + 1) * CHUNK)
        og = jnp.where(lane_head == hh, o[rows] * inv[rows], og)
    return og


def _att_core_kernel(first_pos, has_past, *refs):
    if has_past:
        q_ref, k_ref, v_ref, pk_ref, pv_ref, bias_ref, o_ref, kbuf, vbuf = refs
    else:
        q_ref, k_ref, v_ref, bias_ref, o_ref, kbuf, vbuf = refs
    t = q_ref.shape[0]
    if has_past:
        kbuf[0:BAND_ROWS, :] = pk_ref[...].astype(BF16)
        vbuf[0:BAND_ROWS, :] = pv_ref[...].astype(BF16)
    else:
        kbuf[0:BAND_ROWS, :] = jnp.zeros((BAND_ROWS, ATT_GW), BF16)
        vbuf[0:BAND_ROWS, :] = jnp.zeros((BAND_ROWS, ATT_GW), BF16)
    kbuf[BAND_ROWS:, :] = k_ref[...]
    vbuf[BAND_ROWS:, :] = v_ref[...]
    masks = _head_block_diag()

    def make_chunk(masked):
        def chunk(n, carry):
            r0 = pl.multiple_of(n * CHUNK, CHUNK)
            og = _att_chunk_group(q_ref[pl.ds(r0, CHUNK), :], kbuf[pl.ds(r0, BAND), :],
                                  vbuf[pl.ds(r0, BAND), :], bias_ref[...], masks,
                                  first_pos + r0 if masked else None)
            o_ref[pl.ds(r0, CHUNK), :] = og.astype(BF16)
            return carry
        return chunk

    n_chunks = t // CHUNK
    n_masked = min(n_chunks, max(0, -(first_pos // CHUNK)))
    for lo, hi, masked in ((0, n_masked, True), (n_masked, n_chunks, False)):
        if hi > lo:
            trips = hi - lo
            unroll = next(u for u in (ATT_UNROLL, 2, 1) if trips % u == 0)
            lax.fori_loop(lo, hi, make_chunk(masked), 0, unroll=unroll)


def _att_core(q, k, v, past_k, past_v, bias, first_pos, name):
    b, t, d = q.shape
    has_past = past_k is not None
    grp = lambda rows: pl.BlockSpec((None, rows, ATT_GW), lambda b_, g_: (b_, 0, g_))
    in_specs = [grp(t), grp(t), grp(t)]
    args = [q, k, v]
    if has_past:
        in_specs += [grp(BAND_ROWS), grp(BAND_ROWS)]
        args += [past_k, past_v]
    in_specs.append(pl.BlockSpec((None, ATT_GW, BAND), lambda b_, g_: (g_, 0, 0)))
    args.append(bias)
    return pl.pallas_call(
        functools.partial(_att_core_kernel, first_pos, has_past),
        grid=(b, ATT_NGROUPS),
        in_specs=in_specs,
        out_specs=grp(t),
        out_shape=jax.ShapeDtypeStruct((b, t, d), BF16),
        scratch_shapes=[pltpu.VMEM((BAND_ROWS + t, ATT_GW), BF16)] * 2,
        compiler_params=_cparams(2),
        name=name,
    )(*args)


def _att_fused_kernel(first_keep_tile, x_ref, sh_ref, sc_ref, gm_ref, ng_ref, wqkv_ref, p_ref,
                      qg_ref, kg_ref, bias_ref, wo_ref, o_ref, kk_ref, vk_ref,
                      q_scr, kbuf, vbuf, a_scr):
    tt, d = x_ref.shape
    t_idx = pl.program_id(1)

    @pl.when(t_idx == 0)
    def _():
        kbuf[0:BAND_ROWS, :] = jnp.zeros((BAND_ROWS, d), BF16)
        vbuf[0:BAND_ROWS, :] = jnp.zeros((BAND_ROWS, d), BF16)

    @pl.when(t_idx > 0)
    def _():
        kbuf[0:BAND_ROWS, :] = kbuf[tt:tt + BAND_ROWS, :]
        vbuf[0:BAND_ROWS, :] = vbuf[tt:tt + BAND_ROWS, :]

    hb = _norm_mod(x_ref[...][None], ng_ref[...], sc_ref[...], sh_ref[...])[0].astype(BF16)
    q = jnp.dot(hb, wqkv_ref[:, 0:d], preferred_element_type=F32)
    q_scr[...] = (_head_norm(q, p_ref[...], qg_ref[...]) * ATT_Q_SCALE).astype(BF16)
    k = jnp.dot(hb, wqkv_ref[:, d:2 * d], preferred_element_type=F32)
    k = _head_norm(k, p_ref[...], kg_ref[...])
    kbuf[BAND_ROWS:, :] = k.astype(BF16)
    v = jnp.dot(hb, wqkv_ref[:, 2 * d:3 * d], preferred_element_type=F32)
    vbuf[BAND_ROWS:, :] = v.astype(BF16)

    @pl.when(t_idx >= first_keep_tile)
    def _():
        kk_ref[...] = k
        vk_ref[...] = v

    masks = _head_block_diag()

    def make_chunk(masked):
        def chunk(n, carry):
            r0 = pl.multiple_of(n * CHUNK, CHUNK)
            for g in range(ATT_NGROUPS):
                gsl = slice(g * ATT_GW, (g + 1) * ATT_GW)
                og = _att_chunk_group(q_scr[pl.ds(r0, CHUNK), gsl], kbuf[pl.ds(r0, BAND), gsl],
                                      vbuf[pl.ds(r0, BAND), gsl], bias_ref[g], masks,
                                      r0 - BAND_ROWS if masked else None)
                a_scr[pl.ds(r0, CHUNK), gsl] = og.astype(BF16)
            return carry
        return chunk

    @pl.when(t_idx == 0)
    def _():
        lax.fori_loop(0, tt // CHUNK, make_chunk(True), 0, unroll=2)

    @pl.when(t_idx > 0)
    def _():
        lax.fori_loop(0, tt // CHUNK, make_chunk(False), 0, unroll=4)

    out = jnp.dot(a_scr[...], wo_ref[...], preferred_element_type=F32)
    o_ref[...] = x_ref[...] + gm_ref[0] * out


def _att_fused(x, mod, norm_g4, w_qkv, w_o, pavg, qg, kg, bias, keep, layer, widx, tt, row0):
    b, t, d = x.shape
    assert tt >= BAND_ROWS and tt % CHUNK == 0 and (tt // CHUNK) % 4 == 0
    assert keep % tt == 0 and t % tt == 0
    first_keep_tile = (t - keep) // tt
    xs = pl.BlockSpec((None, tt, d), lambda b_, t_: (b_, t_, 0))
    vec = pl.BlockSpec((None, 1, d), lambda b_, t_: (widx, 0, 0))
    keep_spec = pl.BlockSpec((None, tt, d),
                             lambda b_, t_: (b_, jnp.maximum(t_ - first_keep_tile, 0), 0))
    f32k = jax.ShapeDtypeStruct((b, keep, d), F32)
    return pl.pallas_call(
        functools.partial(_att_fused_kernel, first_keep_tile),
        grid=(b, t // tt),
        in_specs=[
            xs,
            _mod_spec(layer, 3, 1, row0), _mod_spec(layer, 4, 1, row0), _mod_spec(layer, 5, 1, row0),
            _ng_spec(layer, 1),
            _resident((None, d, 3 * d), lambda b_, t_: (widx, 0, 0)),
            _resident((ATT_GW, ATT_GW), lambda b_, t_: (0, 0)),
            vec, vec,
            _resident((ATT_NGROUPS, ATT_GW, BAND), lambda b_, t_: (0, 0, 0)),
            _resident((None, d, d), lambda b_, t_: (widx, 0, 0)),
        ],
        out_specs=[xs, keep_spec, keep_spec],
        out_shape=[jax.ShapeDtypeStruct(x.shape, F32), f32k, f32k],
        scratch_shapes=[pltpu.VMEM((tt, d), BF16), pltpu.VMEM((BAND_ROWS + tt, d), BF16),
                        pltpu.VMEM((BAND_ROWS + tt, d), BF16), pltpu.VMEM((tt, d), BF16)],
        compiler_params=_cparams(2),
        name=f"att_l{layer}",
    )(x, mod, mod, mod, norm_g4, w_qkv, pavg, qg, kg, bias, w_o)


def _att_bias_kernel(rb_ref, o_ref):
    rb = rb_ref[...]
    hi = rb.astype(BF16)
    r1 = rb - hi.astype(F32)
    mid = r1.astype(BF16)
    lo = (r1 - mid.astype(F32)).astype(BF16)
    r = lax.broadcasted_iota(jnp.int32, (REL_SIZE, REL_BASE_W), 0)
    u = lax.broadcasted_iota(jnp.int32, (REL_SIZE, REL_BASE_W), 1)
    rel = jnp.clip(BAND_ROWS + CHUNK - 1 - u, REL_MIN, REL_MAX) - REL_MIN
    sel = jnp.where(r == rel, 1.0, 0.0).astype(BF16)
    base = (jnp.dot(hi, sel, preferred_element_type=F32)
            + jnp.dot(mid, sel, preferred_element_type=F32)
            + jnp.dot(lo, sel, preferred_element_type=F32)) * LOG2E
    for i in range(CHUNK):
        o_ref[i] = base[:, CHUNK - 1 - i:CHUNK - 1 - i + BAND]


def _att_bias_table(rel_bias):
    tab = pl.pallas_call(
        _att_bias_kernel,
        out_shape=jax.ShapeDtypeStruct((CHUNK, ATT_HEADS, BAND), F32),
        name="att_bias",
    )(rel_bias)
    return tab.transpose(1, 0, 2).reshape(ATT_NGROUPS, ATT_GW, BAND)


def _rope_tables(pos0, t):
    inv = ROPE_BASE ** (-np.arange(ROPE_HALF, dtype=np.float64) / ROPE_HALF)
    ang = (pos0 + np.arange(t, dtype=np.float64))[:, None] * inv[None, :]
    return jnp.asarray(np.cos(ang), F32), jnp.asarray(np.sin(ang), F32)


def _trunk(x, mod, row0, pos0, bb, tt, ret_lc, ret_states, pool_bufs, att_k_cache, att_v_cache, p):
    b, t, d = x.shape
    new_pool, new_k, new_v = [], [], []
    ret_all = None
    cos, sin = _rope_tables(pos0, t)
    tt_ffn = FFN_ROWS if (bb == 1 and t % FFN_ROWS == 0) else tt
    bb_ffn = bb
    if tt == t and FFN_ROWS % t == 0:
        wide = min(b, FFN_ROWS // t)
        if b % wide == 0 and row0 % wide == 0:
            bb_ffn = wide
    for i in range(DEPTH):
        x = _ffn(x, mod, p['norm_g'], p['ffn1_w_in'], p['ffn1_w_out'], i, 1, bb_ffn, tt_ffn, row0)
        j = i // N_MIXERS
        kind = i % N_MIXERS
        if kind == 0:
            gn = p['ret_gn_g'][j].reshape(1, 2 * d)
            if ret_states is None:
                x, ret_all = _ret_fused(x, mod, p['norm_g'], p['ret_w_in'], p['ret_w_out'], gn,
                                        cos, sin, ret_all, ret_lc, max(tt, ret_lc),
                                        i, j, row0)
            else:
                q, k, v, g = _ret_proj(x, mod, p['norm_g'], p['ret_w_in'], cos, sin, ret_lc,
                                       i, j, bb, tt, row0)
                y, ret_all = _ret_core(q, k, v, g, ret_states, ret_all, j, gn, ret_lc,
                                       f"ret_core_l{i}")
                x = _matres(y, x, mod, p['ret_w_out'], i, j, bb, tt, row0, f"ret_out_l{i}")
        elif kind == 1:
            if pool_bufs is None:
                buf16 = jnp.zeros((b, POOL_CARRY, d), F32)
            else:
                buf16 = jnp.pad(pool_bufs[j], ((0, 0), (POOL_CARRY - POOL_BUF, 0), (0, 0)))
            if bb_ffn == 1:
                x, nb = _pool_ffn(x, mod, p['norm_g'], buf16, p['pool_w'], p['pool_b'],
                                  p['pool_scale'], p['ffn2_w_in'], p['ffn2_w_out'],
                                  i, j, tt_ffn, row0, pos0)
                new_pool.append(nb)
                continue
            x, nb = _pool(x, mod, p['norm_g'], buf16, p['pool_w'], p['pool_b'], p['pool_scale'],
                          i, j, min(tt, t), row0, pos0)
            new_pool.append(nb)
        else:
            keep = min(BAND_ROWS, t)
            if att_k_cache is None and pos0 == 0 and tt >= BAND_ROWS:
                x, k_keep, v_keep = _att_fused(
                    x, mod, p['norm_g'], p['att_w_qkv'], p['att_w_o'], p['att_pavg'],
                    p['att_q_g'], p['att_k_g'], p['att_bias'][j], keep, i, j, tt, row0)
            else:
                q, kb, vb, k_keep, v_keep = _att_proj(
                    x, mod, p['norm_g'], p['att_w_qkv'], p['att_pavg'], p['att_q_g'],
                    p['att_k_g'], keep, i, j, bb, tt, row0)
                if att_k_cache is None:
                    past_k = past_v = None
                else:
                    past_k = att_k_cache[j].reshape(b, BAND_ROWS, d)
                    past_v = att_v_cache[j].reshape(b, BAND_ROWS, d)
                o = _att_core(q, kb, vb, past_k, past_v, p['att_bias'][j], pos0 - BAND_ROWS,
                              f"att_core_l{i}")
                x = _matres(o, x, mod, p['att_w_o'], i, j, bb, tt, row0, f"att_out_l{i}")
            new_k.append(k_keep.reshape(b, keep, ATT_HEADS, ATT_HD))
            new_v.append(v_keep.reshape(b, keep, ATT_HEADS, ATT_HD))
        x = _ffn(x, mod, p['norm_g'], p['ffn2_w_in'], p['ffn2_w_out'], i, 2, bb_ffn, tt_ffn, row0)
    return x, ret_all, jnp.stack(new_pool), jnp.stack(new_k), jnp.stack(new_v)


def kernel(x_prompt, x_sample, c_prompt, c_sample, state_ret, state_pool, cache_att_k, cache_att_v, norm_g, ada_w, ada_b, ffn1_w_in, ffn1_w_out, ffn2_w_in, ffn2_w_out, ret_w_in, ret_w_out, ret_gn_g, pool_w, pool_b, pool_scale, att_w_qkv, att_w_o, att_q_g, att_k_g, att_rel_bias):
    assert cache_att_k.shape[2] == BAND_ROWS, "key cache must hold exactly the left band"
    bp, tp, d = x_prompt.shape
    bs, ts, _ = x_sample.shape
    n_att = att_w_qkv.shape[0]
    head_id = np.arange(ATT_GW) // ATT_HD
    pavg = jnp.asarray((head_id[:, None] == head_id[None, :]) / ATT_HD, BF16)
    p = {
        'norm_g': norm_g.reshape(DEPTH, 3, 1, d),
        'ffn1_w_in': _interleave_gate_up(ffn1_w_in), 'ffn1_w_out': ffn1_w_out.astype(BF16),
        'ffn2_w_in': _interleave_gate_up(ffn2_w_in), 'ffn2_w_out': ffn2_w_out.astype(BF16),
        'ret_w_in': ret_w_in.astype(BF16), 'ret_w_out': ret_w_out.astype(BF16),
        'ret_gn_g': ret_gn_g,
        'pool_w': pool_w.astype(BF16),
        'pool_b': pool_b.reshape(-1, 1, d), 'pool_scale': pool_scale.reshape(-1, 1, d),
        'att_w_qkv': att_w_qkv.astype(BF16), 'att_w_o': att_w_o.astype(BF16),
        'att_q_g': jnp.tile(att_q_g, (1, ATT_HEADS)).reshape(n_att, 1, d),
        'att_k_g': jnp.tile(att_k_g, (1, ATT_HEADS)).reshape(n_att, 1, d),
        'att_pavg': pavg,
        'att_bias': [_att_bias_table(att_rel_bias[j]) for j in range(n_att)],
    }
    mod = _ada(jnp.concatenate([c_prompt, c_sample], axis=0), ada_w, ada_b)

    tt_p = min(PROMPT_ROWS, tp)
    y_prompt, ret_p, pool_p, k_p, v_p = _trunk(
        x_prompt, mod, 0, 0, 1, tt_p, min(RET_CHUNK_PROMPT, tp), None, None, None, None, p)
    bb_s = max(1, min(bs, PROMPT_ROWS // ts))
    y_sample, ret_s, pool_s, k_s, v_s = _trunk(
        x_sample, mod, bp, PAST_LEN, bb_s, ts, ts, state_ret, state_pool,
        cache_att_k, cache_att_v, p)
    return (y_prompt, y_sample, ret_p, ret_s, pool_p, pool_s, k_p, v_p, k_s, v_s)
```

```python
import functools

import numpy as np
import jax
import jax.numpy as jnp
from jax import lax
from jax.experimental import pallas as pl
from jax.experimental.pallas import tpu as pltpu

F32 = jnp.float32
BF16 = jnp.bfloat16

LANES = 128
D_MODEL = 1024
DEPTH = 4
D_FF = 2816
EPS = 1e-6
CHUNK = 64
PAST_LEN = 2048
N_MIXERS = 3
RET_HEADS = 4
RET_DK = D_MODEL // RET_HEADS
RET_DV = 2 * D_MODEL // RET_HEADS
ROPE_BASE = 10000.0
ROPE_HALF = RET_DK // 2
POOL_WINDOWS = (2, 4, 8, 16)
POOL_GROUPS = 4
POOL_GC = D_MODEL // POOL_GROUPS
POOL_BUF = 15
POOL_CARRY = 16
ATT_HEADS = 16
ATT_HD = D_MODEL // ATT_HEADS
ATT_GROUP = 4
ATT_GW = ATT_GROUP * ATT_HD
ATT_NGROUPS = ATT_HEADS // ATT_GROUP
LEFT_CHUNKS = 8
BAND_ROWS = LEFT_CHUNKS * CHUNK
BAND = BAND_ROWS + CHUNK
REL_MIN = -(CHUNK - 1)
REL_MAX = 256
REL_SIZE = REL_MAX - REL_MIN + 1
REL_BASE_W = 640
NEG_INF = -1e30
LOG2E = 1.4426950408889634
ATT_Q_SCALE = ATT_HD ** -0.5 * LOG2E
N_MOD = 9

VMEM_LIMIT_BYTES = 56 * 1024 * 1024
PROMPT_ROWS = 512
FFN_ROWS = 1024
RET_CHUNK_PROMPT = 256
ATT_UNROLL = 8


def _cparams(n_grid):
    return pltpu.CompilerParams(
        dimension_semantics=("arbitrary",) * n_grid,
        vmem_limit_bytes=VMEM_LIMIT_BYTES,
    )


def _resident(block_shape, index_map):
    return pl.BlockSpec(block_shape, index_map, pipeline_mode=pl.Buffered(1))


def _norm_mod(x, ng, sc, sh):
    ms = jnp.mean(x * x, axis=-1, keepdims=True)
    y = x * lax.rsqrt(ms + EPS) * ng
    return y * (1.0 + sc) + sh


def _silu(x):
    return x * (1.0 / (1.0 + jnp.exp(-x)))


def _ada_kernel(c_ref, w_ref, b_ref, o_ref):
    c = _silu(c_ref[...]).astype(BF16)
    w = w_ref[...].astype(BF16)
    o_ref[...] = jnp.dot(c, w, preferred_element_type=F32) + b_ref[...]


def _ada(c_all, ada_w, ada_b):
    n = c_all.shape[0]
    b4 = ada_b.reshape(DEPTH, N_MOD, 1, D_MODEL)
    out = pl.pallas_call(
        _ada_kernel,
        grid=(DEPTH, N_MOD),
        in_specs=[
            pl.BlockSpec((n, D_MODEL), lambda i, j: (0, 0)),
            pl.BlockSpec((None, D_MODEL, D_MODEL), lambda i, j: (i, 0, j)),
            pl.BlockSpec((None, None, 1, D_MODEL), lambda i, j: (i, j, 0, 0)),
        ],
        out_specs=pl.BlockSpec((None, None, n, D_MODEL), lambda i, j: (i, j, 0, 0)),
        out_shape=jax.ShapeDtypeStruct((DEPTH, N_MOD, n, D_MODEL), F32),
        compiler_params=_cparams(2),
        name="ada_mod",
    )(c_all, ada_w, b4)
    return out.reshape(DEPTH, N_MOD, n, 1, D_MODEL)


def _mod_spec(layer, j, bb, row0):
    blk0 = row0 // bb
    return pl.BlockSpec((None, None, bb, 1, D_MODEL),
                        lambda b, t: (layer, j, blk0 + b, 0, 0))


def _ng_spec(layer, j):
    return pl.BlockSpec((None, None, 1, D_MODEL), lambda b, t: (layer, j, 0, 0))


def _x_spec(bb, tt, width=D_MODEL):
    return pl.BlockSpec((bb, tt, width), lambda b, t: (b, t, 0))


GATE_UP_TILES = 2


def _gate_up_kernel(g_ref, u_ref, o_ref):
    for i in range(GATE_UP_TILES):
        src = slice(i * LANES, (i + 1) * LANES)
        o_ref[:, 2 * i * LANES:(2 * i + 1) * LANES] = g_ref[:, src].astype(BF16)
        o_ref[:, (2 * i + 1) * LANES:(2 * i + 2) * LANES] = u_ref[:, src].astype(BF16)


def _interleave_gate_up(w_in):
    nl, d, _ = w_in.shape
    width = GATE_UP_TILES * LANES
    steps = D_FF // width
    assert D_FF % width == 0
    return pl.pallas_call(
        _gate_up_kernel,
        grid=(nl, steps),
        in_specs=[pl.BlockSpec((None, d, width), lambda l, j: (l, 0, j)),
                  pl.BlockSpec((None, d, width), lambda l, j: (l, 0, steps + j))],
        out_specs=pl.BlockSpec((None, d, 2 * width), lambda l, j: (l, 0, j)),
        out_shape=jax.ShapeDtypeStruct((nl, d, 2 * D_FF), BF16),
        compiler_params=_cparams(2),
        name="ffn_gate_up_layout",
    )(w_in, w_in)


def _swiglu(hb, win_ref, wout_ref):
    gu = jnp.dot(hb, win_ref[...], preferred_element_type=F32)
    act = jnp.concatenate(
        [_silu(gu[:, 2 * j * LANES:(2 * j + 1) * LANES]) * gu[:, (2 * j + 1) * LANES:(2 * j + 2) * LANES]
         for j in range(D_FF // LANES)], axis=1).astype(BF16)
    return jnp.dot(act, wout_ref[...], preferred_element_type=F32)


def _ffn_kernel(x_ref, sh_ref, sc_ref, g_ref, ng_ref, win_ref, wout_ref, o_ref):
    bb, tt, d = x_ref.shape
    x = x_ref[...]
    h = _norm_mod(x, ng_ref[...], sc_ref[...], sh_ref[...])
    y = _swiglu(h.reshape(bb * tt, d).astype(BF16), win_ref, wout_ref)
    o_ref[...] = x + (0.5 * g_ref[...]) * y.reshape(bb, tt, d)


def _ffn(x, mod, norm_g4, w_in, w_out, layer, which, bb, tt, row0):
    b, t, d = x.shape
    j0 = 0 if which == 1 else 6
    nj = 0 if which == 1 else 2
    return pl.pallas_call(
        _ffn_kernel,
        grid=(b // bb, t // tt),
        in_specs=[
            _x_spec(bb, tt),
            _mod_spec(layer, j0, bb, row0),
            _mod_spec(layer, j0 + 1, bb, row0),
            _mod_spec(layer, j0 + 2, bb, row0),
            _ng_spec(layer, nj),
            _resident((None, D_MODEL, 2 * D_FF), lambda b_, t_: (layer, 0, 0)),
            _resident((None, D_FF, D_MODEL), lambda b_, t_: (layer, 0, 0)),
        ],
        out_specs=_x_spec(bb, tt),
        out_shape=jax.ShapeDtypeStruct(x.shape, F32),
        compiler_params=_cparams(2),
        name=f"ffn{which}_l{layer}",
    )(x, mod, mod, mod, norm_g4, w_in, w_out)


def _matres_kernel(a_ref, x_ref, g_ref, w_ref, o_ref):
    bb, tt, k = a_ref.shape
    y = jnp.dot(a_ref[...].reshape(bb * tt, k), w_ref[...], preferred_element_type=F32)
    o_ref[...] = x_ref[...] + g_ref[...] * y.reshape(bb, tt, D_MODEL)


def _matres(a, x, mod, w, layer, widx, bb, tt, row0, name):
    b, t, k = a.shape
    return pl.pallas_call(
        _matres_kernel,
        grid=(b // bb, t // tt),
        in_specs=[
            _x_spec(bb, tt, k),
            _x_spec(bb, tt),
            _mod_spec(layer, 5, bb, row0),
            _resident((None, k, D_MODEL), lambda b_, t_: (widx, 0, 0)),
        ],
        out_specs=_x_spec(bb, tt),
        out_shape=jax.ShapeDtypeStruct(x.shape, F32),
        compiler_params=_cparams(2),
        name=name,
    )(a, x, mod, w)


def _ret_proj_kernel(x_ref, sh_ref, sc_ref, ng_ref, w_ref, cos_ref, sin_ref, qs_ref, ks_ref,
                     q_ref, k_ref, v_ref, g_ref):
    bb, tt, d = x_ref.shape
    h = _norm_mod(x_ref[...], ng_ref[...], sc_ref[...], sh_ref[...])
    hb = h.reshape(bb * tt, d).astype(BF16)
    cos = cos_ref[...][None]
    sin = sin_ref[...][None]
    for idx, (o_ref, s_ref) in enumerate(((q_ref, qs_ref), (k_ref, ks_ref))):
        y = jnp.dot(hb, w_ref[:, idx * d:(idx + 1) * d], preferred_element_type=F32)
        y = y.reshape(bb, tt, d)
        for hd in range(RET_HEADS):
            lo = hd * RET_DK
            mid = lo + ROPE_HALF
            x1 = y[:, :, lo:mid]
            x2 = y[:, :, mid:lo + RET_DK]
            o_ref[:, :, lo:mid] = ((x1 * cos - x2 * sin) * s_ref[:, lo:mid][None]).astype(BF16)
            o_ref[:, :, mid:lo + RET_DK] = (
                (x1 * sin + x2 * cos) * s_ref[:, mid:lo + RET_DK][None]).astype(BF16)
    v = jnp.dot(hb, w_ref[:, 2 * d:4 * d], preferred_element_type=F32)
    v_ref[...] = v.reshape(bb, tt, 2 * d).astype(BF16)
    g = jnp.dot(hb, w_ref[:, 4 * d:6 * d], preferred_element_type=F32)
    g_ref[...] = _silu(g).reshape(bb, tt, 2 * d).astype(BF16)


def _ret_gamma():
    return 1.0 - 2.0 ** (-5.0 - np.arange(RET_HEADS, dtype=np.float64))


def _ret_scale_tables(tt, lc):
    gamma = _ret_gamma()
    i = (np.arange(tt) % lc).astype(np.float64)[:, None]
    g_lane = np.repeat(gamma, RET_DK)[None, :]
    qs = g_lane ** (i + 1.0)
    ks = (RET_DK ** -0.5) * g_lane ** (-(i + 1.0))
    return jnp.asarray(qs, F32), jnp.asarray(ks, F32)


def _ret_proj(x, mod, norm_g4, w_in, cos, sin, lc, layer, widx, bb, tt, row0):
    b, t, d = x.shape
    assert tt % lc == 0
    qs, ks = _ret_scale_tables(tt, lc)
    tab = pl.BlockSpec((tt, ROPE_HALF), lambda b_, t_: (t_, 0))
    scale = _resident((tt, d), lambda b_, t_: (0, 0))
    return pl.pallas_call(
        _ret_proj_kernel,
        grid=(b // bb, t // tt),
        in_specs=[
            _x_spec(bb, tt),
            _mod_spec(layer, 3, bb, row0),
            _mod_spec(layer, 4, bb, row0),
            _ng_spec(layer, 1),
            _resident((None, D_MODEL, 6 * D_MODEL), lambda b_, t_: (widx, 0, 0)),
            tab, tab, scale, scale,
        ],
        out_specs=[_x_spec(bb, tt), _x_spec(bb, tt),
                   _x_spec(bb, tt, 2 * d), _x_spec(bb, tt, 2 * d)],
        out_shape=[jax.ShapeDtypeStruct((b, t, d), BF16), jax.ShapeDtypeStruct((b, t, d), BF16),
                   jax.ShapeDtypeStruct((b, t, 2 * d), BF16), jax.ShapeDtypeStruct((b, t, 2 * d), BF16)],
        compiler_params=_cparams(2),
        name=f"ret_proj_l{layer}",
    )(x, mod, mod, norm_g4, w_in, cos, sin, qs, ks)


def _ret_head(qh, kh, vh, s, tril, gn, cdec_h):
    inner = lax.dot_general(qh, kh, (((1,), (1,)), ((), ())), preferred_element_type=F32) * tril
    o = (jnp.dot(inner.astype(BF16), vh, preferred_element_type=F32)
         + jnp.dot(qh, s.astype(BF16), preferred_element_type=F32))
    s_new = (s + lax.dot_general(kh, vh, (((0,), (0,)), ((), ())),
                                 preferred_element_type=F32)) * cdec_h
    mu = jnp.mean(o, axis=-1, keepdims=True)
    oc = o - mu
    var = jnp.mean(oc * oc, axis=-1, keepdims=True)
    return oc * lax.rsqrt(var + EPS) * gn, s_new


def _init_state_stack(s_ref, slot, prev_ref, s0_ref):
    if slot:
        s_ref[0:slot] = prev_ref[...]
    if s0_ref is None:
        s_ref[slot] = jnp.zeros(s_ref.shape[1:], F32)
    else:
        s_ref[slot] = s0_ref[...]


def _ret_core_kernel(cdec, has_state, slot, *refs):
    refs = list(refs)
    q_ref, k_ref, v_ref, g_ref = refs[:4]
    s0_ref = refs[4] if has_state else None
    tril_ref, gn_ref = refs[4 + has_state:6 + has_state]
    prev_ref = refs[6 + has_state] if slot else None
    y_ref, s_ref = refs[6 + has_state + bool(slot):]

    @pl.when(pl.program_id(1) == 0)
    def _():
        _init_state_stack(s_ref, slot, prev_ref, s0_ref)

    for hd in range(RET_HEADS):
        ksl = slice(hd * RET_DK, (hd + 1) * RET_DK)
        vsl = slice(hd * RET_DV, (hd + 1) * RET_DV)
        on, s_new = _ret_head(q_ref[:, ksl], k_ref[:, ksl], v_ref[:, vsl], s_ref[slot, hd],
                              tril_ref[...], gn_ref[:, vsl], cdec[hd])
        s_ref[slot, hd] = s_new
        y_ref[:, vsl] = (g_ref[:, vsl].astype(F32) * on).astype(BF16)


def _state_stack_specs(slot, b):
    blk = lambda n: pl.BlockSpec((n, None, RET_HEADS, RET_DK, RET_DV),
                                 lambda b_, t_: (0, b_, 0, 0, 0))
    shape = jax.ShapeDtypeStruct((slot + 1, b, RET_HEADS, RET_DK, RET_DV), F32)
    return blk(slot + 1), shape, (blk(slot) if slot else None)


def _ret_core(q, k, v, g, s0_all, s_prev, slot, gn_g, lc, name):
    b, t, d = q.shape
    cdec = tuple(float(c) for c in _ret_gamma() ** float(lc))
    tril = jnp.asarray(np.tril(np.ones((lc, lc))), F32)
    has_state = s0_all is not None
    row = lambda width: pl.BlockSpec((None, lc, width), lambda b_, t_: (b_, t_, 0))
    s_spec, s_shape, prev_spec = _state_stack_specs(slot, b)
    in_specs = [row(d), row(d), row(2 * d), row(2 * d)]
    args = [q, k, v, g]
    if has_state:
        in_specs.append(pl.BlockSpec((None, None, RET_HEADS, RET_DK, RET_DV),
                                     lambda b_, t_: (slot, b_, 0, 0, 0)))
        args.append(s0_all)
    in_specs += [_resident((lc, lc), lambda b_, t_: (0, 0)),
                 pl.BlockSpec((1, 2 * d), lambda b_, t_: (0, 0))]
    args += [tril, gn_g]
    if slot:
        in_specs.append(prev_spec)
        args.append(s_prev)
    return pl.pallas_call(
        functools.partial(_ret_core_kernel, cdec, has_state, slot),
        grid=(b, t // lc),
        in_specs=in_specs,
        out_specs=[row(2 * d), s_spec],
        out_shape=[jax.ShapeDtypeStruct((b, t, 2 * d), BF16), s_shape],
        compiler_params=_cparams(2),
        name=name,
    )(*args)


def _ret_fused_kernel(cdec, lc, slot, *refs):
    refs = list(refs)
    (x_ref, sh_ref, sc_ref, gm_ref, ng_ref, win_ref, wout_ref, cos_ref, sin_ref, qs_ref, ks_ref,
     tril_ref, gn_ref) = refs[:13]
    prev_ref = refs[13] if slot else None
    o_ref, s_ref, q_scr, k_scr, v_scr, y_scr = refs[13 + bool(slot):]
    tt, d = x_ref.shape

    @pl.when(pl.program_id(1) == 0)
    def _():
        _init_state_stack(s_ref, slot, prev_ref, None)

    hb = _norm_mod(x_ref[...][None], ng_ref[...], sc_ref[...], sh_ref[...])[0].astype(BF16)
    cos = cos_ref[...]
    sin = sin_ref[...]
    for idx, (scr, tab) in enumerate(((q_scr, qs_ref), (k_scr, ks_ref))):
        y = jnp.dot(hb, win_ref[:, idx * d:(idx + 1) * d], preferred_element_type=F32)
        for hd in range(RET_HEADS):
            lo = hd * RET_DK
            mid = lo + ROPE_HALF
            hi = lo + RET_DK
            x1 = y[:, lo:mid]
            x2 = y[:, mid:hi]
            scr[:, lo:mid] = ((x1 * cos - x2 * sin) * tab[:, lo:mid]).astype(BF16)
            scr[:, mid:hi] = ((x1 * sin + x2 * cos) * tab[:, mid:hi]).astype(BF16)
    v_scr[...] = jnp.dot(hb, win_ref[:, 2 * d:4 * d], preferred_element_type=F32).astype(BF16)
    for hd in range(RET_HEADS):
        ksl = slice(hd * RET_DK, (hd + 1) * RET_DK)
        vsl = slice(hd * RET_DV, (hd + 1) * RET_DV)
        g = jnp.dot(hb, win_ref[:, 4 * d + hd * RET_DV:4 * d + (hd + 1) * RET_DV],
                    preferred_element_type=F32)
        for c in range(tt // lc):
            rows = slice(c * lc, (c + 1) * lc)
            on, s_new = _ret_head(q_scr[rows, ksl], k_scr[rows, ksl], v_scr[rows, vsl],
                                  s_ref[slot, hd], tril_ref[...], gn_ref[:, vsl], cdec[hd])
            s_ref[slot, hd] = s_new
            y_scr[rows, vsl] = (_silu(g[rows]) * on).astype(BF16)
    out = jnp.dot(y_scr[...], wout_ref[...], preferred_element_type=F32)
    o_ref[...] = x_ref[...] + gm_ref[0] * out


def _ret_fused(x, mod, norm_g4, w_in, w_out, gn_g, cos, sin, s_prev, lc, tt, layer, widx, row0):
    b, t, d = x.shape
    assert tt % lc == 0 and t % tt == 0
    cdec = tuple(float(c) for c in _ret_gamma() ** float(lc))
    tril = jnp.asarray(np.tril(np.ones((lc, lc))), F32)
    qs, ks = _ret_scale_tables(tt, lc)
    slot = widx
    xs = pl.BlockSpec((None, tt, d), lambda b_, t_: (b_, t_, 0))
    tab = pl.BlockSpec((tt, ROPE_HALF), lambda b_, t_: (t_, 0))
    const2 = lambda shape: _resident(shape, lambda b_, t_: (0, 0))
    s_spec, s_shape, prev_spec = _state_stack_specs(slot, b)
    in_specs = [
        xs,
        _mod_spec(layer, 3, 1, row0), _mod_spec(layer, 4, 1, row0), _mod_spec(layer, 5, 1, row0),
        _ng_spec(layer, 1),
        _resident((None, d, 6 * d), lambda b_, t_: (widx, 0, 0)),
        _resident((None, 2 * d, d), lambda b_, t_: (widx, 0, 0)),
        tab, tab, const2((tt, d)), const2((tt, d)), const2((lc, lc)),
        pl.BlockSpec((1, 2 * d), lambda b_, t_: (0, 0)),
    ]
    args = [x, mod, mod, mod, norm_g4, w_in, w_out, cos, sin, qs, ks, tril, gn_g]
    if slot:
        in_specs.append(prev_spec)
        args.append(s_prev)
    return pl.pallas_call(
        functools.partial(_ret_fused_kernel, cdec, lc, slot),
        grid=(b, t // tt),
        in_specs=in_specs,
        out_specs=[xs, s_spec],
        out_shape=[jax.ShapeDtypeStruct(x.shape, F32), s_shape],
        scratch_shapes=[pltpu.VMEM((tt, d), BF16), pltpu.VMEM((tt, d), BF16),
                        pltpu.VMEM((tt, 2 * d), BF16), pltpu.VMEM((tt, 2 * d), BF16)],
        compiler_params=_cparams(2),
        name=f"ret_l{layer}",
    )(*args)


def _pool_mix(pos0, x, sh_ref, sc_ref, gm_ref, ng_ref, buf_ref, w_ref, pb_ref, ps_ref,
              nb_ref, carry_ref):
    tt, d = x.shape
    t_idx = pl.program_id(1)

    @pl.when(t_idx == 0)
    def _():
        carry_ref[...] = buf_ref[...]

    h = _norm_mod(x[None], ng_ref[...], sc_ref[...], sh_ref[...])[0]
    ext = jnp.concatenate([carry_ref[...], h], axis=0)
    carry_ref[...] = h[tt - POOL_CARRY:, :]
    nb_ref[...] = h[tt - POOL_BUF:, :]

    pos = pos0 + t_idx * tt + lax.broadcasted_iota(jnp.int32, (tt, 1), 0)
    run = ext
    w = 1
    outs = []
    for gi, win in enumerate(POOL_WINDOWS):
        while w < win:
            n = run.shape[0]
            run = run[w:, :] + run[:n - w, :]
            w *= 2
        wsum = run[run.shape[0] - tt:, :POOL_GC]
        inv_cnt = 1.0 / jnp.minimum(pos + 1, win).astype(F32)
        lo = gi * POOL_GC
        pooled = wsum * inv_cnt - h[:, lo:lo + POOL_GC]
        outs.append(jnp.dot(pooled.astype(BF16), w_ref[gi], preferred_element_type=F32))
        run = run[:, POOL_GC:]
    y = (jnp.concatenate(outs, axis=1) + pb_ref[...]) * ps_ref[...]
    return x + gm_ref[0] * y


def _pool_kernel(pos0, x_ref, sh_ref, sc_ref, gm_ref, ng_ref, buf_ref, w_ref, pb_ref, ps_ref,
                 o_ref, nb_ref, carry_ref):
    o_ref[...] = _pool_mix(pos0, x_ref[...], sh_ref, sc_ref, gm_ref, ng_ref, buf_ref, w_ref,
                           pb_ref, ps_ref, nb_ref, carry_ref)


def _pool_ffn_kernel(pos0, x_ref, sh_ref, sc_ref, gm_ref, ng_ref, buf_ref, w_ref, pb_ref, ps_ref,
                     fsh_ref, fsc_ref, fg_ref, fng_ref, win_ref, wout_ref,
                     o_ref, nb_ref, carry_ref):
    x = _pool_mix(pos0, x_ref[...], sh_ref, sc_ref, gm_ref, ng_ref, buf_ref, w_ref,
                  pb_ref, ps_ref, nb_ref, carry_ref)
    hb = _norm_mod(x[None], fng_ref[...], fsc_ref[...], fsh_ref[...])[0].astype(BF16)
    o_ref[...] = x + (0.5 * fg_ref[0]) * _swiglu(hb, win_ref, wout_ref)


def _pool_specs(x, layer, widx, tt, row0):
    b, t, d = x.shape
    xs = pl.BlockSpec((None, tt, d), lambda b_, t_: (b_, t_, 0))
    vec = pl.BlockSpec((None, 1, d), lambda b_, t_: (widx, 0, 0))
    in_specs = [
        xs,
        _mod_spec(layer, 3, 1, row0),
        _mod_spec(layer, 4, 1, row0),
        _mod_spec(layer, 5, 1, row0),
        _ng_spec(layer, 1),
        pl.BlockSpec((None, POOL_CARRY, d), lambda b_, t_: (b_, 0, 0)),
        pl.BlockSpec((None, POOL_GROUPS, POOL_GC, POOL_GC), lambda b_, t_: (widx, 0, 0, 0)),
        vec, vec,
    ]
    out_specs = [xs, pl.BlockSpec((None, POOL_BUF, d), lambda b_, t_: (b_, 0, 0))]
    out_shape = [jax.ShapeDtypeStruct(x.shape, F32), jax.ShapeDtypeStruct((b, POOL_BUF, d), F32)]
    return in_specs, out_specs, out_shape


def _pool(x, mod, norm_g4, buf16, w, pb, ps, layer, widx, tt, row0, pos0):
    b, t, d = x.shape
    in_specs, out_specs, out_shape = _pool_specs(x, layer, widx, tt, row0)
    return pl.pallas_call(
        functools.partial(_pool_kernel, pos0),
        grid=(b, t // tt),
        in_specs=in_specs,
        out_specs=out_specs,
        out_shape=out_shape,
        scratch_shapes=[pltpu.VMEM((POOL_CARRY, d), F32)],
        compiler_params=_cparams(2),
        name=f"pool_l{layer}",
    )(x, mod, mod, mod, norm_g4, buf16, w, pb, ps)


def _pool_ffn(x, mod, norm_g4, buf16, w, pb, ps, w_in, w_out, layer, widx, tt, row0, pos0):
    b, t, d = x.shape
    in_specs, out_specs, out_shape = _pool_specs(x, layer, widx, tt, row0)
    in_specs += [
        _mod_spec(layer, 6, 1, row0), _mod_spec(layer, 7, 1, row0), _mod_spec(layer, 8, 1, row0),
        _ng_spec(layer, 2),
        _resident((None, D_MODEL, 2 * D_FF), lambda b_, t_: (layer, 0, 0)),
        _resident((None, D_FF, D_MODEL), lambda b_, t_: (layer, 0, 0)),
    ]
    return pl.pallas_call(
        functools.partial(_pool_ffn_kernel, pos0),
        grid=(b, t // tt),
        in_specs=in_specs,
        out_specs=out_specs,
        out_shape=out_shape,
        scratch_shapes=[pltpu.VMEM((POOL_CARRY, d), F32)],
        compiler_params=_cparams(2),
        name=f"pool_ffn2_l{layer}",
    )(x, mod, mod, mod, norm_g4, buf16, w, pb, ps, mod, mod, mod, norm_g4, w_in, w_out)


def _head_norm(y, pavg, gain):
    y2 = (y * y).astype(BF16)
    ms = jnp.concatenate(
        [jnp.dot(y2[:, g * ATT_GW:(g + 1) * ATT_GW], pavg, preferred_element_type=F32)
         for g in range(ATT_NGROUPS)], axis=1)
    return y * lax.rsqrt(ms + EPS) * gain


def _att_proj_kernel(first_keep_tile, x_ref, sh_ref, sc_ref, ng_ref, w_ref, p_ref, qg_ref, kg_ref,
                     q_ref, kb_ref, vb_ref, kk_ref, vk_ref):
    bb, tt, d = x_ref.shape
    h = _norm_mod(x_ref[...], ng_ref[...], sc_ref[...], sh_ref[...])
    hb = h.reshape(bb * tt, d).astype(BF16)
    q = jnp.dot(hb, w_ref[:, 0:d], preferred_element_type=F32)
    q = _head_norm(q, p_ref[...], qg_ref[...]) * ATT_Q_SCALE
    q_ref[...] = q.reshape(bb, tt, d).astype(BF16)
    k = jnp.dot(hb, w_ref[:, d:2 * d], preferred_element_type=F32)
    k = _head_norm(k, p_ref[...], kg_ref[...]).reshape(bb, tt, d)
    kb_ref[...] = k.astype(BF16)
    v = jnp.dot(hb, w_ref[:, 2 * d:3 * d], preferred_element_type=F32).reshape(bb, tt, d)
    vb_ref[...] = v.astype(BF16)

    @pl.when(pl.program_id(1) >= first_keep_tile)
    def _():
        kk_ref[...] = k
        vk_ref[...] = v


def _att_proj(x, mod, norm_g4, w_qkv, pavg, qg, kg, keep, layer, widx, bb, tt, row0):
    b, t, d = x.shape
    assert keep % tt == 0 and t % tt == 0
    first_keep_tile = (t - keep) // tt
    vec = pl.BlockSpec((None, 1, d), lambda b_, t_: (widx, 0, 0))
    keep_spec = pl.BlockSpec((bb, tt, d),
                             lambda b_, t_: (b_, jnp.maximum(t_ - first_keep_tile, 0), 0))
    b16o = jax.ShapeDtypeStruct((b, t, d), BF16)
    f32k = jax.ShapeDtypeStruct((b, keep, d), F32)
    return pl.pallas_call(
        functools.partial(_att_proj_kernel, first_keep_tile),
        grid=(b // bb, t // tt),
        in_specs=[
            _x_spec(bb, tt),
            _mod_spec(layer, 3, bb, row0),
            _mod_spec(layer, 4, bb, row0),
            _ng_spec(layer, 1),
            _resident((None, d, 3 * d), lambda b_, t_: (widx, 0, 0)),
            _resident((ATT_GW, ATT_GW), lambda b_, t_: (0, 0)),
            vec, vec,
        ],
        out_specs=[_x_spec(bb, tt)] * 3 + [keep_spec] * 2,
        out_shape=[b16o, b16o, b16o, f32k, f32k],
        compiler_params=_cparams(2),
        name=f"att_proj_l{layer}",
    )(x, mod, mod, norm_g4, w_qkv, pavg, qg, kg)


def _head_block_diag():
    rb = lax.broadcasted_iota(jnp.int32, (ATT_GW, ATT_GW), 0) // ATT_HD
    cb = lax.broadcasted_iota(jnp.int32, (ATT_GW, ATT_GW), 1) // ATT_HD
    lane_head = lax.broadcasted_iota(jnp.int32, (CHUNK, ATT_GW), 1) // ATT_HD
    return rb == cb, lane_head


def _att_chunk_group(qc, kb, vb, bias, masks, first_kpos):
    diag, lane_head = masks
    qbd = jnp.where(diag, jnp.concatenate([qc] * ATT_GROUP, axis=0), jnp.zeros((), BF16))
    s = lax.dot_general(qbd, kb, (((1,), (1,)), ((), ())), preferred_element_type=F32) + bias
    if first_kpos is not None:
        kpos = first_kpos + lax.broadcasted_iota(jnp.int32, (1, BAND), 1)
        s = jnp.where(kpos >= 0, s, NEG_INF)
    m = jnp.max(s, axis=-1, keepdims=True)
    p = jnp.exp2(s - m)
    inv = 1.0 / jnp.sum(p, axis=-1, keepdims=True)
    o = jnp.dot(p.astype(BF16), vb, preferred_element_type=F32)
    last = ATT_GROUP - 1
    og = o[last * CHUNK:] * inv[last * CHUNK:]
    for hh in range(last - 1, -1, -1):
        rows = slice(hh * CHUNK, (hh + 1) * CHUNK)
        og = jnp.where(lane_head == hh, o[rows] * inv[rows], og)
    return og


def _att_core_kernel(first_pos, has_past, *refs):
    if has_past:
        q_ref, k_ref, v_ref, pk_ref, pv_ref, bias_ref, o_ref, kbuf, vbuf = refs
    else:
        q_ref, k_ref, v_ref, bias_ref, o_ref, kbuf, vbuf = refs
    t = q_ref.shape[0]
    if has_past:
        kbuf[0:BAND_ROWS, :] = pk_ref[...].astype(BF16)
        vbuf[0:BAND_ROWS, :] = pv_ref[...].astype(BF16)
    else:
        kbuf[0:BAND_ROWS, :] = jnp.zeros((BAND_ROWS, ATT_GW), BF16)
        vbuf[0:BAND_ROWS, :] = jnp.zeros((BAND_ROWS, ATT_GW), BF16)
    kbuf[BAND_ROWS:, :] = k_ref[...]
    vbuf[BAND_ROWS:, :] = v_ref[...]
    masks = _head_block_diag()

    def make_chunk(masked):
        def chunk(n, carry):
            r0 = pl.multiple_of(n * CHUNK, CHUNK)
            og = _att_chunk_group(q_ref[pl.ds(r0, CHUNK), :], kbuf[pl.ds(r0, BAND), :],
                                  vbuf[pl.ds(r0, BAND), :], bias_ref[...], masks,
                                  first_pos + r0 if masked else None)
            o_ref[pl.ds(r0, CHUNK), :] = og.astype(BF16)
            return carry
        return chunk

    n_chunks = t // CHUNK
    n_masked = min(n_chunks, max(0, -(first_pos // CHUNK)))
    for lo, hi, masked in ((0, n_masked, True), (n_masked, n_chunks, False)):
        if hi > lo:
            trips = hi - lo
            unroll = next(u for u in (ATT_UNROLL, 2, 1) if trips % u == 0)
            lax.fori_loop(lo, hi, make_chunk(masked), 0, unroll=unroll)


def _att_core(q, k, v, past_k, past_v, bias, first_pos, name):
    b, t, d = q.shape
    has_past = past_k is not None
    grp = lambda rows: pl.BlockSpec((None, rows, ATT_GW), lambda b_, g_: (b_, 0, g_))
    in_specs = [grp(t), grp(t), grp(t)]
    args = [q, k, v]
    if has_past:
        in_specs += [grp(BAND_ROWS), grp(BAND_ROWS)]
        args += [past_k, past_v]
    in_specs.append(pl.BlockSpec((None, ATT_GW, BAND), lambda b_, g_: (g_, 0, 0)))
    args.append(bias)
    return pl.pallas_call(
        functools.partial(_att_core_kernel, first_pos, has_past),
        grid=(b, ATT_NGROUPS),
        in_specs=in_specs,
        out_specs=grp(t),
        out_shape=jax.ShapeDtypeStruct((b, t, d), BF16),
        scratch_shapes=[pltpu.VMEM((BAND_ROWS + t, ATT_GW), BF16)] * 2,
        compiler_params=_cparams(2),
        name=name,
    )(*args)


def _att_fused_kernel(first_keep_tile, x_ref, sh_ref, sc_ref, gm_ref, ng_ref, wqkv_ref, p_ref,
                      qg_ref, kg_ref, bias_ref, wo_ref, o_ref, kk_ref, vk_ref,
                      q_scr, kbuf, vbuf, a_scr):
    tt, d = x_ref.shape
    t_idx = pl.program_id(1)

    @pl.when(t_idx == 0)
    def _():
        kbuf[0:BAND_ROWS, :] = jnp.zeros((BAND_ROWS, d), BF16)
        vbuf[0:BAND_ROWS, :] = jnp.zeros((BAND_ROWS, d), BF16)

    @pl.when(t_idx > 0)
    def _():
        kbuf[0:BAND_ROWS, :] = kbuf[tt:tt + BAND_ROWS, :]
        vbuf[0:BAND_ROWS, :] = vbuf[tt:tt + BAND_ROWS, :]

    hb = _norm_mod(x_ref[...][None], ng_ref[...], sc_ref[...], sh_ref[...])[0].astype(BF16)
    q = jnp.dot(hb, wqkv_ref[:, 0:d], preferred_element_type=F32)
    q_scr[...] = (_head_norm(q, p_ref[...], qg_ref[...]) * ATT_Q_SCALE).astype(BF16)
    k = jnp.dot(hb, wqkv_ref[:, d:2 * d], preferred_element_type=F32)
    k = _head_norm(k, p_ref[...], kg_ref[...])
    kbuf[BAND_ROWS:, :] = k.astype(BF16)
    v = jnp.dot(hb, wqkv_ref[:, 2 * d:3 * d], preferred_element_type=F32)
    vbuf[BAND_ROWS:, :] = v.astype(BF16)

    @pl.when(t_idx >= first_keep_tile)
    def _():
        kk_ref[...] = k
        vk_ref[...] = v

    masks = _head_block_diag()

    def make_chunk(masked):
        def chunk(n, carry):
            r0 = pl.multiple_of(n * CHUNK, CHUNK)
            for g in range(ATT_NGROUPS):
                gsl = slice(g * ATT_GW, (g + 1) * ATT_GW)
                og = _att_chunk_group(q_scr[pl.ds(r0, CHUNK), gsl], kbuf[pl.ds(r0, BAND), gsl],
                                      vbuf[pl.ds(r0, BAND), gsl], bias_ref[g], masks,
                                      r0 - BAND_ROWS if masked else None)
                a_scr[pl.ds(r0, CHUNK), gsl] = og.astype(BF16)
            return carry
        return chunk

    @pl.when(t_idx == 0)
    def _():
        lax.fori_loop(0, tt // CHUNK, make_chunk(True), 0, unroll=4)

    @pl.when(t_idx > 0)
    def _():
        lax.fori_loop(0, tt // CHUNK, make_chunk(False), 0, unroll=8)

    out = jnp.dot(a_scr[...], wo_ref[...], preferred_element_type=F32)
    o_ref[...] = x_ref[...] + gm_ref[0] * out


def _att_fused(x, mod, norm_g4, w_qkv, w_o, pavg, qg, kg, bias, keep, layer, widx, tt, row0):
    b, t, d = x.shape
    assert tt >= BAND_ROWS and tt % CHUNK == 0 and (tt // CHUNK) % 8 == 0
    assert keep % tt == 0 and t % tt == 0
    first_keep_tile = (t - keep) // tt
    xs = pl.BlockSpec((None, tt, d), lambda b_, t_: (b_, t_, 0))
    vec = pl.BlockSpec((None, 1, d), lambda b_, t_: (widx, 0, 0))
    keep_spec = pl.BlockSpec((None, tt, d),
                             lambda b_, t_: (b_, jnp.maximum(t_ - first_keep_tile, 0), 0))
    f32k = jax.ShapeDtypeStruct((b, keep, d), F32)
    return pl.pallas_call(
        functools.partial(_att_fused_kernel, first_keep_tile),
        grid=(b, t // tt),
        in_specs=[
            xs,
            _mod_spec(layer, 3, 1, row0), _mod_spec(layer, 4, 1, row0), _mod_spec(layer, 5, 1, row0),
            _ng_spec(layer, 1),
            _resident((None, d, 3 * d), lambda b_, t_: (widx, 0, 0)),
            _resident((ATT_GW, ATT_GW), lambda b_, t_: (0, 0)),
            vec, vec,
            _resident((ATT_NGROUPS, ATT_GW, BAND), lambda b_, t_: (0, 0, 0)),
            _resident((None, d, d), lambda b_, t_: (widx, 0, 0)),
        ],
        out_specs=[xs, keep_spec, keep_spec],
        out_shape=[jax.ShapeDtypeStruct(x.shape, F32), f32k, f32k],
        scratch_shapes=[pltpu.VMEM((tt, d), BF16), pltpu.VMEM((BAND_ROWS + tt, d), BF16),
                        pltpu.VMEM((BAND_ROWS + tt, d), BF16), pltpu.VMEM((tt, d), BF16)],
        compiler_params=_cparams(2),
        name=f"att_l{layer}",
    )(x, mod, mod, mod, norm_g4, w_qkv, pavg, qg, kg, bias, w_o)


def _att_bias_kernel(rb_ref, o_ref):
    rb = rb_ref[...]
    hi = rb.astype(BF16)
    r1 = rb - hi.astype(F32)
    mid = r1.astype(BF16)
    lo = (r1 - mid.astype(F32)).astype(BF16)
    r = lax.broadcasted_iota(jnp.int32, (REL_SIZE, REL_BASE_W), 0)
    u = lax.broadcasted_iota(jnp.int32, (REL_SIZE, REL_BASE_W), 1)
    rel = jnp.clip(BAND_ROWS + CHUNK - 1 - u, REL_MIN, REL_MAX) - REL_MIN
    sel = jnp.where(r == rel, 1.0, 0.0).astype(BF16)
    base = (jnp.dot(hi, sel, preferred_element_type=F32)
            + jnp.dot(mid, sel, preferred_element_type=F32)
            + jnp.dot(lo, sel, preferred_element_type=F32)) * LOG2E
    for i in range(CHUNK):
        o_ref[i] = base[:, CHUNK - 1 - i:CHUNK - 1 - i + BAND]


def _att_bias_table(rel_bias):
    tab = pl.pallas_call(
        _att_bias_kernel,
        out_shape=jax.ShapeDtypeStruct((CHUNK, ATT_HEADS, BAND), F32),
        name="att_bias",
    )(rel_bias)
    return tab.transpose(1, 0, 2).reshape(ATT_NGROUPS, ATT_GW, BAND)


def _rope_tables(pos0, t):
    inv = ROPE_BASE ** (-np.arange(ROPE_HALF, dtype=np.float64) / ROPE_HALF)
    ang = (pos0 + np.arange(t, dtype=np.float64))[:, None] * inv[None, :]
    return jnp.asarray(np.cos(ang), F32), jnp.asarray(np.sin(ang), F32)


def _trunk(x, mod, row0, pos0, bb, tt, ret_lc, ret_states, pool_bufs, att_k_cache, att_v_cache, p):
    b, t, d = x.shape
    new_pool, new_k, new_v = [], [], []
    ret_all = None
    cos, sin = _rope_tables(pos0, t)
    tt_ffn = FFN_ROWS if (bb == 1 and t % FFN_ROWS == 0) else tt
    bb_ffn = bb
    if tt == t and FFN_ROWS % t == 0:
        wide = min(b, FFN_ROWS // t)
        if b % wide == 0 and row0 % wide == 0:
            bb_ffn = wide
    for i in range(DEPTH):
        x = _ffn(x, mod, p['norm_g'], p['ffn1_w_in'], p['ffn1_w_out'], i, 1, bb_ffn, tt_ffn, row0)
        j = i // N_MIXERS
        kind = i % N_MIXERS
        if kind == 0:
            gn = p['ret_gn_g'][j].reshape(1, 2 * d)
            if ret_states is None:
                x, ret_all = _ret_fused(x, mod, p['norm_g'], p['ret_w_in'], p['ret_w_out'], gn,
                                        cos, sin, ret_all, ret_lc, max(tt, ret_lc),
                                        i, j, row0)
            else:
                q, k, v, g = _ret_proj(x, mod, p['norm_g'], p['ret_w_in'], cos, sin, ret_lc,
                                       i, j, bb, tt, row0)
                y, ret_all = _ret_core(q, k, v, g, ret_states, ret_all, j, gn, ret_lc,
                                       f"ret_core_l{i}")
                x = _matres(y, x, mod, p['ret_w_out'], i, j, bb, tt, row0, f"ret_out_l{i}")
        elif kind == 1:
            if pool_bufs is None:
                buf16 = jnp.zeros((b, POOL_CARRY, d), F32)
            else:
                buf16 = jnp.pad(pool_bufs[j], ((0, 0), (POOL_CARRY - POOL_BUF, 0), (0, 0)))
            if bb_ffn == 1:
                x, nb = _pool_ffn(x, mod, p['norm_g'], buf16, p['pool_w'], p['pool_b'],
                                  p['pool_scale'], p['ffn2_w_in'], p['ffn2_w_out'],
                                  i, j, tt_ffn, row0, pos0)
                new_pool.append(nb)
                continue
            x, nb = _pool(x, mod, p['norm_g'], buf16, p['pool_w'], p['pool_b'], p['pool_scale'],
                          i, j, min(tt, t), row0, pos0)
            new_pool.append(nb)
        else:
            keep = min(BAND_ROWS, t)
            if att_k_cache is None and pos0 == 0 and tt >= BAND_ROWS:
                x, k_keep, v_keep = _att_fused(
                    x, mod, p['norm_g'], p['att_w_qkv'], p['att_w_o'], p['att_pavg'],
                    p['att_q_g'], p['att_k_g'], p['att_bias'][j], keep, i, j, tt, row0)
            else:
                q, kb, vb, k_keep, v_keep = _att_proj(
                    x, mod, p['norm_g'], p['att_w_qkv'], p['att_pavg'], p['att_q_g'],
                    p['att_k_g'], keep, i, j, bb, tt, row0)
                if att_k_cache is None:
                    past_k = past_v = None
                else:
                    past_k = att_k_cache[j].reshape(b, BAND_ROWS, d)
                    past_v = att_v_cache[j].reshape(b, BAND_ROWS, d)
                o = _att_core(q, kb, vb, past_k, past_v, p['att_bias'][j], pos0 - BAND_ROWS,
                              f"att_core_l{i}")
                x = _matres(o, x, mod, p['att_w_o'], i, j, bb, tt, row0, f"att_out_l{i}")
            new_k.append(k_keep.reshape(b, keep, ATT_HEADS, ATT_HD))
            new_v.append(v_keep.reshape(b, keep, ATT_HEADS, ATT_HD))
        x = _ffn(x, mod, p['norm_g'], p['ffn2_w_in'], p['ffn2_w_out'], i, 2, bb_ffn, tt_ffn, row0)
    return x, ret_all, jnp.stack(new_pool), jnp.stack(new_k), jnp.stack(new_v)


def kernel(x_prompt, x_sample, c_prompt, c_sample, state_ret, state_pool, cache_att_k, cache_att_v, norm_g, ada_w, ada_b, ffn1_w_in, ffn1_w_out, ffn2_w_in, ffn2_w_out, ret_w_in, ret_w_out, ret_gn_g, pool_w, pool_b, pool_scale, att_w_qkv, att_w_o, att_q_g, att_k_g, att_rel_bias):
    assert cache_att_k.shape[2] == BAND_ROWS, "key cache must hold exactly the left band"
    bp, tp, d = x_prompt.shape
    bs, ts, _ = x_sample.shape
    n_att = att_w_qkv.shape[0]
    head_id = np.arange(ATT_GW) // ATT_HD
    pavg = jnp.asarray((head_id[:, None] == head_id[None, :]) / ATT_HD, BF16)
    p = {
        'norm_g': norm_g.reshape(DEPTH, 3, 1, d),
        'ffn1_w_in': _interleave_gate_up(ffn1_w_in), 'ffn1_w_out': ffn1_w_out.astype(BF16),
        'ffn2_w_in': _interleave_gate_up(ffn2_w_in), 'ffn2_w_out': ffn2_w_out.astype(BF16),
        'ret_w_in': ret_w_in.astype(BF16), 'ret_w_out': ret_w_out.astype(BF16),
        'ret_gn_g': ret_gn_g,
        'pool_w': pool_w.astype(BF16),
        'pool_b': pool_b.reshape(-1, 1, d), 'pool_scale': pool_scale.reshape(-1, 1, d),
        'att_w_qkv': att_w_qkv.astype(BF16), 'att_w_o': att_w_o.astype(BF16),
        'att_q_g': jnp.tile(att_q_g, (1, ATT_HEADS)).reshape(n_att, 1, d),
        'att_k_g': jnp.tile(att_k_g, (1, ATT_HEADS)).reshape(n_att, 1, d),
        'att_pavg': pavg,
        'att_bias': [_att_bias_table(att_rel_bias[j]) for j in range(n_att)],
    }
    mod = _ada(jnp.concatenate([c_prompt, c_sample], axis=0), ada_w, ada_b)

    tt_p = min(PROMPT_ROWS, tp)
    y_prompt, ret_p, pool_p, k_p, v_p = _trunk(
        x_prompt, mod, 0, 0, 1, tt_p, min(RET_CHUNK_PROMPT, tp), None, None, None, None, p)
    bb_s = max(1, min(bs, PROMPT_ROWS // ts))
    y_sample, ret_s, pool_s, k_s, v_s = _trunk(
        x_sample, mod, bp, PAST_LEN, bb_s, ts, ts, state_ret, state_pool,
        cache_att_k, cache_att_v, p)
    return (y_prompt, y_sample, ret_p, ret_s, pool_p, pool_s, k_p, v_p, k_s, v_s)
```

```python
import functools

import numpy as np
import jax
import jax.numpy as jnp
from jax import lax
from jax.experimental import pallas as pl
from jax.experimental.pallas import tpu as pltpu

F32 = jnp.float32
BF16 = jnp.bfloat16

LANES = 128
D_MODEL = 1024
DEPTH = 4
D_FF = 2816
EPS = 1e-6
CHUNK = 64
PAST_LEN = 2048
N_MIXERS = 3
RET_HEADS = 4
RET_DK = D_MODEL // RET_HEADS
RET_DV = 2 * D_MODEL // RET_HEADS
ROPE_BASE = 10000.0
ROPE_HALF = RET_DK // 2
POOL_WINDOWS = (2, 4, 8, 16)
POOL_GROUPS = 4
POOL_GC = D_MODEL // POOL_GROUPS
POOL_BUF = 15
POOL_CARRY = 16
ATT_HEADS = 16
ATT_HD = D_MODEL // ATT_HEADS
ATT_GROUP = 4
ATT_GW = ATT_GROUP * ATT_HD
ATT_NGROUPS = ATT_HEADS // ATT_GROUP
LEFT_CHUNKS = 8
BAND_ROWS = LEFT_CHUNKS * CHUNK
BAND = BAND_ROWS + CHUNK
REL_MIN = -(CHUNK - 1)
REL_MAX = 256
REL_SIZE = REL_MAX - REL_MIN + 1
REL_BASE_W = -(-(BAND + CHUNK - 1) // LANES) * LANES
NEG_INF = -1e30
LOG2E = 1.4426950408889634
ATT_Q_SCALE = ATT_HD ** -0.5 * LOG2E
N_MOD = 9

V7X_VMEM_BYTES = 64 * 1024 * 1024
VMEM_LIMIT_BYTES = V7X_VMEM_BYTES * 7 // 8
PROMPT_ROWS = 512
FFN_ROWS = 1024
RET_CHUNK_PROMPT = 256
ATT_UNROLL = 8


def _cparams(n_grid):
    return pltpu.CompilerParams(
        dimension_semantics=("arbitrary",) * n_grid,
        vmem_limit_bytes=VMEM_LIMIT_BYTES,
    )


def _resident(block_shape, index_map):
    return pl.BlockSpec(block_shape, index_map, pipeline_mode=pl.Buffered(1))


def _norm_mod(x, ng, sc, sh):
    ms = jnp.mean(x * x, axis=-1, keepdims=True)
    y = x * lax.rsqrt(ms + EPS) * ng
    return y * (1.0 + sc) + sh


def _silu(x):
    return x * (1.0 / (1.0 + jnp.exp(-x)))


def _ada_kernel(c_ref, w_ref, b_ref, o_ref):
    c = _silu(c_ref[...]).astype(BF16)
    w = w_ref[...].astype(BF16)
    o_ref[...] = jnp.dot(c, w, preferred_element_type=F32) + b_ref[...]


def _ada(c_all, ada_w, ada_b):
    n = c_all.shape[0]
    b4 = ada_b.reshape(DEPTH, N_MOD, 1, D_MODEL)
    out = pl.pallas_call(
        _ada_kernel,
        grid=(DEPTH, N_MOD),
        in_specs=[
            pl.BlockSpec((n, D_MODEL), lambda i, j: (0, 0)),
            pl.BlockSpec((None, D_MODEL, D_MODEL), lambda i, j: (i, 0, j)),
            pl.BlockSpec((None, None, 1, D_MODEL), lambda i, j: (i, j, 0, 0)),
        ],
        out_specs=pl.BlockSpec((None, None, n, D_MODEL), lambda i, j: (i, j, 0, 0)),
        out_shape=jax.ShapeDtypeStruct((DEPTH, N_MOD, n, D_MODEL), F32),
        compiler_params=_cparams(2),
        name="ada_mod",
    )(c_all, ada_w, b4)
    return out.reshape(DEPTH, N_MOD, n, 1, D_MODEL)


def _mod_spec(layer, j, bb, row0):
    blk0 = row0 // bb
    return pl.BlockSpec((None, None, bb, 1, D_MODEL),
                        lambda b, t: (layer, j, blk0 + b, 0, 0))


def _ng_spec(layer, j):
    return pl.BlockSpec((None, None, 1, D_MODEL), lambda b, t: (layer, j, 0, 0))


def _x_spec(bb, tt, width=D_MODEL):
    return pl.BlockSpec((bb, tt, width), lambda b, t: (b, t, 0))


GATE_UP_TILES = 2


def _gate_up_kernel(g_ref, u_ref, o_ref):
    for i in range(GATE_UP_TILES):
        src = slice(i * LANES, (i + 1) * LANES)
        o_ref[:, 2 * i * LANES:(2 * i + 1) * LANES] = g_ref[:, src].astype(BF16)
        o_ref[:, (2 * i + 1) * LANES:(2 * i + 2) * LANES] = u_ref[:, src].astype(BF16)


def _interleave_gate_up(w_in):
    nl, d, _ = w_in.shape
    width = GATE_UP_TILES * LANES
    steps = D_FF // width
    assert D_FF % width == 0
    return pl.pallas_call(
        _gate_up_kernel,
        grid=(nl, steps),
        in_specs=[pl.BlockSpec((None, d, width), lambda l, j: (l, 0, j)),
                  pl.BlockSpec((None, d, width), lambda l, j: (l, 0, steps + j))],
        out_specs=pl.BlockSpec((None, d, 2 * width), lambda l, j: (l, 0, j)),
        out_shape=jax.ShapeDtypeStruct((nl, d, 2 * D_FF), BF16),
        compiler_params=_cparams(2),
        name="ffn_gate_up_layout",
    )(w_in, w_in)


def _swiglu(hb, win_ref, wout_ref):
    gu = jnp.dot(hb, win_ref[...], preferred_element_type=F32)
    act = jnp.concatenate(
        [_silu(gu[:, 2 * j * LANES:(2 * j + 1) * LANES]) * gu[:, (2 * j + 1) * LANES:(2 * j + 2) * LANES]
         for j in range(D_FF // LANES)], axis=1).astype(BF16)
    return jnp.dot(act, wout_ref[...], preferred_element_type=F32)


def _ffn_kernel(x_ref, sh_ref, sc_ref, g_ref, ng_ref, win_ref, wout_ref, o_ref):
    bb, tt, d = x_ref.shape
    x = x_ref[...]
    h = _norm_mod(x, ng_ref[...], sc_ref[...], sh_ref[...])
    y = _swiglu(h.reshape(bb * tt, d).astype(BF16), win_ref, wout_ref)
    o_ref[...] = x + (0.5 * g_ref[...]) * y.reshape(bb, tt, d)


def _ffn(x, mod, norm_g4, w_in, w_out, layer, which, bb, tt, row0):
    b, t, d = x.shape
    j0 = 0 if which == 1 else 6
    nj = 0 if which == 1 else 2
    return pl.pallas_call(
        _ffn_kernel,
        grid=(b // bb, t // tt),
        in_specs=[
            _x_spec(bb, tt),
            _mod_spec(layer, j0, bb, row0),
            _mod_spec(layer, j0 + 1, bb, row0),
            _mod_spec(layer, j0 + 2, bb, row0),
            _ng_spec(layer, nj),
            _resident((None, D_MODEL, 2 * D_FF), lambda b_, t_: (layer, 0, 0)),
            _resident((None, D_FF, D_MODEL), lambda b_, t_: (layer, 0, 0)),
        ],
        out_specs=_x_spec(bb, tt),
        out_shape=jax.ShapeDtypeStruct(x.shape, F32),
        compiler_params=_cparams(2),
        name=f"ffn{which}_l{layer}",
    )(x, mod, mod, mod, norm_g4, w_in, w_out)


def _matres_kernel(a_ref, x_ref, g_ref, w_ref, o_ref):
    bb, tt, k = a_ref.shape
    y = jnp.dot(a_ref[...].reshape(bb * tt, k), w_ref[...], preferred_element_type=F32)
    o_ref[...] = x_ref[...] + g_ref[...] * y.reshape(bb, tt, D_MODEL)


def _matres(a, x, mod, w, layer, widx, bb, tt, row0, name):
    b, t, k = a.shape
    return pl.pallas_call(
        _matres_kernel,
        grid=(b // bb, t // tt),
        in_specs=[
            _x_spec(bb, tt, k),
            _x_spec(bb, tt),
            _mod_spec(layer, 5, bb, row0),
            _resident((None, k, D_MODEL), lambda b_, t_: (widx, 0, 0)),
        ],
        out_specs=_x_spec(bb, tt),
        out_shape=jax.ShapeDtypeStruct(x.shape, F32),
        compiler_params=_cparams(2),
        name=name,
    )(a, x, mod, w)


def _ret_proj_kernel(x_ref, sh_ref, sc_ref, ng_ref, w_ref, cos_ref, sin_ref, qs_ref, ks_ref,
                     q_ref, k_ref, v_ref, g_ref):
    bb, tt, d = x_ref.shape
    h = _norm_mod(x_ref[...], ng_ref[...], sc_ref[...], sh_ref[...])
    hb = h.reshape(bb * tt, d).astype(BF16)
    cos = cos_ref[...][None]
    sin = sin_ref[...][None]
    for idx, (o_ref, s_ref) in enumerate(((q_ref, qs_ref), (k_ref, ks_ref))):
        y = jnp.dot(hb, w_ref[:, idx * d:(idx + 1) * d], preferred_element_type=F32)
        y = y.reshape(bb, tt, d)
        for hd in range(RET_HEADS):
            lo = hd * RET_DK
            mid = lo + ROPE_HALF
            x1 = y[:, :, lo:mid]
            x2 = y[:, :, mid:lo + RET_DK]
            o_ref[:, :, lo:mid] = ((x1 * cos - x2 * sin) * s_ref[:, lo:mid][None]).astype(BF16)
            o_ref[:, :, mid:lo + RET_DK] = (
                (x1 * sin + x2 * cos) * s_ref[:, mid:lo + RET_DK][None]).astype(BF16)
    v = jnp.dot(hb, w_ref[:, 2 * d:4 * d], preferred_element_type=F32)
    v_ref[...] = v.reshape(bb, tt, 2 * d).astype(BF16)
    g = jnp.dot(hb, w_ref[:, 4 * d:6 * d], preferred_element_type=F32)
    g_ref[...] = _silu(g).reshape(bb, tt, 2 * d).astype(BF16)


def _ret_gamma():
    return 1.0 - 2.0 ** (-5.0 - np.arange(RET_HEADS, dtype=np.float64))


def _ret_scale_tables(tt, lc):
    gamma = _ret_gamma()
    i = (np.arange(tt) % lc).astype(np.float64)[:, None]
    g_lane = np.repeat(gamma, RET_DK)[None, :]
    qs = g_lane ** (i + 1.0)
    ks = (RET_DK ** -0.5) * g_lane ** (-(i + 1.0))
    return jnp.asarray(qs, F32), jnp.asarray(ks, F32)


def _ret_proj(x, mod, norm_g4, w_in, cos, sin, lc, layer, widx, bb, tt, row0):
    b, t, d = x.shape
    assert tt % lc == 0
    qs, ks = _ret_scale_tables(tt, lc)
    tab = pl.BlockSpec((tt, ROPE_HALF), lambda b_, t_: (t_, 0))
    scale = _resident((tt, d), lambda b_, t_: (0, 0))
    return pl.pallas_call(
        _ret_proj_kernel,
        grid=(b // bb, t // tt),
        in_specs=[
            _x_spec(bb, tt),
            _mod_spec(layer, 3, bb, row0),
            _mod_spec(layer, 4, bb, row0),
            _ng_spec(layer, 1),
            _resident((None, D_MODEL, 6 * D_MODEL), lambda b_, t_: (widx, 0, 0)),
            tab, tab, scale, scale,
        ],
        out_specs=[_x_spec(bb, tt), _x_spec(bb, tt),
                   _x_spec(bb, tt, 2 * d), _x_spec(bb, tt, 2 * d)],
        out_shape=[jax.ShapeDtypeStruct((b, t, d), BF16), jax.ShapeDtypeStruct((b, t, d), BF16),
                   jax.ShapeDtypeStruct((b, t, 2 * d), BF16), jax.ShapeDtypeStruct((b, t, 2 * d), BF16)],
        compiler_params=_cparams(2),
        name=f"ret_proj_l{layer}",
    )(x, mod, mod, norm_g4, w_in, cos, sin, qs, ks)


def _ret_head(qh, kh, vh, s, tril, gn, cdec_h):
    inner = lax.dot_general(qh, kh, (((1,), (1,)), ((), ())), preferred_element_type=F32) * tril
    o = (jnp.dot(inner.astype(BF16), vh, preferred_element_type=F32)
         + jnp.dot(qh, s.astype(BF16), preferred_element_type=F32))
    s_new = (s + lax.dot_general(kh, vh, (((0,), (0,)), ((), ())),
                                 preferred_element_type=F32)) * cdec_h
    mu = jnp.mean(o, axis=-1, keepdims=True)
    oc = o - mu
    var = jnp.mean(oc * oc, axis=-1, keepdims=True)
    return oc * lax.rsqrt(var + EPS) * gn, s_new


def _init_state_stack(s_ref, slot, prev_ref, s0_ref):
    if slot:
        s_ref[0:slot] = prev_ref[...]
    if s0_ref is None:
        s_ref[slot] = jnp.zeros(s_ref.shape[1:], F32)
    else:
        s_ref[slot] = s0_ref[...]


def _ret_core_kernel(cdec, has_state, slot, *refs):
    refs = list(refs)
    q_ref, k_ref, v_ref, g_ref = refs[:4]
    s0_ref = refs[4] if has_state else None
    tril_ref, gn_ref = refs[4 + has_state:6 + has_state]
    prev_ref = refs[6 + has_state] if slot else None
    y_ref, s_ref = refs[6 + has_state + bool(slot):]

    @pl.when(pl.program_id(1) == 0)
    def _():
        _init_state_stack(s_ref, slot, prev_ref, s0_ref)

    for hd in range(RET_HEADS):
        ksl = slice(hd * RET_DK, (hd + 1) * RET_DK)
        vsl = slice(hd * RET_DV, (hd + 1) * RET_DV)
        on, s_new = _ret_head(q_ref[:, ksl], k_ref[:, ksl], v_ref[:, vsl], s_ref[slot, hd],
                              tril_ref[...], gn_ref[:, vsl], cdec[hd])
        s_ref[slot, hd] = s_new
        y_ref[:, vsl] = (g_ref[:, vsl].astype(F32) * on).astype(BF16)


def _state_stack_specs(slot, b):
    blk = lambda n: pl.BlockSpec((n, None, RET_HEADS, RET_DK, RET_DV),
                                 lambda b_, t_: (0, b_, 0, 0, 0))
    shape = jax.ShapeDtypeStruct((slot + 1, b, RET_HEADS, RET_DK, RET_DV), F32)
    return blk(slot + 1), shape, (blk(slot) if slot else None)


def _ret_core(q, k, v, g, s0_all, s_prev, slot, gn_g, lc, name):
    b, t, d = q.shape
    cdec = tuple(float(c) for c in _ret_gamma() ** float(lc))
    tril = jnp.asarray(np.tril(np.ones((lc, lc))), F32)
    has_state = s0_all is not None
    row = lambda width: pl.BlockSpec((None, lc, width), lambda b_, t_: (b_, t_, 0))
    s_spec, s_shape, prev_spec = _state_stack_specs(slot, b)
    in_specs = [row(d), row(d), row(2 * d), row(2 * d)]
    args = [q, k, v, g]
    if has_state:
        in_specs.append(pl.BlockSpec((None, None, RET_HEADS, RET_DK, RET_DV),
                                     lambda b_, t_: (slot, b_, 0, 0, 0)))
        args.append(s0_all)
    in_specs += [_resident((lc, lc), lambda b_, t_: (0, 0)),
                 pl.BlockSpec((1, 2 * d), lambda b_, t_: (0, 0))]
    args += [tril, gn_g]
    if slot:
        in_specs.append(prev_spec)
        args.append(s_prev)
    return pl.pallas_call(
        functools.partial(_ret_core_kernel, cdec, has_state, slot),
        grid=(b, t // lc),
        in_specs=in_specs,
        out_specs=[row(2 * d), s_spec],
        out_shape=[jax.ShapeDtypeStruct((b, t, 2 * d), BF16), s_shape],
        compiler_params=_cparams(2),
        name=name,
    )(*args)


def _ret_fused_kernel(cdec, lc, slot, *refs):
    refs = list(refs)
    (x_ref, sh_ref, sc_ref, gm_ref, ng_ref, win_ref, wout_ref, cos_ref, sin_ref, qs_ref, ks_ref,
     tril_ref, gn_ref) = refs[:13]
    prev_ref = refs[13] if slot else None
    o_ref, s_ref, q_scr, k_scr, v_scr, y_scr = refs[13 + bool(slot):]
    tt, d = x_ref.shape

    @pl.when(pl.program_id(1) == 0)
    def _():
        _init_state_stack(s_ref, slot, prev_ref, None)

    hb = _norm_mod(x_ref[...][None], ng_ref[...], sc_ref[...], sh_ref[...])[0].astype(BF16)
    cos = cos_ref[...]
    sin = sin_ref[...]
    for idx, (scr, tab) in enumerate(((q_scr, qs_ref), (k_scr, ks_ref))):
        y = jnp.dot(hb, win_ref[:, idx * d:(idx + 1) * d], preferred_element_type=F32)
        for hd in range(RET_HEADS):
            lo = hd * RET_DK
            mid = lo + ROPE_HALF
            hi = lo + RET_DK
            x1 = y[:, lo:mid]
            x2 = y[:, mid:hi]
            scr[:, lo:mid] = ((x1 * cos - x2 * sin) * tab[:, lo:mid]).astype(BF16)
            scr[:, mid:hi] = ((x1 * sin + x2 * cos) * tab[:, mid:hi]).astype(BF16)
    v_scr[...] = jnp.dot(hb, win_ref[:, 2 * d:4 * d], preferred_element_type=F32).astype(BF16)
    for hd in range(RET_HEADS):
        ksl = slice(hd * RET_DK, (hd + 1) * RET_DK)
        vsl = slice(hd * RET_DV, (hd + 1) * RET_DV)
        g = jnp.dot(hb, win_ref[:, 4 * d + hd * RET_DV:4 * d + (hd + 1) * RET_DV],
                    preferred_element_type=F32)
        for c in range(tt // lc):
            rows = slice(c * lc, (c + 1) * lc)
            on, s_new = _ret_head(q_scr[rows, ksl], k_scr[rows, ksl], v_scr[rows, vsl],
                                  s_ref[slot, hd], tril_ref[...], gn_ref[:, vsl], cdec[hd])
            s_ref[slot, hd] = s_new
            y_scr[rows, vsl] = (_silu(g[rows]) * on).astype(BF16)
    out = jnp.dot(y_scr[...], wout_ref[...], preferred_element_type=F32)
    o_ref[...] = x_ref[...] + gm_ref[0] * out


def _ret_fused(x, mod, norm_g4, w_in, w_out, gn_g, cos, sin, s_prev, lc, tt, layer, widx, row0):
    b, t, d = x.shape
    assert tt % lc == 0 and t % tt == 0
    cdec = tuple(float(c) for c in _ret_gamma() ** float(lc))
    tril = jnp.asarray(np.tril(np.ones((lc, lc))), F32)
    qs, ks = _ret_scale_tables(tt, lc)
    slot = widx
    xs = pl.BlockSpec((None, tt, d), lambda b_, t_: (b_, t_, 0))
    tab = pl.BlockSpec((tt, ROPE_HALF), lambda b_, t_: (t_, 0))
    const2 = lambda shape: _resident(shape, lambda b_, t_: (0, 0))
    s_spec, s_shape, prev_spec = _state_stack_specs(slot, b)
    in_specs = [
        xs,
        _mod_spec(layer, 3, 1, row0), _mod_spec(layer, 4, 1, row0), _mod_spec(layer, 5, 1, row0),
        _ng_spec(layer, 1),
        _resident((None, d, 6 * d), lambda b_, t_: (widx, 0, 0)),
        _resident((None, 2 * d, d), lambda b_, t_: (widx, 0, 0)),
        tab, tab, const2((tt, d)), const2((tt, d)), const2((lc, lc)),
        pl.BlockSpec((1, 2 * d), lambda b_, t_: (0, 0)),
    ]
    args = [x, mod, mod, mod, norm_g4, w_in, w_out, cos, sin, qs, ks, tril, gn_g]
    if slot:
        in_specs.append(prev_spec)
        args.append(s_prev)
    return pl.pallas_call(
        functools.partial(_ret_fused_kernel, cdec, lc, slot),
        grid=(b, t // tt),
        in_specs=in_specs,
        out_specs=[xs, s_spec],
        out_shape=[jax.ShapeDtypeStruct(x.shape, F32), s_shape],
        scratch_shapes=[pltpu.VMEM((tt, d), BF16), pltpu.VMEM((tt, d), BF16),
                        pltpu.VMEM((tt, 2 * d), BF16), pltpu.VMEM((tt, 2 * d), BF16)],
        compiler_params=_cparams(2),
        name=f"ret_l{layer}",
    )(*args)


def _pool_mix(pos0, x, sh_ref, sc_ref, gm_ref, ng_ref, buf_ref, w_ref, pb_ref, ps_ref,
              nb_ref, carry_ref):
    tt, d = x.shape
    t_idx = pl.program_id(1)

    @pl.when(t_idx == 0)
    def _():
        carry_ref[...] = buf_ref[...]

    h = _norm_mod(x[None], ng_ref[...], sc_ref[...], sh_ref[...])[0]
    ext = jnp.concatenate([carry_ref[...], h], axis=0)
    carry_ref[...] = h[tt - POOL_CARRY:, :]
    nb_ref[...] = h[tt - POOL_BUF:, :]

    pos = pos0 + t_idx * tt + lax.broadcasted_iota(jnp.int32, (tt, 1), 0)
    run = ext
    w = 1
    outs = []
    for gi, win in enumerate(POOL_WINDOWS):
        while w < win:
            n = run.shape[0]
            run = run[w:, :] + run[:n - w, :]
            w *= 2
        wsum = run[run.shape[0] - tt:, :POOL_GC]
        inv_cnt = 1.0 / jnp.minimum(pos + 1, win).astype(F32)
        lo = gi * POOL_GC
        pooled = wsum * inv_cnt - h[:, lo:lo + POOL_GC]
        outs.append(jnp.dot(pooled.astype(BF16), w_ref[gi], preferred_element_type=F32))
        run = run[:, POOL_GC:]
    y = (jnp.concatenate(outs, axis=1) + pb_ref[...]) * ps_ref[...]
    return x + gm_ref[0] * y


def _pool_kernel(pos0, x_ref, sh_ref, sc_ref, gm_ref, ng_ref, buf_ref, w_ref, pb_ref, ps_ref,
                 o_ref, nb_ref, carry_ref):
    o_ref[...] = _pool_mix(pos0, x_ref[...], sh_ref, sc_ref, gm_ref, ng_ref, buf_ref, w_ref,
                           pb_ref, ps_ref, nb_ref, carry_ref)


def _pool_ffn_kernel(pos0, x_ref, sh_ref, sc_ref, gm_ref, ng_ref, buf_ref, w_ref, pb_ref, ps_ref,
                     fsh_ref, fsc_ref, fg_ref, fng_ref, win_ref, wout_ref,
                     o_ref, nb_ref, carry_ref):
    x = _pool_mix(pos0, x_ref[...], sh_ref, sc_ref, gm_ref, ng_ref, buf_ref, w_ref,
                  pb_ref, ps_ref, nb_ref, carry_ref)
    hb = _norm_mod(x[None], fng_ref[...], fsc_ref[...], fsh_ref[...])[0].astype(BF16)
    o_ref[...] = x + (0.5 * fg_ref[0]) * _swiglu(hb, win_ref, wout_ref)


def _pool_specs(x, layer, widx, tt, row0):
    b, t, d = x.shape
    xs = pl.BlockSpec((None, tt, d), lambda b_, t_: (b_, t_, 0))
    vec = pl.BlockSpec((None, 1, d), lambda b_, t_: (widx, 0, 0))
    in_specs = [
        xs,
        _mod_spec(layer, 3, 1, row0),
        _mod_spec(layer, 4, 1, row0),
        _mod_spec(layer, 5, 1, row0),
        _ng_spec(layer, 1),
        pl.BlockSpec((None, POOL_CARRY, d), lambda b_, t_: (b_, 0, 0)),
        pl.BlockSpec((None, POOL_GROUPS, POOL_GC, POOL_GC), lambda b_, t_: (widx, 0, 0, 0)),
        vec, vec,
    ]
    out_specs = [xs, pl.BlockSpec((None, POOL_BUF, d), lambda b_, t_: (b_, 0, 0))]
    out_shape = [jax.ShapeDtypeStruct(x.shape, F32), jax.ShapeDtypeStruct((b, POOL_BUF, d), F32)]
    return in_specs, out_specs, out_shape


def _pool(x, mod, norm_g4, buf16, w, pb, ps, layer, widx, tt, row0, pos0):
    b, t, d = x.shape
    in_specs, out_specs, out_shape = _pool_specs(x, layer, widx, tt, row0)
    return pl.pallas_call(
        functools.partial(_pool_kernel, pos0),
        grid=(b, t // tt),
        in_specs=in_specs,
        out_specs=out_specs,
        out_shape=out_shape,
        scratch_shapes=[pltpu.VMEM((POOL_CARRY, d), F32)],
        compiler_params=_cparams(2),
        name=f"pool_l{layer}",
    )(x, mod, mod, mod, norm_g4, buf16, w, pb, ps)


def _pool_ffn(x, mod, norm_g4, buf16, w, pb, ps, w_in, w_out, layer, widx, tt, row0, pos0):
    b, t, d = x.shape
    in_specs, out_specs, out_shape = _pool_specs(x, layer, widx, tt, row0)
    in_specs += [
        _mod_spec(layer, 6, 1, row0), _mod_spec(layer, 7, 1, row0), _mod_spec(layer, 8, 1, row0),
        _ng_spec(layer, 2),
        _resident((None, D_MODEL, 2 * D_FF), lambda b_, t_: (layer, 0, 0)),
        _resident((None, D_FF, D_MODEL), lambda b_, t_: (layer, 0, 0)),
    ]
    return pl.pallas_call(
        functools.partial(_pool_ffn_kernel, pos0),
        grid=(b, t // tt),
        in_specs=in_specs,
        out_specs=out_specs,
        out_shape=out_shape,
        scratch_shapes=[pltpu.VMEM((POOL_CARRY, d), F32)],
        compiler_params=_cparams(2),
        name=f"pool_ffn2_l{layer}",
    )(x, mod, mod, mod, norm_g4, buf16, w, pb, ps, mod, mod, mod, norm_g4, w_in, w_out)


def _head_norm(y, pavg, gain):
    y2 = (y * y).astype(BF16)
    ms = jnp.concatenate(
        [jnp.dot(y2[:, g * ATT_GW:(g + 1) * ATT_GW], pavg, preferred_element_type=F32)
         for g in range(ATT_NGROUPS)], axis=1)
    return y * lax.rsqrt(ms + EPS) * gain


def _att_proj_kernel(first_keep_tile, x_ref, sh_ref, sc_ref, ng_ref, w_ref, p_ref, qg_ref, kg_ref,
                     q_ref, kb_ref, vb_ref, kk_ref, vk_ref):
    bb, tt, d = x_ref.shape
    h = _norm_mod(x_ref[...], ng_ref[...], sc_ref[...], sh_ref[...])
    hb = h.reshape(bb * tt, d).astype(BF16)
    q = jnp.dot(hb, w_ref[:, 0:d], preferred_element_type=F32)
    q = _head_norm(q, p_ref[...], qg_ref[...]) * ATT_Q_SCALE
    q_ref[...] = q.reshape(bb, tt, d).astype(BF16)
    k = jnp.dot(hb, w_ref[:, d:2 * d], preferred_element_type=F32)
    k = _head_norm(k, p_ref[...], kg_ref[...]).reshape(bb, tt, d)
    kb_ref[...] = k.astype(BF16)
    v = jnp.dot(hb, w_ref[:, 2 * d:3 * d], preferred_element_type=F32).reshape(bb, tt, d)
    vb_ref[...] = v.astype(BF16)

    @pl.when(pl.program_id(1) >= first_keep_tile)
    def _():
        kk_ref[...] = k
        vk_ref[...] = v


def _att_proj(x, mod, norm_g4, w_qkv, pavg, qg, kg, keep, layer, widx, bb, tt, row0):
    b, t, d = x.shape
    assert keep % tt == 0 and t % tt == 0
    first_keep_tile = (t - keep) // tt
    vec = pl.BlockSpec((None, 1, d), lambda b_, t_: (widx, 0, 0))
    keep_spec = pl.BlockSpec((bb, tt, d),
                             lambda b_, t_: (b_, jnp.maximum(t_ - first_keep_tile, 0), 0))
    b16o = jax.ShapeDtypeStruct((b, t, d), BF16)
    f32k = jax.ShapeDtypeStruct((b, keep, d), F32)
    return pl.pallas_call(
        functools.partial(_att_proj_kernel, first_keep_tile),
        grid=(b // bb, t // tt),
        in_specs=[
            _x_spec(bb, tt),
            _mod_spec(layer, 3, bb, row0),
            _mod_spec(layer, 4, bb, row0),
            _ng_spec(layer, 1),
            _resident((None, d, 3 * d), lambda b_, t_: (widx, 0, 0)),
            _resident((ATT_GW, ATT_GW), lambda b_, t_: (0, 0)),
            vec, vec,
        ],
        out_specs=[_x_spec(bb, tt)] * 3 + [keep_spec] * 2,
        out_shape=[b16o, b16o, b16o, f32k, f32k],
        compiler_params=_cparams(2),
        name=f"att_proj_l{layer}",
    )(x, mod, mod, norm_g4, w_qkv, pavg, qg, kg)


def _head_block_diag():
    rb = lax.broadcasted_iota(jnp.int32, (ATT_GW, ATT_GW), 0) // ATT_HD
    cb = lax.broadcasted_iota(jnp.int32, (ATT_GW, ATT_GW), 1) // ATT_HD
    lane_head = lax.broadcasted_iota(jnp.int32, (CHUNK, ATT_GW), 1) // ATT_HD
    return rb == cb, lane_head


def _att_chunk_group(qc, kb, vb, bias, masks, first_kpos):
    diag, lane_head = masks
    qbd = jnp.where(diag, jnp.concatenate([qc] * ATT_GROUP, axis=0), jnp.zeros((), BF16))
    s = lax.dot_general(qbd, kb, (((1,), (1,)), ((), ())), preferred_element_type=F32) + bias
    if first_kpos is not None:
        kpos = first_kpos + lax.broadcasted_iota(jnp.int32, (1, BAND), 1)
        s = jnp.where(kpos >= 0, s, NEG_INF)
    m = jnp.max(s, axis=-1, keepdims=True)
    p = jnp.exp2(s - m)
    inv = 1.0 / jnp.sum(p, axis=-1, keepdims=True)
    o = jnp.dot(p.astype(BF16), vb, preferred_element_type=F32)
    last = ATT_GROUP - 1
    og = o[last * CHUNK:] * inv[last * CHUNK:]
    for hh in range(last - 1, -1, -1):
        rows = slice(hh * CHUNK, (hh + 1) * CHUNK)
        og = jnp.where(lane_head == hh, o[rows] * inv[rows], og)
    return og


def _att_core_kernel(first_pos, has_past, *refs):
    if has_past:
        q_ref, k_ref, v_ref, pk_ref, pv_ref, bias_ref, o_ref, kbuf, vbuf = refs
    else:
        q_ref, k_ref, v_ref, bias_ref, o_ref, kbuf, vbuf = refs
    t = q_ref.shape[0]
    if has_past:
        kbuf[0:BAND_ROWS, :] = pk_ref[...].astype(BF16)
        vbuf[0:BAND_ROWS, :] = pv_ref[...].astype(BF16)
    else:
        kbuf[0:BAND_ROWS, :] = jnp.zeros((BAND_ROWS, ATT_GW), BF16)
        vbuf[0:BAND_ROWS, :] = jnp.zeros((BAND_ROWS, ATT_GW), BF16)
    kbuf[BAND_ROWS:, :] = k_ref[...]
    vbuf[BAND_ROWS:, :] = v_ref[...]
    masks = _head_block_diag()

    def make_chunk(masked):
        def chunk(n, carry):
            r0 = pl.multiple_of(n * CHUNK, CHUNK)
            og = _att_chunk_group(q_ref[pl.ds(r0, CHUNK), :], kbuf[pl.ds(r0, BAND), :],
                                  vbuf[pl.ds(r0, BAND), :], bias_ref[...], masks,
                                  first_pos + r0 if masked else None)
            o_ref[pl.ds(r0, CHUNK), :] = og.astype(BF16)
            return carry
        return chunk

    n_chunks = t // CHUNK
    n_masked = min(n_chunks, max(0, -(first_pos // CHUNK)))
    for lo, hi, masked in ((0, n_masked, True), (n_masked, n_chunks, False)):
        if hi > lo:
            trips = hi - lo
            unroll = next(u for u in (ATT_UNROLL, 2, 1) if trips % u == 0)
            lax.fori_loop(lo, hi, make_chunk(masked), 0, unroll=unroll)


def _att_core(q, k, v, past_k, past_v, bias, first_pos, name):
    b, t, d = q.shape
    has_past = past_k is not None
    grp = lambda rows: pl.BlockSpec((None, rows, ATT_GW), lambda b_, g_: (b_, 0, g_))
    in_specs = [grp(t), grp(t), grp(t)]
    args = [q, k, v]
    if has_past:
        in_specs += [grp(BAND_ROWS), grp(BAND_ROWS)]
        args += [past_k, past_v]
    in_specs.append(pl.BlockSpec((None, ATT_GW, BAND), lambda b_, g_: (g_, 0, 0)))
    args.append(bias)
    return pl.pallas_call(
        functools.partial(_att_core_kernel, first_pos, has_past),
        grid=(b, ATT_NGROUPS),
        in_specs=in_specs,
        out_specs=grp(t),
        out_shape=jax.ShapeDtypeStruct((b, t, d), BF16),
        scratch_shapes=[pltpu.VMEM((BAND_ROWS + t, ATT_GW), BF16)] * 2,
        compiler_params=_cparams(2),
        name=name,
    )(*args)


def _att_fused_kernel(first_keep_tile, x_ref, sh_ref, sc_ref, gm_ref, ng_ref, wqkv_ref, p_ref,
                      qg_ref, kg_ref, bias_ref, wo_ref, o_ref, kk_ref, vk_ref,
                      q_scr, kbuf, vbuf, a_scr):
    tt, d = x_ref.shape
    t_idx = pl.program_id(1)

    @pl.when(t_idx == 0)
    def _():
        kbuf[0:BAND_ROWS, :] = jnp.zeros((BAND_ROWS, d), BF16)
        vbuf[0:BAND_ROWS, :] = jnp.zeros((BAND_ROWS, d), BF16)

    @pl.when(t_idx > 0)
    def _():
        kbuf[0:BAND_ROWS, :] = kbuf[tt:tt + BAND_ROWS, :]
        vbuf[0:BAND_ROWS, :] = vbuf[tt:tt + BAND_ROWS, :]

    hb = _norm_mod(x_ref[...][None], ng_ref[...], sc_ref[...], sh_ref[...])[0].astype(BF16)
    q = jnp.dot(hb, wqkv_ref[:, 0:d], preferred_element_type=F32)
    q_scr[...] = (_head_norm(q, p_ref[...], qg_ref[...]) * ATT_Q_SCALE).astype(BF16)
    k = jnp.dot(hb, wqkv_ref[:, d:2 * d], preferred_element_type=F32)
    k = _head_norm(k, p_ref[...], kg_ref[...])
    kbuf[BAND_ROWS:, :] = k.astype(BF16)
    v = jnp.dot(hb, wqkv_ref[:, 2 * d:3 * d], preferred_element_type=F32)
    vbuf[BAND_ROWS:, :] = v.astype(BF16)

    @pl.when(t_idx >= first_keep_tile)
    def _():
        kk_ref[...] = k
        vk_ref[...] = v

    masks = _head_block_diag()

    def make_chunk(masked):
        def chunk(n, carry):
            r0 = pl.multiple_of(n * CHUNK, CHUNK)
            for g in range(ATT_NGROUPS):
                gsl = slice(g * ATT_GW, (g + 1) * ATT_GW)
                og = _att_chunk_group(q_scr[pl.ds(r0, CHUNK), gsl], kbuf[pl.ds(r0, BAND), gsl],
                                      vbuf[pl.ds(r0, BAND), gsl], bias_ref[g], masks,
                                      r0 - BAND_ROWS if masked else None)
                a_scr[pl.ds(r0, CHUNK), gsl] = og.astype(BF16)
            return carry
        return chunk

    n_chunks = tt // CHUNK
    @pl.when(t_idx == 0)
    def _():
        lax.fori_loop(0, n_chunks, make_chunk(True), 0, unroll=n_chunks // 2)

    @pl.when(t_idx > 0)
    def _():
        lax.fori_loop(0, n_chunks, make_chunk(False), 0, unroll=n_chunks)

    out = jnp.dot(a_scr[...], wo_ref[...], preferred_element_type=F32)
    o_ref[...] = x_ref[...] + gm_ref[0] * out


def _att_fused(x, mod, norm_g4, w_qkv, w_o, pavg, qg, kg, bias, keep, layer, widx, tt, row0):
    b, t, d = x.shape
    assert tt >= BAND_ROWS and tt % (2 * CHUNK) == 0
    assert keep % tt == 0 and t % tt == 0
    first_keep_tile = (t - keep) // tt
    xs = pl.BlockSpec((None, tt, d), lambda b_, t_: (b_, t_, 0))
    vec = pl.BlockSpec((None, 1, d), lambda b_, t_: (widx, 0, 0))
    keep_spec = pl.BlockSpec((None, tt, d),
                             lambda b_, t_: (b_, jnp.maximum(t_ - first_keep_tile, 0), 0))
    f32k = jax.ShapeDtypeStruct((b, keep, d), F32)
    return pl.pallas_call(
        functools.partial(_att_fused_kernel, first_keep_tile),
        grid=(b, t // tt),
        in_specs=[
            xs,
            _mod_spec(layer, 3, 1, row0), _mod_spec(layer, 4, 1, row0), _mod_spec(layer, 5, 1, row0),
            _ng_spec(layer, 1),
            _resident((None, d, 3 * d), lambda b_, t_: (widx, 0, 0)),
            _resident((ATT_GW, ATT_GW), lambda b_, t_: (0, 0)),
            vec, vec,
            _resident((ATT_NGROUPS, ATT_GW, BAND), lambda b_, t_: (0, 0, 0)),
            _resident((None, d, d), lambda b_, t_: (widx, 0, 0)),
        ],
        out_specs=[xs, keep_spec, keep_spec],
        out_shape=[jax.ShapeDtypeStruct(x.shape, F32), f32k, f32k],
        scratch_shapes=[pltpu.VMEM((tt, d), BF16), pltpu.VMEM((BAND_ROWS + tt, d), BF16),
                        pltpu.VMEM((BAND_ROWS + tt, d), BF16), pltpu.VMEM((tt, d), BF16)],
        compiler_params=_cparams(2),
        name=f"att_l{layer}",
    )(x, mod, mod, mod, norm_g4, w_qkv, pavg, qg, kg, bias, w_o)


def _att_bias_kernel(rb_ref, o_ref):
    rb = rb_ref[...]
    hi = rb.astype(BF16)
    r1 = rb - hi.astype(F32)
    mid = r1.astype(BF16)
    lo = (r1 - mid.astype(F32)).astype(BF16)
    r = lax.broadcasted_iota(jnp.int32, (REL_SIZE, REL_BASE_W), 0)
    u = lax.broadcasted_iota(jnp.int32, (REL_SIZE, REL_BASE_W), 1)
    rel = jnp.clip(BAND_ROWS + CHUNK - 1 - u, REL_MIN, REL_MAX) - REL_MIN
    sel = jnp.where(r == rel, 1.0, 0.0).astype(BF16)
    base = (jnp.dot(hi, sel, preferred_element_type=F32)
            + jnp.dot(mid, sel, preferred_element_type=F32)
            + jnp.dot(lo, sel, preferred_element_type=F32)) * LOG2E
    for i in range(CHUNK):
        o_ref[i] = base[:, CHUNK - 1 - i:CHUNK - 1 - i + BAND]


def _att_bias_table(rel_bias):
    tab = pl.pallas_call(
        _att_bias_kernel,
        out_shape=jax.ShapeDtypeStruct((CHUNK, ATT_HEADS, BAND), F32),
        name="att_bias",
    )(rel_bias)
    return tab.transpose(1, 0, 2).reshape(ATT_NGROUPS, ATT_GW, BAND)


def _rope_tables(pos0, t):
    inv = ROPE_BASE ** (-np.arange(ROPE_HALF, dtype=np.float64) / ROPE_HALF)
    ang = (pos0 + np.arange(t, dtype=np.float64))[:, None] * inv[None, :]
    return jnp.asarray(np.cos(ang), F32), jnp.asarray(np.sin(ang), F32)


def _trunk(x, mod, row0, pos0, bb, tt, ret_lc, ret_states, pool_bufs, att_k_cache, att_v_cache, p):
    b, t, d = x.shape
    new_pool, new_k, new_v = [], [], []
    ret_all = None
    cos, sin = _rope_tables(pos0, t)
    tt_ffn = FFN_ROWS if (bb == 1 and t % FFN_ROWS == 0) else tt
    bb_ffn = bb
    if tt == t and FFN_ROWS % t == 0:
        wide = min(b, FFN_ROWS // t)
        if b % wide == 0 and row0 % wide == 0:
            bb_ffn = wide
    for i in range(DEPTH):
        x = _ffn(x, mod, p['norm_g'], p['ffn1_w_in'], p['ffn1_w_out'], i, 1, bb_ffn, tt_ffn, row0)
        j = i // N_MIXERS
        kind = i % N_MIXERS
        if kind == 0:
            gn = p['ret_gn_g'][j].reshape(1, 2 * d)
            if ret_states is None:
                x, ret_all = _ret_fused(x, mod, p['norm_g'], p['ret_w_in'], p['ret_w_out'], gn,
                                        cos, sin, ret_all, ret_lc, max(tt, ret_lc),
                                        i, j, row0)
            else:
                q, k, v, g = _ret_proj(x, mod, p['norm_g'], p['ret_w_in'], cos, sin, ret_lc,
                                       i, j, bb, tt, row0)
                y, ret_all = _ret_core(q, k, v, g, ret_states, ret_all, j, gn, ret_lc,
                                       f"ret_core_l{i}")
                x = _matres(y, x, mod, p['ret_w_out'], i, j, bb, tt, row0, f"ret_out_l{i}")
        elif kind == 1:
            if pool_bufs is None:
                buf16 = jnp.zeros((b, POOL_CARRY, d), F32)
            else:
                buf16 = jnp.pad(pool_bufs[j], ((0, 0), (POOL_CARRY - POOL_BUF, 0), (0, 0)))
            if bb_ffn == 1:
                x, nb = _pool_ffn(x, mod, p['norm_g'], buf16, p['pool_w'], p['pool_b'],
                                  p['pool_scale'], p['ffn2_w_in'], p['ffn2_w_out'],
                                  i, j, tt_ffn, row0, pos0)
                new_pool.append(nb)
                continue
            x, nb = _pool(x, mod, p['norm_g'], buf16, p['pool_w'], p['pool_b'], p['pool_scale'],
                          i, j, min(tt, t), row0, pos0)
            new_pool.append(nb)
        else:
            keep = min(BAND_ROWS, t)
            if att_k_cache is None and pos0 == 0 and tt >= BAND_ROWS:
                x, k_keep, v_keep = _att_fused(
                    x, mod, p['norm_g'], p['att_w_qkv'], p['att_w_o'], p['att_pavg'],
                    p['att_q_g'], p['att_k_g'], p['att_bias'][j], keep, i, j, tt, row0)
            else:
                q, kb, vb, k_keep, v_keep = _att_proj(
                    x, mod, p['norm_g'], p['att_w_qkv'], p['att_pavg'], p['att_q_g'],
                    p['att_k_g'], keep, i, j, bb, tt, row0)
                if att_k_cache is None:
                    past_k = past_v = None
                else:
                    past_k = att_k_cache[j].reshape(b, BAND_ROWS, d)
                    past_v = att_v_cache[j].reshape(b, BAND_ROWS, d)
                o = _att_core(q, kb, vb, past_k, past_v, p['att_bias'][j], pos0 - BAND_ROWS,
                              f"att_core_l{i}")
                x = _matres(o, x, mod, p['att_w_o'], i, j, bb, tt, row0, f"att_out_l{i}")
            new_k.append(k_keep.reshape(b, keep, ATT_HEADS, ATT_HD))
            new_v.append(v_keep.reshape(b, keep, ATT_HEADS, ATT_HD))
        x = _ffn(x, mod, p['norm_g'], p['ffn2_w_in'], p['ffn2_w_out'], i, 2, bb_ffn, tt_ffn, row0)
    return x, ret_all, jnp.stack(new_pool), jnp.stack(new_k), jnp.stack(new_v)


def kernel(x_prompt, x_sample, c_prompt, c_sample, state_ret, state_pool, cache_att_k, cache_att_v, norm_g, ada_w, ada_b, ffn1_w_in, ffn1_w_out, ffn2_w_in, ffn2_w_out, ret_w_in, ret_w_out, ret_gn_g, pool_w, pool_b, pool_scale, att_w_qkv, att_w_o, att_q_g, att_k_g, att_rel_bias):
    assert cache_att_k.shape[2] == BAND_ROWS, "key cache must hold exactly the left band"
    bp, tp, d = x_prompt.shape
    bs, ts, _ = x_sample.shape
    n_att = att_w_qkv.shape[0]
    head_id = np.arange(ATT_GW) // ATT_HD
    pavg = jnp.asarray((head_id[:, None] == head_id[None, :]) / ATT_HD, BF16)
    p = {
        'norm_g': norm_g.reshape(DEPTH, 3, 1, d),
        'ffn1_w_in': _interleave_gate_up(ffn1_w_in), 'ffn1_w_out': ffn1_w_out.astype(BF16),
        'ffn2_w_in': _interleave_gate_up(ffn2_w_in), 'ffn2_w_out': ffn2_w_out.astype(BF16),
        'ret_w_in': ret_w_in.astype(BF16), 'ret_w_out': ret_w_out.astype(BF16),
        'ret_gn_g': ret_gn_g,
        'pool_w': pool_w.astype(BF16),
        'pool_b': pool_b.reshape(-1, 1, d), 'pool_scale': pool_scale.reshape(-1, 1, d),
        'att_w_qkv': att_w_qkv.astype(BF16), 'att_w_o': att_w_o.astype(BF16),
        'att_q_g': jnp.tile(att_q_g, (1, ATT_HEADS)).reshape(n_att, 1, d),
        'att_k_g': jnp.tile(att_k_g, (1, ATT_HEADS)).reshape(n_att, 1, d),
        'att_pavg': pavg,
        'att_bias': [_att_bias_table(att_rel_bias[j]) for j in range(n_att)],
    }
    mod = _ada(jnp.concatenate([c_prompt, c_sample], axis=0), ada_w, ada_b)

    tt_p = min(PROMPT_ROWS, tp)
    y_prompt, ret_p, pool_p, k_p, v_p = _trunk(
        x_prompt, mod, 0, 0, 1, tt_p, min(RET_CHUNK_PROMPT, tp), None, None, None, None, p)
    bb_s = max(1, min(bs, PROMPT_ROWS // ts))
    y_sample, ret_s, pool_s, k_s, v_s = _trunk(
        x_sample, mod, bp, PAST_LEN, bb_s, ts, ts, state_ret, state_pool,
        cache_att_k, cache_att_v, p)
    return (y_prompt, y_sample, ret_p, ret_s, pool_p, pool_s, k_p, v_p, k_s, v_s)
```
